```python
import math
import jax
import jax.numpy as jnp
from jax import lax
import numpy as np

D_MODEL = 1024
BATCH = 32
SEQ = 2048
DEPTH = 2

CTX_LEN = 256
GRID_W = 64
BLOCK_Q = 128
ROPE_THETA = 10000.0
EPS = 1e-6

GROUP_W = D_MODEL // 4
D_MIX = 4 * GROUP_W

POOL_WINDOWS = (2, 4, 8, 16)
POOL_GC = GROUP_W // len(POOL_WINDOWS)

GQA_HEADS = 4
GQA_KV_HEADS = 2
GQA_HD = GROUP_W // GQA_HEADS

HY_ORDER = 2
HY_C = GROUP_W
HY_BANDS = 8
HY_EMB = 1 + 2 * HY_BANDS
HY_FW = 64
HY_DIRS = 2
HY_DECAY_TARGET = 1e-2
HY_FAST_PCT = 0.3
HY_SLOW_PCT = 1.5

MLA_HEADS = 4
MLA_Q_RANK = D_MODEL // 4
MLA_KV_RANK = D_MODEL // 8
MLA_NOPE = GROUP_W // MLA_HEADS
MLA_ROPE = MLA_NOPE // 2
MLA_V = GROUP_W // MLA_HEADS

N_EXPERTS = 16
EC_CAPACITY = 2
EXPERT_FF = D_MODEL // 2

COL_SIZES = (GQA_KV_HEADS * GQA_HD, GQA_KV_HEADS * GQA_HD, MLA_KV_RANK, MLA_ROPE,
             GQA_HEADS * GQA_HD, MLA_Q_RANK, GROUP_W, (HY_ORDER + 1) * HY_C)
COL_SPLITS = tuple(sum(COL_SIZES[:i + 1]) for i in range(len(COL_SIZES) - 1))
IN_COLS = sum(COL_SIZES)
CTX_KV_COLS = sum(COL_SIZES[:4])

kernel_name = 'hybrid_parallel_group_dit_block'


def rms_norm(x, g):
    xf = x.astype(jnp.float32)
    y = xf * lax.rsqrt(jnp.mean(xf * xf, axis=-1, keepdims=True) + EPS)
    return (y * g.astype(jnp.float32)).astype(x.dtype)


def modulate(h, shift, scale):
    return h * (1 + scale) + shift


def _rotate(x, pos):
    m = x.shape[-1] // 2
    inv = ROPE_THETA ** (-jnp.arange(m, dtype=jnp.float32) / m)
    ang = pos[:, None] * inv[None, :]
    cos = jnp.cos(ang)[None, :, None, :].astype(x.dtype)
    sin = jnp.sin(ang)[None, :, None, :].astype(x.dtype)
    x1, x2 = x[..., :m], x[..., m:]
    return jnp.concatenate([x1 * cos - x2 * sin, x1 * sin + x2 * cos], axis=-1)


def axial_rope(x, row, col):
    half = x.shape[-1] // 2
    return jnp.concatenate([_rotate(x[..., :half], row), _rotate(x[..., half:], col)], axis=-1)


def attend(q, k, v, scale):
    s = jnp.einsum('bqhgd,bkhd->bhgqk', q, k).astype(jnp.float32) * scale
    p = jax.nn.softmax(s, axis=-1).astype(v.dtype)
    return jnp.einsum('bhgqk,bkhd->bqhgd', p, v)


def block_sweep_attention(q, k, v, scale):
    b, n, kvh, g, dk = q.shape
    nb = n // BLOCK_Q
    qb = jnp.moveaxis(q.reshape(b, nb, BLOCK_Q, kvh, g, dk), 1, 0)
    ob = lax.map(lambda qi: attend(qi, k, v, scale), qb)
    return jnp.moveaxis(ob, 0, 1).reshape(b, n, -1)


def gqa_q(q_raw, g_q, pos):
    b, n, _ = q_raw.shape
    q = rms_norm(q_raw.reshape(b, n, GQA_HEADS, GQA_HD), g_q)
    if pos is not None:
        q = axial_rope(q, *pos)
    return q.reshape(b, n, GQA_KV_HEADS, GQA_HEADS // GQA_KV_HEADS, GQA_HD)


def gqa_kv(k_raw, v_raw, g_k, pos):
    b, n, _ = k_raw.shape
    k = rms_norm(k_raw.reshape(b, n, GQA_KV_HEADS, GQA_HD), g_k)
    if pos is not None:
        k = axial_rope(k, *pos)
    return k, v_raw.reshape(b, n, GQA_KV_HEADS, GQA_HD)


def mla_q(cq, g_cq, w_uq, g_mq, pos):
    b, n, _ = cq.shape
    q = (rms_norm(cq, g_cq) @ w_uq).reshape(b, n, MLA_HEADS, MLA_NOPE + MLA_ROPE)
    q = rms_norm(q, g_mq)
    if pos is not None:
        q = jnp.concatenate([q[..., :MLA_NOPE], axial_rope(q[..., MLA_NOPE:], *pos)], axis=-1)
    return q[:, :, :, None, :]


def mla_kv(ckv, kr, g_ckv, w_ukv, g_mk, pos):
    b, n, _ = ckv.shape
    kv = (rms_norm(ckv, g_ckv) @ w_ukv).reshape(b, n, MLA_HEADS, MLA_NOPE + MLA_V)
    k_nope, v = kv[..., :MLA_NOPE], kv[..., MLA_NOPE:]
    k_rope = jnp.broadcast_to(kr[:, :, None, :], (b, n, MLA_HEADS, MLA_ROPE))
    k = rms_norm(jnp.concatenate([k_nope, k_rope], axis=-1), g_mk)
    if pos is not None:
        k = jnp.concatenate([k[..., :MLA_NOPE], axial_rope(k[..., MLA_NOPE:], *pos)], axis=-1)
    return k, v


def multiscale_pool(u, w_pool, scale):
    b, n, _ = u.shape
    uf = u.astype(jnp.float32)
    cs = jnp.concatenate([jnp.zeros((b, 1, GROUP_W), jnp.float32), jnp.cumsum(uf, axis=1)], axis=1)
    t = jnp.arange(n)
    diffs = []
    for gi, w in enumerate(POOL_WINDOWS):
        left = w // 2
        right = w - 1 - left
        lo = jnp.clip(t - left, 0, n)
        hi = jnp.clip(t + right + 1, 0, n)
        sl = slice(gi * POOL_GC, (gi + 1) * POOL_GC)
        csg = cs[:, :, sl]
        cnt = (hi - lo).astype(jnp.float32)[None, :, None]
        mean = (jnp.take(csg, hi, axis=1) - jnp.take(csg, lo, axis=1)) / cnt
        diffs.append(mean - uf[:, :, sl])
    d = jnp.stack(diffs, axis=2).astype(u.dtype)
    y = jnp.einsum('bngc,gce->bnge', d, w_pool).reshape(b, n, GROUP_W)
    return y * scale


def short_conv3(u, w, bias):
    up = jnp.pad(u, ((0, 0), (1, 1), (0, 0)))
    return up[:, :-2] * w[0] + up[:, 1:-1] * w[1] + up[:, 2:] * w[2] + bias


def hyena_filters(n, w1, b1, w2, b2, w3, freq):
    f32 = jnp.float32
    t = jnp.linspace(0.0, 1.0, n, dtype=f32)[:, None]
    lag = jnp.arange(n, dtype=f32)[:, None]
    bands = jnp.linspace(1e-4, HY_BANDS - 1, HY_BANDS, dtype=f32)[None, :]
    ang = (2.0 * math.pi / n) * lag * bands
    feats = jnp.concatenate([t, jnp.cos(ang), -jnp.sin(ang)], axis=-1)
    fr = freq.astype(f32)
    hid = jnp.sin(fr * (feats @ w1.astype(f32) + b1.astype(f32)))
    hid = jnp.sin(fr * (hid @ w2.astype(f32) + b2.astype(f32)))
    h = (hid @ w3.astype(f32)).reshape(n, HY_DIRS, HY_ORDER, HY_C)
    deltas = jnp.abs(jnp.linspace(math.log(HY_DECAY_TARGET) / HY_SLOW_PCT,
                                  math.log(HY_DECAY_TARGET) / HY_FAST_PCT, HY_C, dtype=f32))
    h = h * jnp.exp(-t * deltas[None, :])[:, None, None, :]
    fwd, bwd = h[:, 0], h[:, 1]
    filt = jnp.concatenate([fwd, jnp.zeros((1, HY_ORDER, HY_C), f32), bwd[:0:-1]], axis=0)
    return filt * lax.rsqrt(jnp.sum(filt * filt, axis=0, keepdims=True) + EPS)


def long_conv(z, filt, bias):
    n = z.shape[1]
    zf = jnp.fft.rfft(z.astype(jnp.float32), n=2 * n, axis=1)
    hf = jnp.fft.rfft(filt, n=2 * n, axis=0)
    y = jnp.fft.irfft(zf * hf[None], n=2 * n, axis=1)[:, :n]
    return (y + z.astype(jnp.float32) * bias.astype(jnp.float32)).astype(z.dtype)


def hyena(u, conv_w, conv_b, filt, bias):
    v, x1, x2 = jnp.split(short_conv3(u, conv_w, conv_b), 3, axis=-1)
    z = x1 * long_conv(v, filt[:, 0], bias[0])
    return x2 * long_conv(z, filt[:, 1], bias[1])


def expert_choice_ffn(h, w_router, w_gate, w_up, w_down):
    b, n, d = h.shape
    cap = EC_CAPACITY * n // N_EXPERTS
    aff = jax.nn.softmax((h @ w_router).astype(jnp.float32), axis=-1)
    gate, idx = lax.top_k(jnp.swapaxes(aff, 1, 2), cap)
    xg = jax.vmap(lambda hb, ib: hb[ib])(h, idx)
    a = jnp.einsum('becd,edf->becf', xg, w_gate)
    u = jnp.einsum('becd,edf->becf', xg, w_up)
    y = jnp.einsum('becf,efd->becd', jax.nn.silu(a) * u, w_down) * gate[..., None].astype(h.dtype)
    scatter = lambda ib, yb: jnp.zeros((n, d), h.dtype).at[ib.reshape(-1)].add(yb.reshape(-1, d))
    return jax.vmap(scatter)(idx, y)


def setup_inputs(seed: int = 0) -> dict:
    key = jax.random.key(seed)
    keys = iter(jax.random.split(key, 40))
    f32 = jnp.float32

    def nrm(shape, scale):
        return scale * jax.random.normal(next(keys), shape, f32)

    def gain(shape):
        return 1.0 + nrm(shape, 0.05)

    L = DEPTH
    return {
        'x': nrm((BATCH, SEQ, D_MODEL), 1.0),
        'c': nrm((BATCH, D_MODEL), 1.0),
        'ctx': nrm((BATCH, CTX_LEN, D_MODEL), 1.0),
        'c_ctx': nrm((D_MODEL,), 1.0),
        'norm1_g': gain((L, D_MODEL)),
        'norm2_g': gain((L, D_MODEL)),
        'w_mod': nrm((L, D_MODEL, 6 * D_MODEL), 0.5 * D_MODEL ** -0.5),
        'b_mod': nrm((L, 6 * D_MODEL), 0.02),
        'w_in': nrm((L, D_MODEL, IN_COLS), D_MODEL ** -0.5),
        'w_out': nrm((L, D_MIX, D_MODEL), D_MIX ** -0.5),
        'pool_w': nrm((L, len(POOL_WINDOWS), POOL_GC, POOL_GC), POOL_GC ** -0.5),
        'pool_scale': gain((L, GROUP_W)),
        'gqa_qnorm_g': gain((L, GQA_HD)),
        'gqa_knorm_g': gain((L, GQA_HD)),
        'hy_conv_w': nrm((L, 3, (HY_ORDER + 1) * HY_C), 3 ** -0.5),
        'hy_conv_b': nrm((L, (HY_ORDER + 1) * HY_C), 0.02),
        'hy_f_w1': nrm((L, HY_EMB, HY_FW), 1.0),
        'hy_f_b1': nrm((L, HY_FW), 0.02),
        'hy_f_w2': nrm((L, HY_FW, HY_FW), HY_FW ** -0.5),
        'hy_f_b2': nrm((L, HY_FW), 0.02),
        'hy_f_w3': nrm((L, HY_FW, HY_DIRS * HY_ORDER * HY_C), HY_FW ** -0.5),
        'hy_freq': gain((L, HY_FW)),
        'hy_bias': nrm((L, HY_ORDER, HY_C), 0.5),
        'mla_cq_g': gain((L, MLA_Q_RANK)),
        'mla_w_uq': nrm((L, MLA_Q_RANK, MLA_HEADS * (MLA_NOPE + MLA_ROPE)), MLA_Q_RANK ** -0.5),
        'mla_ckv_g': gain((L, MLA_KV_RANK)),
        'mla_w_ukv': nrm((L, MLA_KV_RANK, MLA_HEADS * (MLA_NOPE + MLA_V)), MLA_KV_RANK ** -0.5),
        'mla_qnorm_g': gain((L, MLA_NOPE + MLA_ROPE)),
        'mla_knorm_g': gain((L, MLA_NOPE + MLA_ROPE)),
        'router_w': nrm((L, D_MODEL, N_EXPERTS), D_MODEL ** -0.5),
        'exp_w_gate': nrm((L, N_EXPERTS, D_MODEL, EXPERT_FF), D_MODEL ** -0.5),
        'exp_w_up': nrm((L, N_EXPERTS, D_MODEL, EXPERT_FF), D_MODEL ** -0.5),
        'exp_w_down': nrm((L, N_EXPERTS, EXPERT_FF, D_MODEL), EXPERT_FF ** -0.5),
    }


def reference(x, c, ctx, c_ctx, norm1_g, norm2_g, w_mod, b_mod, w_in, w_out, pool_w, pool_scale,
              gqa_qnorm_g, gqa_knorm_g, hy_conv_w, hy_conv_b, hy_f_w1, hy_f_b1, hy_f_w2, hy_f_b2,
              hy_f_w3, hy_freq, hy_bias, mla_cq_g, mla_w_uq, mla_ckv_g, mla_w_ukv, mla_qnorm_g,
              mla_knorm_g, router_w, exp_w_gate, exp_w_up, exp_w_down):
    b, n, _ = x.shape
    n_ctx = ctx.shape[1]
    rows = n // GRID_W
    row = jnp.repeat(jnp.arange(rows, dtype=jnp.float32), GRID_W)
    col = jnp.tile(jnp.arange(GRID_W, dtype=jnp.float32), rows)
    pos = (row, col)
    gqa_scale = GQA_HD ** -0.5
    mla_scale = (MLA_NOPE + MLA_ROPE) ** -0.5
    xc = ctx
    for l in range(DEPTH):
        last = l == DEPTH - 1
        sh1, sc1, g1, sh2, sc2, g2 = jnp.split((jax.nn.silu(c) @ w_mod[l] + b_mod[l])[:, None, :], 6, axis=-1)
        modc = jnp.split(jax.nn.silu(c_ctx) @ w_mod[l] + b_mod[l], 6)
        h = modulate(rms_norm(x, norm1_g[l]), sh1, sc1)
        hc = modulate(rms_norm(xc, norm1_g[l]), modc[0], modc[1])
        p_k, p_v, p_ckv, p_kr, p_q, p_cq, p_pool, p_hy = jnp.split(h @ w_in[l], COL_SPLITS, axis=-1)
        if last:
            pc = jnp.split(hc @ w_in[l][:, :CTX_KV_COLS], COL_SPLITS[:3], axis=-1)
        else:
            pc = jnp.split(hc @ w_in[l], COL_SPLITS, axis=-1)
        kc, vc = gqa_kv(pc[0], pc[1], gqa_knorm_g[l], None)
        mkc, mvc = mla_kv(pc[2], pc[3], mla_ckv_g[l], mla_w_ukv[l], mla_knorm_g[l], None)
        kl, vl = gqa_kv(p_k, p_v, gqa_knorm_g[l], pos)
        o_gqa = block_sweep_attention(gqa_q(p_q, gqa_qnorm_g[l], pos),
                                      jnp.concatenate([kl, kc], axis=1), jnp.concatenate([vl, vc], axis=1), gqa_scale)
        mkl, mvl = mla_kv(p_ckv, p_kr, mla_ckv_g[l], mla_w_ukv[l], mla_knorm_g[l], pos)
        o_mla = block_sweep_attention(mla_q(p_cq, mla_cq_g[l], mla_w_uq[l], mla_qnorm_g[l], pos),
                                      jnp.concatenate([mkl, mkc], axis=1), jnp.concatenate([mvl, mvc], axis=1), mla_scale)
        o_pool = multiscale_pool(p_pool, pool_w[l], pool_scale[l])
        filt = hyena_filters(n, hy_f_w1[l], hy_f_b1[l], hy_f_w2[l], hy_f_b2[l], hy_f_w3[l], hy_freq[l])
        o_hy = hyena(p_hy, hy_conv_w[l], hy_conv_b[l], filt, hy_bias[l])
        mixed = jnp.concatenate([o_pool, o_gqa, o_hy, o_mla], axis=-1) @ w_out[l]
        x_new = x + g1 * mixed
        x_new = x_new + g2 * expert_choice_ffn(modulate(rms_norm(x_new, norm2_g[l]), sh2, sc2),
                                               router_w[l], exp_w_gate[l], exp_w_up[l], exp_w_down[l])
        if not last:
            oc_gqa = attend(gqa_q(pc[4], gqa_qnorm_g[l], None), kc, vc, gqa_scale).reshape(b, n_ctx, -1)
            oc_mla = attend(mla_q(pc[5], mla_cq_g[l], mla_w_uq[l], mla_qnorm_g[l], None),
                            mkc, mvc, mla_scale).reshape(b, n_ctx, -1)
            oc_pool = multiscale_pool(pc[6], pool_w[l], pool_scale[l])
            filt_c = hyena_filters(n_ctx, hy_f_w1[l], hy_f_b1[l], hy_f_w2[l], hy_f_b2[l], hy_f_w3[l], hy_freq[l])
            oc_hy = hyena(pc[7], hy_conv_w[l], hy_conv_b[l], filt_c, hy_bias[l])
            mixed_c = jnp.concatenate([oc_pool, oc_gqa, oc_hy, oc_mla], axis=-1) @ w_out[l]
            xc = xc + modc[2] * mixed_c
            xc = xc + modc[5] * expert_choice_ffn(modulate(rms_norm(xc, norm2_g[l]), modc[3], modc[4]),
                                                  router_w[l], exp_w_gate[l], exp_w_up[l], exp_w_down[l])
        x = x_new
    return x
```

```python
import functools
import math

import numpy as np
import jax
import jax.numpy as jnp
from jax import lax
from jax.experimental import pallas as pl
from jax.experimental.pallas import tpu as pltpu

F32 = jnp.float32
BF16 = jnp.bfloat16
I32 = jnp.int32

GRID_W = 64
ROPE_THETA = 10000.0
EPS = 1e-6
POOL_WINDOWS = (2, 4, 8, 16)
GQA_HEADS = 4
GQA_KV_HEADS = 2
GQA_HD = 64
HY_BANDS = 8
HY_DECAY_TARGET = 1e-2
HY_FAST_PCT = 0.3
HY_SLOW_PCT = 1.5
MLA_HEADS = 4
MLA_NOPE = 64
MLA_ROPE = 32
MLA_V = 64
MLA_HD = MLA_NOPE + MLA_ROPE
N_EXPERTS = 16
EC_CAPACITY = 2

LANES = 128
SUBLANES = 8
MLA_PAD = LANES
VMEM_LIMIT = 56 * 2 ** 20
HIGHEST = lax.Precision.HIGHEST


def _cp(sem, vmem=VMEM_LIMIT):
    return pltpu.CompilerParams(dimension_semantics=sem, vmem_limit_bytes=vmem)


def _dot(a, b):
    return jnp.dot(a, b, preferred_element_type=F32)


def _dot_split(x, m):
    hi = x.astype(BF16)
    lo = (x - hi.astype(F32)).astype(BF16)
    return _dot(hi, m) + _dot(lo, m)


def _rms_rows(x):
    return x * lax.rsqrt(jnp.mean(x * x, axis=-1, keepdims=True) + EPS)


def _rope(y, cos, sin_a, sin_b, half):
    w = y.shape[-1]
    nxt = pltpu.roll(y, w - half, axis=1)
    prv = pltpu.roll(y, half, axis=1)
    return y * cos + nxt * sin_a + prv * sin_b


def _mod_kernel(c_ref, w_ref, b_ref, o_ref):
    c = c_ref[...]
    s = c * (1.0 / (1.0 + jnp.exp(-c)))
    o_ref[0] = jnp.dot(s, w_ref[0], precision=HIGHEST, preferred_element_type=F32) + b_ref[0]


def _mod_call(cc, w_mod, b_mod):
    L, D, N = w_mod.shape
    R = cc.shape[0]
    tn = 1536
    return pl.pallas_call(
        _mod_kernel,
        grid=(L, N // tn),
        in_specs=[
            pl.BlockSpec((R, D), lambda l, j: (0, 0)),
            pl.BlockSpec((1, D, tn), lambda l, j: (l, 0, j)),
            pl.BlockSpec((1, 1, tn), lambda l, j: (l, 0, j)),
        ],
        out_specs=pl.BlockSpec((1, R, tn), lambda l, j: (l, 0, j)),
        out_shape=jax.ShapeDtypeStruct((L, R, N), F32),
        compiler_params=_cp(("arbitrary", "arbitrary")),
        name="adaln_mod",
    )(cc, w_mod, b_mod.reshape(L, 1, N))


KV_COLS = 512
ALL_COLS = 2048


def _inproj_kernel(kv_only, x_ref, sh_ref, sc_ref, g1_ref, w_ref,
                   gk_ref, gq_ref, gckv_ref, wukv_ref, gmk_ref, gcq_ref, wuq_ref, gmq_ref,
                   seg_ref, cg_ref, sag_ref, sbg_ref, cm_ref, sam_ref, sbm_ref, *outs):
    if kv_only:
        k_ref, v_ref, mk_ref, mv_ref = outs
    else:
        k_ref, v_ref, mk_ref, mv_ref, q_ref, mq_ref, pool_ref, hy_ref = outs
    x = x_ref[0]
    h = _rms_rows(x) * g1_ref[...] * (1.0 + sc_ref[0]) + sh_ref[0]
    p = _dot(h.astype(BF16), w_ref[...])

    def seg_norm(y, seg):
        ss = _dot_split(y * y, seg)
        return y * lax.rsqrt(ss * (1.0 / GQA_HD) + EPS)

    def head_norm(y):
        parts = []
        for hh in range(MLA_HEADS):
            s = y[:, hh * MLA_PAD:(hh + 1) * MLA_PAD]
            ms = jnp.sum(s * s, axis=-1, keepdims=True) * (1.0 / MLA_HD)
            parts.append(s * lax.rsqrt(ms + EPS))
        return jnp.concatenate(parts, axis=1)

    cm, sam, sbm = cm_ref[...], sam_ref[...], sbm_ref[...]
    kk = seg_norm(p[:, 0:128], seg_ref[0:128, 0:128]) * gk_ref[...]
    kk = _rope(kk, cg_ref[:, 0:128], sag_ref[:, 0:128], sbg_ref[:, 0:128], GQA_HD // 4)
    k_ref[0] = kk.astype(BF16)
    v_ref[0] = p[:, 128:256].astype(BF16)
    ckv = _rms_rows(p[:, 256:384]) * gckv_ref[...]
    kv = _dot(ckv.astype(BF16), wukv_ref[...])
    kr = p[:, 384:512]
    kraw = kv[:, 0:MLA_HEADS * MLA_PAD] + jnp.concatenate([kr] * MLA_HEADS, axis=1)
    mk = head_norm(kraw) * gmk_ref[...]
    mk = _rope(mk, cm, sam, sbm, MLA_ROPE // 4)
    mk_ref[0] = mk.astype(BF16)
    mv_ref[0] = kv[:, MLA_HEADS * MLA_PAD:].astype(BF16)
    if kv_only:
        return
    qq = seg_norm(p[:, 512:768], seg_ref[...]) * gq_ref[...]
    qq = _rope(qq, cg_ref[...], sag_ref[...], sbg_ref[...], GQA_HD // 4)
    q_ref[0] = (qq * (GQA_HD ** -0.5)).astype(BF16)
    cq = _rms_rows(p[:, 768:1024]) * gcq_ref[...]
    mq = head_norm(_dot(cq.astype(BF16), wuq_ref[...])) * gmq_ref[...]
    mq = _rope(mq, cm, sam, sbm, MLA_ROPE // 4)
    mq_ref[0] = (mq * (MLA_HD ** -0.5)).astype(BF16)
    pool_ref[0] = p[:, 1024:1280]
    hy_ref[0] = p[:, 1280:2048]


def _inproj_call(xs, sh, sc, lp, tabs, kv_only):
    B, n, D = xs.shape
    tm = min(n, 512)
    cols = KV_COLS if kv_only else ALL_COLS
    w = lp["w_in_r"][:, :cols]
    full = lambda a: pl.BlockSpec(a.shape, lambda i, b: (0,) * a.ndim)
    tab = lambda a: pl.BlockSpec((tm, a.shape[1]), lambda i, b: (i, 0))
    row = lambda width: pl.BlockSpec((1, tm, width), lambda i, b: (b, i, 0))
    params = [lp["g_k"], lp["g_q"], lp["g_ckv"], lp["w_ukv_r"], lp["g_mk"], lp["g_cq"], lp["w_uq_r"], lp["g_mq"],
              lp["seg"]]
    out_w = [(128, BF16), (128, BF16), (512, BF16), (256, BF16)]
    if not kv_only:
        out_w += [(256, BF16), (512, BF16), (256, F32), (768, F32)]
    return pl.pallas_call(
        functools.partial(_inproj_kernel, kv_only),
        grid=(n // tm, B),
        in_specs=[row(D),
                  pl.BlockSpec((1, 1, D), lambda i, b: (b, 0, 0)),
                  pl.BlockSpec((1, 1, D), lambda i, b: (b, 0, 0)),
                  full(lp["norm1_g"]), full(w)] + [full(a) for a in params] + [tab(a) for a in tabs],
        out_specs=[row(wd) for wd, _ in out_w],
        out_shape=[jax.ShapeDtypeStruct((B, n, wd), dt) for wd, dt in out_w],
        compiler_params=_cp(("arbitrary", "arbitrary")),
        name="inproj_kv" if kv_only else "inproj",
    )(xs, sh, sc, lp["norm1_g"], w, *params, *tabs)


def _attn_kernel(heads, kv_heads, dk, dv, q_ref, k_ref, v_ref, o_ref):
    grp = heads // kv_heads
    outs = []
    for g in range(kv_heads):
        kg = k_ref[0, :, g * dk:(g + 1) * dk]
        vg = v_ref[0, :, g * dv:(g + 1) * dv]
        for j in range(grp):
            hh = g * grp + j
            qh = q_ref[0, :, hh * dk:(hh + 1) * dk]
            s = lax.dot_general(qh, kg, (((1,), (1,)), ((), ())), preferred_element_type=F32)
            m = jnp.max(s, axis=-1, keepdims=True)
            e = jnp.exp(s - m)
            l = jnp.sum(e, axis=-1, keepdims=True)
            outs.append(_dot(e.astype(BF16), vg) / l)
    o_ref[0] = jnp.concatenate(outs, axis=1).astype(o_ref.dtype)


def _attn_call(q, k, v, heads, kv_heads, dk, dv, name):
    B, n, _ = q.shape
    nk = k.shape[1]
    tq = min(n, 256)
    return pl.pallas_call(
        functools.partial(_attn_kernel, heads, kv_heads, dk, dv),
        grid=(B, n // tq),
        in_specs=[pl.BlockSpec((1, tq, heads * dk), lambda b, i: (b, i, 0)),
                  pl.BlockSpec((1, nk, kv_heads * dk), lambda b, i: (b, 0, 0)),
                  pl.BlockSpec((1, nk, kv_heads * dv), lambda b, i: (b, 0, 0))],
        out_specs=pl.BlockSpec((1, tq, heads * dv), lambda b, i: (b, i, 0)),
        out_shape=jax.ShapeDtypeStruct((B, n, heads * dv), BF16),
        compiler_params=_cp(("arbitrary", "arbitrary")),
        name=name,
    )(q, k, v)


POOL_HALO = 8


def _pool_kernel(n, tile, u_ref, wbd_ref, scale_ref, o_ref):
    kw = min(n, tile + 2 * POOL_HALO)
    gc = u_ref.shape[2] // len(POOL_WINDOWS)
    for i in range(n // tile):
        t0 = i * tile
        ws = min(max(t0 - POOL_HALO, 0), n - kw)
        uw = u_ref[0, ws:ws + kw, :]
        hi = uw.astype(BF16)
        lo = (uw - hi.astype(F32)).astype(BF16)
        t = t0 + lax.broadcasted_iota(I32, (tile, kw), 0)
        s = ws + lax.broadcasted_iota(I32, (tile, kw), 1)
        trow = t0 + lax.broadcasted_iota(I32, (tile, 1), 0)
        lane = lax.broadcasted_iota(I32, (tile, u_ref.shape[2]), 1)
        uc = u_ref[0, t0:t0 + tile, :]
        mean = jnp.zeros_like(uc)
        for gi, w in enumerate(POOL_WINDOWS):
            left = w // 2
            right = w - 1 - left
            band = jnp.where((s >= t - left) & (s <= t + right), 1.0, 0.0).astype(BF16)
            sm = _dot(band, hi) + _dot(band, lo)
            cnt = (jnp.minimum(trow + right, n - 1) - jnp.maximum(trow - left, 0) + 1).astype(F32)
            mean = jnp.where((lane >= gi * gc) & (lane < (gi + 1) * gc), sm / cnt, mean)
        d = mean - uc
        o_ref[0, t0:t0 + tile, :] = (_dot(d.astype(BF16), wbd_ref[...]) * scale_ref[...]).astype(o_ref.dtype)


def _pool_call(u, lp):
    B, n, W = u.shape
    return pl.pallas_call(
        functools.partial(_pool_kernel, n, min(n, 256)),
        grid=(B,),
        in_specs=[pl.BlockSpec((1, n, W), lambda b: (b, 0, 0)),
                  pl.BlockSpec((W, W), lambda b: (0, 0)),
                  pl.BlockSpec((1, W), lambda b: (0, 0))],
        out_specs=pl.BlockSpec((1, n, W), lambda b: (b, 0, 0)),
        out_shape=jax.ShapeDtypeStruct((B, n, W), BF16),
        compiler_params=_cp(("arbitrary",)),
        name="pool",
    )(u, lp["pool_bd"], lp["pool_scale"])


def _hyfilt_kernel(n, feats_ref, w1_ref, b1_ref, w2_ref, b2_ref, w3_ref, fr_ref, decay_ref, c_ref, s_ref,
                   hre_ref, him_ref, hn_ref, fsum_ref, fdif_ref):
    i = pl.program_id(0)
    tk = c_ref.shape[0]
    hc2 = fsum_ref.shape[1]

    @pl.when(i == 0)
    def _():
        fr = fr_ref[...]
        hid = jnp.sin(fr * (jnp.dot(feats_ref[...], w1_ref[...], precision=HIGHEST, preferred_element_type=F32)
                            + b1_ref[...]))
        hid = jnp.sin(fr * (jnp.dot(hid, w2_ref[...], precision=HIGHEST, preferred_element_type=F32) + b2_ref[...]))
        h = jnp.dot(hid, w3_ref[...], precision=HIGHEST, preferred_element_type=F32)
        dec = decay_ref[...]
        dec2 = jnp.concatenate([dec, dec], axis=1)
        row = lax.broadcasted_iota(I32, (n, 1), 0)
        fwd = h[:, 0:hc2] * dec2
        bwd = jnp.where(row == 0, 0.0, h[:, hc2:2 * hc2] * dec2)
        nrm = jnp.sum(fwd * fwd + bwd * bwd, axis=0, keepdims=True)
        scl = lax.rsqrt(nrm + EPS)
        fwd = fwd * scl
        bwd = bwd * scl
        fsum_ref[...] = fwd + bwd
        fdif_ref[...] = bwd - fwd
        alt = jnp.where(row % 2 == 0, 1.0, -1.0)
        hn_ref[...] = jnp.sum((fwd + bwd) * alt, axis=0, keepdims=True) * (1.0 / (2 * n))

    k = i * tk + lax.broadcasted_iota(I32, (tk, 1), 0)
    wk = jnp.where(k == 0, 1.0 / (2 * n), 2.0 / (2 * n))
    hre_ref[...] = wk * jnp.dot(c_ref[...], fsum_ref[...], precision=HIGHEST, preferred_element_type=F32)
    him_ref[...] = wk * jnp.dot(s_ref[...], fdif_ref[...], precision=HIGHEST, preferred_element_type=F32)


def _hyfilt_call(lp, hc):
    n = hc["n"]
    tk = min(n, 256)
    hc2 = lp["hy_w3"].shape[1] // 2
    full = lambda a: pl.BlockSpec(a.shape, lambda i: (0,) * a.ndim)
    args = [hc["feats"], lp["hy_w1"], lp["hy_b1"], lp["hy_w2"], lp["hy_b2"], lp["hy_w3"], lp["hy_freq"], hc["decay"]]
    return pl.pallas_call(
        functools.partial(_hyfilt_kernel, n),
        grid=(n // tk,),
        in_specs=[full(a) for a in args] + [pl.BlockSpec((tk, n), lambda i: (i, 0))] * 2,
        out_specs=[pl.BlockSpec((tk, hc2), lambda i: (i, 0)), pl.BlockSpec((tk, hc2), lambda i: (i, 0)),
                   pl.BlockSpec((1, hc2), lambda i: (0, 0))],
        out_shape=[jax.ShapeDtypeStruct((n, hc2), F32), jax.ShapeDtypeStruct((n, hc2), F32),
                   jax.ShapeDtypeStruct((1, hc2), F32)],
        scratch_shapes=[pltpu.VMEM((n, hc2), F32), pltpu.VMEM((n, hc2), F32)],
        compiler_params=_cp(("arbitrary",)),
        name="hyena_filter",
    )(*args, hc["c32"], hc["s32"])


def _hyena_kernel(n, rc, u_ref, cw_ref, cb_ref, c_ref, s_ref, hre_ref, him_ref, hn_ref, bias_ref, o_ref,
                  uc_ref, z_ref, p_ref, q_ref):
    ch = o_ref.shape[2]
    chunks = [(r0, lax.broadcasted_iota(I32, (rc, 1), 0) + r0) for r0 in range(0, n, rc)]
    for r0, row in chunks:
        u = u_ref[0, r0:r0 + rc, :]
        if r0 == 0:
            up = jnp.where(row == 0, 0.0, pltpu.roll(u, 1, axis=0))
        else:
            up = u_ref[0, r0 - 1:r0 - 1 + rc, :]
        if r0 + rc == n:
            un = jnp.where(row == n - 1, 0.0, pltpu.roll(u, rc - 1, axis=0))
        else:
            un = u_ref[0, r0 + 1:r0 + 1 + rc, :]
        uc_ref[r0:r0 + rc, :] = up * cw_ref[0:1, :] + u * cw_ref[1:2, :] + un * cw_ref[2:3, :] + cb_ref[...]

    def long_conv(o, emit):
        z = z_ref[...]
        zb = z.astype(BF16)
        alt_all = jnp.where(lax.broadcasted_iota(I32, (n, 1), 0) % 2 == 0, 1.0, -1.0)
        nyq = jnp.sum(z * alt_all, axis=0, keepdims=True) * hn_ref[:, o * ch:(o + 1) * ch]
        for r0, _ in chunks:
            a = _dot(c_ref[r0:r0 + rc, :], zb)
            b = _dot(s_ref[r0:r0 + rc, :], zb)
            hre = hre_ref[r0:r0 + rc, o * ch:(o + 1) * ch]
            him = him_ref[r0:r0 + rc, o * ch:(o + 1) * ch]
            p_ref[r0:r0 + rc, :] = (a * hre + b * him).astype(BF16)
            q_ref[r0:r0 + rc, :] = (b * hre - a * him).astype(BF16)
        pq = p_ref[...]
        qq = q_ref[...]
        for r0, row in chunks:
            y = _dot(c_ref[r0:r0 + rc, :], pq) + _dot(s_ref[r0:r0 + rc, :], qq)
            alt = jnp.where(row % 2 == 0, 1.0, -1.0)
            emit(r0, y + alt * nyq + z_ref[r0:r0 + rc, :] * bias_ref[o:o + 1, :])

    z_ref[...] = uc_ref[:, 0:ch]

    def emit1(r0, y):
        z_ref[r0:r0 + rc, :] = uc_ref[r0:r0 + rc, ch:2 * ch] * y

    def emit2(r0, y):
        o_ref[0, r0:r0 + rc, :] = (uc_ref[r0:r0 + rc, 2 * ch:3 * ch] * y).astype(o_ref.dtype)

    long_conv(0, emit1)
    long_conv(1, emit2)


def _hyena_call(u, lp, hc, filt):
    B, n, W = u.shape
    ch = W // 3
    hre, him, hn = filt
    once = lambda a: pl.BlockSpec(a.shape, lambda b: (0,) * a.ndim, pipeline_mode=pl.Buffered(1))
    return pl.pallas_call(
        functools.partial(_hyena_kernel, n, min(n, 512)),
        scratch_shapes=[pltpu.VMEM((n, W), F32), pltpu.VMEM((n, ch), F32),
                        pltpu.VMEM((n, ch), BF16), pltpu.VMEM((n, ch), BF16)],
        grid=(B,),
        in_specs=[pl.BlockSpec((1, n, W), lambda b: (b, 0, 0)),
                  once(lp["hy_conv_w"]), once(lp["hy_conv_b"]), once(hc["c16"]), once(hc["s16"]),
                  once(hre), once(him), once(hn), once(lp["hy_bias"])],
        out_specs=pl.BlockSpec((1, n, ch), lambda b: (b, 0, 0)),
        out_shape=jax.ShapeDtypeStruct((B, n, ch), BF16),
        compiler_params=_cp(("arbitrary",)),
        name="hyena",
    )(u, lp["hy_conv_w"], lp["hy_conv_b"], hc["c16"], hc["s16"], hre, him, hn, lp["hy_bias"])


def _outproj_kernel(x_ref, g_ref, a_ref, b_ref, c_ref, d_ref, w_ref, o_ref):
    mixed = (_dot(a_ref[0], w_ref[0]) + _dot(b_ref[0], w_ref[1])
             + _dot(c_ref[0], w_ref[2]) + _dot(d_ref[0], w_ref[3]))
    o_ref[0] = x_ref[0] + g_ref[0] * mixed


def _outproj_call(xs, gate, parts, lp):
    B, n, D = xs.shape
    tm = min(n, 512)
    gw = parts[0].shape[2]
    return pl.pallas_call(
        _outproj_kernel,
        grid=(B, n // tm),
        in_specs=[pl.BlockSpec((1, tm, D), lambda b, i: (b, i, 0)),
                  pl.BlockSpec((1, 1, D), lambda b, i: (b, 0, 0))]
        + [pl.BlockSpec((1, tm, gw), lambda b, i: (b, i, 0))] * 4
        + [pl.BlockSpec(lp["w_out"].shape, lambda b, i: (0, 0, 0))],
        out_specs=pl.BlockSpec((1, tm, D), lambda b, i: (b, i, 0)),
        out_shape=jax.ShapeDtypeStruct((B, n, D), F32),
        compiler_params=_cp(("arbitrary", "arbitrary")),
        name="outproj",
    )(xs, gate, *parts, lp["w_out"])


PREFIX_BLOCK = 256


def _prefix_lanes(m, tri):
    e, n = m.shape
    nb = n // PREFIX_BLOCK
    stk = jnp.concatenate([m[:, j * PREFIX_BLOCK:(j + 1) * PREFIX_BLOCK] for j in range(nb)], axis=0)
    pre = _dot(stk.astype(BF16), tri)
    outs = []
    carry = jnp.zeros((e, 1), F32)
    for j in range(nb):
        pj = pre[j * e:(j + 1) * e]
        outs.append(pj + carry)
        carry = carry + pj[:, PREFIX_BLOCK - 1:PREFIX_BLOCK]
    return jnp.concatenate(outs, axis=1)


def _route_kernel(n, cap, x_ref, sh_ref, sc_ref, g_ref, wr_ref, tri_ref, hcm_ref, idx_ref, gate_ref):
    x = x_ref[0]
    h = _rms_rows(x) * g_ref[...] * (1.0 + sc_ref[0]) + sh_ref[0]
    for c in range(h.shape[1] // LANES):
        hcm_ref[0, pl.ds(c, n, stride=SUBLANES), :] = h[:, c * LANES:(c + 1) * LANES]
    logits = jnp.dot(h, wr_ref[...], precision=HIGHEST, preferred_element_type=F32)
    lt = logits.T[0:N_EXPERTS, :]
    ex = jnp.exp(lt - jnp.max(lt, axis=0, keepdims=True))
    aff = ex / jnp.sum(ex, axis=0, keepdims=True)
    bits = pltpu.bitcast(aff, I32)
    thr = jnp.zeros((N_EXPERTS, 1), I32)
    for bit in range(30, -1, -1):
        cand = thr | (1 << bit)
        cnt = jnp.sum(jnp.where(bits >= cand, 1.0, 0.0), axis=1, keepdims=True)
        thr = jnp.where(cnt >= cap, cand, thr)
    gt = jnp.where(bits > thr, 1.0, 0.0)
    eq = jnp.where(bits == thr, 1.0, 0.0)
    need = cap - jnp.sum(gt, axis=1, keepdims=True)
    tri = tri_ref[...]
    sel = gt + eq * jnp.where(_prefix_lanes(eq, tri) <= need, 1.0, 0.0)
    incl = _prefix_lanes(sel, tri)
    val = sel * aff
    slot = lax.broadcasted_iota(I32, (cap, n), 0).astype(F32)
    lane = lax.broadcasted_iota(I32, (cap, LANES), 1)
    res_i = jnp.zeros((cap, LANES), F32)
    res_g = jnp.zeros((cap, LANES), F32)
    for e in range(N_EXPERTS):
        inc_e = incl[e:e + 1, :]
        idx_col = jnp.sum(jnp.where(inc_e <= slot, 1.0, 0.0), axis=1, keepdims=True)
        gate_col = jnp.sum(jnp.where(inc_e == slot + 1.0, val[e:e + 1, :], 0.0), axis=1, keepdims=True)
        res_i = jnp.where(lane == e, idx_col, res_i)
        res_g = jnp.where(lane == e, gate_col, res_g)
    idx_ref[0] = res_i.astype(I32)
    gate_ref[0] = res_g


def _route_call(xs, sh, sc, lp, tri):
    B, n, D = xs.shape
    cap = EC_CAPACITY * n // N_EXPERTS
    rows = n * (D // LANES)
    return pl.pallas_call(
        functools.partial(_route_kernel, n, cap),
        grid=(B,),
        in_specs=[pl.BlockSpec((1, n, D), lambda b: (b, 0, 0)),
                  pl.BlockSpec((1, 1, D), lambda b: (b, 0, 0)),
                  pl.BlockSpec((1, 1, D), lambda b: (b, 0, 0)),
                  pl.BlockSpec((1, D), lambda b: (0, 0)),
                  pl.BlockSpec((D, LANES), lambda b: (0, 0)),
                  pl.BlockSpec(tri.shape, lambda b: (0, 0))],
        out_specs=[pl.BlockSpec((1, rows, LANES), lambda b: (b, 0, 0)),
                   pl.BlockSpec((1, cap, LANES), lambda b: (b, 0, 0)),
                   pl.BlockSpec((1, cap, LANES), lambda b: (b, 0, 0))],
        out_shape=[jax.ShapeDtypeStruct((B, rows, LANES), F32),
                   jax.ShapeDtypeStruct((B, cap, LANES), I32),
                   jax.ShapeDtypeStruct((B, cap, LANES), F32)],
        compiler_params=_cp(("arbitrary",)),
        name="route",
    )(xs, sh, sc, lp["norm2_g"], lp["router_pad"], tri)


GATHER_UNROLL = 8


def _ffn_kernel(cap, idx_ref, h_ref, gate_ref, wg_ref, wu_ref, wd_ref, o_ref, xg_ref, y_ref):
    b = pl.program_id(0)
    e = pl.program_id(1)
    base = (b * N_EXPERTS + e) * cap
    chunks = wg_ref.shape[1] // LANES

    @pl.when(e == 0)
    def _():
        o_ref[...] = jnp.zeros_like(o_ref)

    def gather(i, carry):
        for j in range(GATHER_UNROLL):
            r = i * GATHER_UNROLL + j
            src = pl.multiple_of(idx_ref[base + r] * SUBLANES, SUBLANES)
            dst = pl.multiple_of(r * SUBLANES, SUBLANES)
            xg_ref[pl.ds(dst, SUBLANES), :] = h_ref[0, pl.ds(src, SUBLANES), :]
        return carry

    lax.fori_loop(0, cap // GATHER_UNROLL, gather, 0)
    xt = jnp.concatenate([xg_ref[pl.ds(c, cap, stride=SUBLANES), :] for c in range(chunks)], axis=1).astype(BF16)
    a = _dot(xt, wg_ref[0])
    u = _dot(xt, wu_ref[0])
    hmid = (a * (1.0 / (1.0 + jnp.exp(-a))) * u).astype(BF16)
    lane = lax.broadcasted_iota(I32, (cap, LANES), 1)
    gcol = jnp.sum(jnp.where(lane == e, gate_ref[0], 0.0), axis=1, keepdims=True)
    y = _dot(hmid, wd_ref[0]) * gcol
    for c in range(chunks):
        y_ref[pl.ds(c, cap, stride=SUBLANES), :] = y[:, c * LANES:(c + 1) * LANES]

    def scatter(i, carry):
        for j in range(GATHER_UNROLL):
            r = i * GATHER_UNROLL + j
            dst = pl.multiple_of(idx_ref[base + r] * SUBLANES, SUBLANES)
            src = pl.multiple_of(r * SUBLANES, SUBLANES)
            o_ref[0, pl.ds(dst, SUBLANES), :] = o_ref[0, pl.ds(dst, SUBLANES), :] + y_ref[pl.ds(src, SUBLANES), :]
        return carry

    lax.fori_loop(0, cap // GATHER_UNROLL, scatter, 0)


def _ffn_call(idx_flat, hcm, gate, lp, cap):
    B, rows, _ = hcm.shape
    D = lp["w_gate"].shape[1]
    ff = lp["w_gate"].shape[2]
    return pl.pallas_call(
        functools.partial(_ffn_kernel, cap),
        grid=(B, N_EXPERTS),
        in_specs=[pl.BlockSpec(memory_space=pltpu.SMEM),
                  pl.BlockSpec((1, rows, LANES), lambda b, e: (b, 0, 0)),
                  pl.BlockSpec((1, cap, LANES), lambda b, e: (b, 0, 0)),
                  pl.BlockSpec((1, D, ff), lambda b, e: (e, 0, 0)),
                  pl.BlockSpec((1, D, ff), lambda b, e: (e, 0, 0)),
                  pl.BlockSpec((1, ff, D), lambda b, e: (e, 0, 0))],
        out_specs=pl.BlockSpec((1, rows, LANES), lambda b, e: (b, 0, 0)),
        out_shape=jax.ShapeDtypeStruct((B, rows, LANES), F32),
        scratch_shapes=[pltpu.VMEM((cap * SUBLANES, LANES), F32), pltpu.VMEM((cap * SUBLANES, LANES), F32)],
        compiler_params=_cp(("arbitrary", "arbitrary")),
        name="expert_ffn",
    )(idx_flat, hcm, gate, lp["w_gate"], lp["w_up"], lp["w_down"])


def _moe_residual_kernel(x_ref, g_ref, m_ref, o_ref):
    tm = x_ref.shape[1]
    chunks = x_ref.shape[2] // LANES
    moe = jnp.concatenate([m_ref[0, pl.ds(c, tm, stride=SUBLANES), :] for c in range(chunks)], axis=1)
    o_ref[0] = x_ref[0] + g_ref[0] * moe


def _moe_residual_call(xs, gate, moe_cm):
    B, n, D = xs.shape
    tm = min(n, 512)
    return pl.pallas_call(
        _moe_residual_kernel,
        grid=(B, n // tm),
        in_specs=[pl.BlockSpec((1, tm, D), lambda b, i: (b, i, 0)),
                  pl.BlockSpec((1, 1, D), lambda b, i: (b, 0, 0)),
                  pl.BlockSpec((1, tm * (D // LANES), LANES), lambda b, i: (b, i, 0))],
        out_specs=pl.BlockSpec((1, tm, D), lambda b, i: (b, i, 0)),
        out_shape=jax.ShapeDtypeStruct((B, n, D), F32),
        compiler_params=_cp(("arbitrary", "arbitrary")),
        name="moe_residual",
    )(xs, gate, moe_cm)


def _axial_tables(n, head_pat, half, reps, roped):
    t = jnp.arange(n)
    pos = (jnp.floor_divide(t, GRID_W).astype(F32), jnp.remainder(t, GRID_W).astype(F32))
    inv = ROPE_THETA ** (-jnp.arange(half, dtype=F32) / half)
    cols_c, cols_a, cols_b = [], [], []
    one = jnp.ones((n,), F32)
    zero = jnp.zeros((n,), F32)
    for pat in head_pat:
        if pat is None or not roped:
            cols_c.append(one), cols_a.append(zero), cols_b.append(zero)
            continue
        axis, f, first = pat
        ang = pos[axis] * inv[f]
        cols_c.append(jnp.cos(ang))
        cols_a.append(-jnp.sin(ang) if first else zero)
        cols_b.append(zero if first else jnp.sin(ang))
    mk = lambda cols: jnp.tile(jnp.stack(cols, axis=1), (1, reps))
    return mk(cols_c), mk(cols_a), mk(cols_b)


def _rope_tables(n, roped):
    gh = GQA_HD // 4
    gpat = [(j // (2 * gh), j % gh, (j % (2 * gh)) < gh) for j in range(GQA_HD)]
    mh = MLA_ROPE // 4
    mpat = [None] * MLA_NOPE + [(j // (2 * mh), j % mh, (j % (2 * mh)) < mh) for j in range(MLA_ROPE)]
    mpat += [None] * (MLA_PAD - MLA_HD)
    return _axial_tables(n, gpat, gh, GQA_HEADS, roped) + _axial_tables(n, mpat, mh, MLA_HEADS, roped)


def _hyena_consts(n, ch):
    f32 = F32
    k = jnp.arange(n, dtype=I32)
    ang = jnp.remainder(k[:, None] * k[None, :], 2 * n).astype(f32) * (math.pi / n)
    c32, s32 = jnp.cos(ang), jnp.sin(ang)
    t = jnp.linspace(0.0, 1.0, n, dtype=f32)[:, None]
    lag = jnp.arange(n, dtype=f32)[:, None]
    bands = jnp.linspace(1e-4, HY_BANDS - 1, HY_BANDS, dtype=f32)[None, :]
    a = (2.0 * math.pi / n) * lag * bands
    feats = jnp.concatenate([t, jnp.cos(a), -jnp.sin(a)], axis=-1)
    feats = jnp.pad(feats, ((0, 0), (0, LANES - feats.shape[1])))
    deltas = jnp.abs(jnp.linspace(math.log(HY_DECAY_TARGET) / HY_SLOW_PCT,
                                  math.log(HY_DECAY_TARGET) / HY_FAST_PCT, ch, dtype=f32))
    decay = jnp.exp(-t * deltas[None, :])
    return {"n": n, "c32": c32, "s32": s32, "c16": c32.astype(BF16), "s16": s32.astype(BF16),
            "feats": feats, "decay": decay}


def _seg_matrix(width, seg):
    i = np.arange(width)
    return jnp.asarray((i[:, None] // seg == i[None, :] // seg).astype(np.float32), dtype=BF16)


def _layer_params(l, w):
    D = w["w_in"].shape[1]
    wi = w["w_in"][l]
    z = lambda c: jnp.zeros((D, c), F32)
    w_in_r = jnp.concatenate(
        [wi[:, 0:384], z(MLA_NOPE), wi[:, 384:416], z(MLA_PAD - MLA_HD), wi[:, 416:]], axis=1).astype(BF16)
    ukv = w["mla_w_ukv"][l].reshape(-1, MLA_HEADS, MLA_NOPE + MLA_V)
    kpart = jnp.pad(ukv[:, :, :MLA_NOPE], ((0, 0), (0, 0), (0, MLA_PAD - MLA_NOPE)))
    w_ukv_r = jnp.concatenate([kpart.reshape(ukv.shape[0], -1), ukv[:, :, MLA_NOPE:].reshape(ukv.shape[0], -1)],
                              axis=1).astype(BF16)
    uq = w["mla_w_uq"][l].reshape(-1, MLA_HEADS, MLA_HD)
    w_uq_r = jnp.pad(uq, ((0, 0), (0, 0), (0, MLA_PAD - MLA_HD))).reshape(uq.shape[0], -1).astype(BF16)
    padg = lambda g: jnp.tile(jnp.pad(g, (0, MLA_PAD - MLA_HD)), MLA_HEADS)[None, :]
    pw = w["pool_w"][l]
    gc = pw.shape[1]
    pool_bd = jnp.zeros((len(POOL_WINDOWS) * gc,) * 2, F32)
    for gi in range(len(POOL_WINDOWS)):
        pool_bd = pool_bd.at[gi * gc:(gi + 1) * gc, gi * gc:(gi + 1) * gc].set(pw[gi])
    nf = w["hy_f_w1"].shape[1]
    return {
        "norm1_g": w["norm1_g"][l][None, :], "norm2_g": w["norm2_g"][l][None, :],
        "w_in_r": w_in_r, "w_ukv_r": w_ukv_r, "w_uq_r": w_uq_r,
        "g_k": jnp.tile(w["gqa_knorm_g"][l], GQA_KV_HEADS)[None, :],
        "g_q": jnp.tile(w["gqa_qnorm_g"][l], GQA_HEADS)[None, :],
        "g_ckv": w["mla_ckv_g"][l][None, :], "g_cq": w["mla_cq_g"][l][None, :],
        "g_mk": padg(w["mla_knorm_g"][l]), "g_mq": padg(w["mla_qnorm_g"][l]),
        "seg": _seg_matrix(GQA_HEADS * GQA_HD, GQA_HD),
        "pool_bd": pool_bd.astype(BF16), "pool_scale": w["pool_scale"][l][None, :],
        "hy_conv_w": w["hy_conv_w"][l], "hy_conv_b": w["hy_conv_b"][l][None, :],
        "hy_w1": jnp.pad(w["hy_f_w1"][l], ((0, LANES - nf), (0, 0))), "hy_b1": w["hy_f_b1"][l][None, :],
        "hy_w2": w["hy_f_w2"][l], "hy_b2": w["hy_f_b2"][l][None, :], "hy_w3": w["hy_f_w3"][l],
        "hy_freq": w["hy_freq"][l][None, :], "hy_bias": w["hy_bias"][l],
        "w_out": w["w_out"][l].reshape(4, -1, D).astype(BF16),
        "router_pad": jnp.pad(w["router_w"][l], ((0, 0), (0, LANES - N_EXPERTS))),
        "w_gate": w["exp_w_gate"][l].astype(BF16), "w_up": w["exp_w_up"][l].astype(BF16),
        "w_down": w["exp_w_down"][l].astype(BF16),
    }


def _moe(xs, sh, sc, gate, lp, tri):
    B, n, _ = xs.shape
    cap = EC_CAPACITY * n // N_EXPERTS
    hcm, idx, gts = _route_call(xs, sh, sc, lp, tri)
    idx_flat = jnp.swapaxes(idx[:, :, :N_EXPERTS], 1, 2).reshape(-1)
    moe_cm = _ffn_call(idx_flat, hcm, gts, lp, cap)
    return _moe_residual_call(xs, gate, moe_cm)


def kernel(x, c, ctx, c_ctx, norm1_g, norm2_g, w_mod, b_mod, w_in, w_out, pool_w, pool_scale, gqa_qnorm_g, gqa_knorm_g, hy_conv_w, hy_conv_b, hy_f_w1, hy_f_b1, hy_f_w2, hy_f_b2, hy_f_w3, hy_freq, hy_bias, mla_cq_g, mla_w_uq, mla_ckv_g, mla_w_ukv, mla_qnorm_g, mla_knorm_g, router_w, exp_w_gate, exp_w_up, exp_w_down):
    w = dict(norm1_g=norm1_g, norm2_g=norm2_g, w_in=w_in, w_out=w_out, pool_w=pool_w, pool_scale=pool_scale,
             gqa_qnorm_g=gqa_qnorm_g, gqa_knorm_g=gqa_knorm_g, hy_conv_w=hy_conv_w, hy_conv_b=hy_conv_b,
             hy_f_w1=hy_f_w1, hy_f_b1=hy_f_b1, hy_f_w2=hy_f_w2, hy_f_b2=hy_f_b2, hy_f_w3=hy_f_w3, hy_freq=hy_freq,
             hy_bias=hy_bias, mla_cq_g=mla_cq_g, mla_w_uq=mla_w_uq, mla_ckv_g=mla_ckv_g, mla_w_ukv=mla_w_ukv,
             mla_qnorm_g=mla_qnorm_g, mla_knorm_g=mla_knorm_g, router_w=router_w, exp_w_gate=exp_w_gate,
             exp_w_up=exp_w_up, exp_w_down=exp_w_down)
    B, n, D = x.shape
    nc = ctx.shape[1]
    depth = w_mod.shape[0]
    ch = hy_bias.shape[2]
    rows = -(-(B + 1) // SUBLANES) * SUBLANES
    cc = jnp.concatenate([c, c_ctx[None, :], jnp.zeros((rows - B - 1, D), F32)], axis=0)
    mod = _mod_call(cc, w_mod, b_mod)
    tabs_x = _rope_tables(n, True)
    tabs_c = _rope_tables(nc, False)
    hc_x = _hyena_consts(n, ch)
    hc_c = _hyena_consts(nc, ch)
    tri = jnp.asarray(np.triu(np.ones((PREFIX_BLOCK, PREFIX_BLOCK), np.float32)), dtype=BF16)
    xc = ctx
    for l in range(depth):
        last = l == depth - 1
        lp = _layer_params(l, w)
        m = mod[l]
        mx = [m[:B, i * D:(i + 1) * D][:, None, :] for i in range(6)]
        mc = [jnp.broadcast_to(m[B:B + 1, i * D:(i + 1) * D][:, None, :], (B, 1, D)) for i in range(6)]
        pc = _inproj_call(xc, mc[0], mc[1], lp, tabs_c, last)
        kc, vc, mkc, mvc = pc[:4]
        kx, vx, mkx, mvx, qx, mqx, poolx, hyx = _inproj_call(x, mx[0], mx[1], lp, tabs_x, False)
        cat = lambda a, b: jnp.concatenate([a, b], axis=1)
        o_gqa = _attn_call(qx, cat(kx, kc), cat(vx, vc), GQA_HEADS, GQA_KV_HEADS, GQA_HD, GQA_HD, "attn_gqa")
        o_mla = _attn_call(mqx, cat(mkx, mkc), cat(mvx, mvc), MLA_HEADS, MLA_HEADS, MLA_PAD, MLA_V, "attn_mla")
        o_pool = _pool_call(poolx, lp)
        o_hy = _hyena_call(hyx, lp, hc_x, _hyfilt_call(lp, hc_x))
        x_new = _outproj_call(x, mx[2], [o_pool, o_gqa, o_hy, o_mla], lp)
        x_new = _moe(x_new, mx[3], mx[4], mx[5], lp, tri)
        if not last:
            qc, mqc, poolc, hyc = pc[4:]
            oc_gqa = _attn_call(qc, kc, vc, GQA_HEADS, GQA_KV_HEADS, GQA_HD, GQA_HD, "attn_gqa_ctx")
            oc_mla = _attn_call(mqc, mkc, mvc, MLA_HEADS, MLA_HEADS, MLA_PAD, MLA_V, "attn_mla_ctx")
            oc_pool = _pool_call(poolc, lp)
            oc_hy = _hyena_call(hyc, lp, hc_c, _hyfilt_call(lp, hc_c))
            xc_new = _outproj_call(xc, mc[2], [oc_pool, oc_gqa, oc_hy, oc_mla], lp)
            xc = _moe(xc_new, mc[3], mc[4], mc[5], lp, tri)
        x = x_new
    return x
```

```python
import functools
import math

import numpy as np
import jax
import jax.numpy as jnp
from jax import lax
from jax.experimental import pallas as pl
from jax.experimental.pallas import tpu as pltpu

F32 = jnp.float32
BF16 = jnp.bfloat16
I32 = jnp.int32

GRID_W = 64
ROPE_THETA = 10000.0
EPS = 1e-6
POOL_WINDOWS = (2, 4, 8, 16)
GQA_HEADS = 4
GQA_KV_HEADS = 2
GQA_HD = 64
HY_BANDS = 8
HY_DECAY_TARGET = 1e-2
HY_FAST_PCT = 0.3
HY_SLOW_PCT = 1.5
MLA_HEADS = 4
MLA_NOPE = 64
MLA_ROPE = 32
MLA_V = 64
MLA_HD = MLA_NOPE + MLA_ROPE
N_EXPERTS = 16
EC_CAPACITY = 2

LANES = 128
SUBLANES = 8
MLA_PAD = LANES
VMEM_LIMIT = 56 * 2 ** 20
HIGHEST = lax.Precision.HIGHEST


def _cp(sem, vmem=VMEM_LIMIT):
    return pltpu.CompilerParams(dimension_semantics=sem, vmem_limit_bytes=vmem)


def _dot(a, b):
    return jnp.dot(a, b, preferred_element_type=F32)


def _dot_nt(a, b):
    return lax.dot_general(a, b, (((1,), (1,)), ((), ())), preferred_element_type=F32)


def _dot_split(x, m):
    hi = x.astype(BF16)
    lo = (x - hi.astype(F32)).astype(BF16)
    return _dot(hi, m) + _dot(lo, m)


def _rms_rows(x):
    return x * lax.rsqrt(jnp.mean(x * x, axis=-1, keepdims=True) + EPS)


def _rope(y, cos, sin_s, perm):
    pw = perm.shape[0]
    yb = y.astype(BF16)
    partner = jnp.concatenate([_dot(yb[:, j:j + pw], perm) for j in range(0, y.shape[1], pw)], axis=1)
    return y * cos + partner * sin_s


def _mod_kernel(c_ref, w_ref, b_ref, o_ref):
    c = c_ref[...]
    s = c * (1.0 / (1.0 + jnp.exp(-c)))
    o_ref[0] = jnp.dot(s, w_ref[0], precision=HIGHEST, preferred_element_type=F32) + b_ref[0]


def _mod_call(cc, w_mod, b_mod):
    L, D, N = w_mod.shape
    R = cc.shape[0]
    tn = 1536
    return pl.pallas_call(
        _mod_kernel,
        grid=(L, N // tn),
        in_specs=[
            pl.BlockSpec((R, D), lambda l, j: (0, 0)),
            pl.BlockSpec((1, D, tn), lambda l, j: (l, 0, j)),
            pl.BlockSpec((1, 1, tn), lambda l, j: (l, 0, j)),
        ],
        out_specs=pl.BlockSpec((1, R, tn), lambda l, j: (l, 0, j)),
        out_shape=jax.ShapeDtypeStruct((L, R, N), F32),
        compiler_params=_cp(("arbitrary", "arbitrary")),
        name="adaln_mod",
    )(cc, w_mod, b_mod.reshape(L, 1, N))


KV_COLS = 512
ALL_COLS = 2048


def _inproj_kernel(kv_only, x_ref, sh_ref, sc_ref, g1_ref, w_ref,
                   gk_ref, gq_ref, gckv_ref, wukv_ref, gmk_ref, gcq_ref, wuq_ref, gmq_ref,
                   seg_ref, pg_ref, pm_ref, cg_ref, sg_ref, cm_ref, sm_ref, *outs):
    if kv_only:
        k_ref, v_ref, mk_ref, mv_ref = outs
    else:
        k_ref, v_ref, mk_ref, mv_ref, q_ref, mq_ref, pool_ref, hy_ref = outs
    x = x_ref[0]
    h = _rms_rows(x) * g1_ref[...] * (1.0 + sc_ref[0]) + sh_ref[0]
    p = _dot(h.astype(BF16), w_ref[...])

    def seg_norm(y, seg):
        ss = _dot((y * y).astype(BF16), seg)
        return y * lax.rsqrt(ss * (1.0 / GQA_HD) + EPS)

    def head_norm(y):
        parts = []
        for hh in range(MLA_HEADS):
            s = y[:, hh * MLA_PAD:(hh + 1) * MLA_PAD]
            ms = jnp.sum(s * s, axis=-1, keepdims=True) * (1.0 / MLA_HD)
            parts.append(s * lax.rsqrt(ms + EPS))
        return jnp.concatenate(parts, axis=1)

    cm, sm, pm = cm_ref[...], sm_ref[...], pm_ref[...]
    kk = seg_norm(p[:, 0:128], seg_ref[0:128, 0:128]) * gk_ref[...]
    kk = _rope(kk, cg_ref[:, 0:128], sg_ref[:, 0:128], pg_ref[0:128, 0:128])
    k_ref[0] = kk.astype(BF16)
    v_ref[0] = p[:, 128:256].astype(BF16)
    ckv = _rms_rows(p[:, 256:384]) * gckv_ref[...]
    kv = _dot(ckv.astype(BF16), wukv_ref[...])
    kr = p[:, 384:512]
    kraw = kv[:, 0:MLA_HEADS * MLA_PAD] + jnp.concatenate([kr] * MLA_HEADS, axis=1)
    mk = _rope(head_norm(kraw) * gmk_ref[...], cm, sm, pm)
    mk_ref[0] = mk.astype(BF16)
    mv_ref[0] = kv[:, MLA_HEADS * MLA_PAD:].astype(BF16)
    if kv_only:
        return
    qq = seg_norm(p[:, 512:768], seg_ref[...]) * gq_ref[...]
    qq = _rope(qq, cg_ref[...], sg_ref[...], pg_ref[...])
    q_ref[0] = (qq * (GQA_HD ** -0.5)).astype(BF16)
    cq = _rms_rows(p[:, 768:1024]) * gcq_ref[...]
    mq = _rope(head_norm(_dot(cq.astype(BF16), wuq_ref[...])) * gmq_ref[...], cm, sm, pm)
    mq_ref[0] = (mq * (MLA_HD ** -0.5)).astype(BF16)
    pool_ref[0] = p[:, 1024:1280]
    hy_ref[0] = p[:, 1280:2048]


def _inproj_call(xs, sh, sc, lp, tabs, kv_only):
    B, n, D = xs.shape
    tm = min(n, 512)
    cols = KV_COLS if kv_only else ALL_COLS
    w = lp["w_in_r"][:, :cols]
    full = lambda a: pl.BlockSpec(a.shape, lambda i, b: (0,) * a.ndim)
    tab = lambda a: pl.BlockSpec((tm, a.shape[1]), lambda i, b: (i, 0))
    row = lambda width: pl.BlockSpec((1, tm, width), lambda i, b: (b, i, 0))
    params = [lp["g_k"], lp["g_q"], lp["g_ckv"], lp["w_ukv_r"], lp["g_mk"], lp["g_cq"], lp["w_uq_r"], lp["g_mq"],
              lp["seg"], lp["perm_g"], lp["perm_m"]]
    out_w = [(128, BF16), (128, BF16), (512, BF16), (256, BF16)]
    if not kv_only:
        out_w += [(256, BF16), (512, BF16), (256, F32), (768, F32)]
    return pl.pallas_call(
        functools.partial(_inproj_kernel, kv_only),
        grid=(n // tm, B),
        in_specs=[row(D),
                  pl.BlockSpec((1, 1, D), lambda i, b: (b, 0, 0)),
                  pl.BlockSpec((1, 1, D), lambda i, b: (b, 0, 0)),
                  full(lp["norm1_g"]), full(w)] + [full(a) for a in params] + [tab(a) for a in tabs],
        out_specs=[row(wd) for wd, _ in out_w],
        out_shape=[jax.ShapeDtypeStruct((B, n, wd), dt) for wd, dt in out_w],
        compiler_params=_cp(("arbitrary", "arbitrary")),
        name="inproj_kv" if kv_only else "inproj",
    )(xs, sh, sc, lp["norm1_g"], w, *params, *tabs)


def _attn_kernel(heads, kv_heads, dk, dv, nseg, q_ref, *refs):
    k_refs, vt_refs, o_ref = refs[:nseg], refs[nseg:2 * nseg], refs[2 * nseg]
    grp = heads // kv_heads
    outs = []
    for g in range(kv_heads):
        kgs = [k_ref[0, :, g * dk:(g + 1) * dk] for k_ref in k_refs]
        vts = [vt_ref[0, g * dv:(g + 1) * dv, :] for vt_ref in vt_refs]
        for j in range(grp):
            hh = g * grp + j
            qh = q_ref[0, :, hh * dk:(hh + 1) * dk]
            ss = [_dot_nt(kg, qh) for kg in kgs]
            m = functools.reduce(jnp.maximum, [jnp.max(s, axis=0, keepdims=True) for s in ss])
            es = [jnp.exp(s - m) for s in ss]
            l = functools.reduce(jnp.add, [jnp.sum(e, axis=0, keepdims=True) for e in es])
            ot = functools.reduce(jnp.add, [_dot(vt, e.astype(BF16)) for vt, e in zip(vts, es)])
            outs.append(ot / l)
    o_ref[0] = jnp.concatenate(outs, axis=0).T.astype(o_ref.dtype)


def _attn_call(q, ks, vts, heads, kv_heads, dk, dv, name):
    B, n, _ = q.shape
    tq = min(n, 256)
    kspec = lambda a: pl.BlockSpec((1,) + a.shape[1:], lambda b, i: (b, 0, 0))
    return pl.pallas_call(
        functools.partial(_attn_kernel, heads, kv_heads, dk, dv, len(ks)),
        grid=(B, n // tq),
        in_specs=[pl.BlockSpec((1, tq, heads * dk), lambda b, i: (b, i, 0))]
        + [kspec(a) for a in ks] + [kspec(a) for a in vts],
        out_specs=pl.BlockSpec((1, tq, heads * dv), lambda b, i: (b, i, 0)),
        out_shape=jax.ShapeDtypeStruct((B, n, heads * dv), BF16),
        compiler_params=_cp(("arbitrary", "arbitrary")),
        name=name,
    )(q, *ks, *vts)


POOL_HALO = 8


def _pool_kernel(n, tile, u_ref, wbd_ref, scale_ref, o_ref):
    kw = min(n, tile + 2 * POOL_HALO)
    gc = u_ref.shape[2] // len(POOL_WINDOWS)
    for i in range(n // tile):
        t0 = i * tile
        ws = min(max(t0 - POOL_HALO, 0), n - kw)
        uw = u_ref[0, ws:ws + kw, :]
        hi = uw.astype(BF16)
        lo = (uw - hi.astype(F32)).astype(BF16)
        t = t0 + lax.broadcasted_iota(I32, (tile, kw), 0)
        s = ws + lax.broadcasted_iota(I32, (tile, kw), 1)
        trow = t0 + lax.broadcasted_iota(I32, (tile, 1), 0)
        lane = lax.broadcasted_iota(I32, (tile, u_ref.shape[2]), 1)
        uc = u_ref[0, t0:t0 + tile, :]
        mean = jnp.zeros_like(uc)
        for gi, w in enumerate(POOL_WINDOWS):
            left = w // 2
            right = w - 1 - left
            band = jnp.where((s >= t - left) & (s <= t + right), 1.0, 0.0).astype(BF16)
            sm = _dot(band, hi) + _dot(band, lo)
            cnt = (jnp.minimum(trow + right, n - 1) - jnp.maximum(trow - left, 0) + 1).astype(F32)
            mean = jnp.where((lane >= gi * gc) & (lane < (gi + 1) * gc), sm / cnt, mean)
        d = mean - uc
        o_ref[0, t0:t0 + tile, :] = (_dot(d.astype(BF16), wbd_ref[...]) * scale_ref[...]).astype(o_ref.dtype)


def _pool_call(u, lp):
    B, n, W = u.shape
    return pl.pallas_call(
        functools.partial(_pool_kernel, n, min(n, 256)),
        grid=(B,),
        in_specs=[pl.BlockSpec((1, n, W), lambda b: (b, 0, 0)),
                  pl.BlockSpec((W, W), lambda b: (0, 0)),
                  pl.BlockSpec((1, W), lambda b: (0, 0))],
        out_specs=pl.BlockSpec((1, n, W), lambda b: (b, 0, 0)),
        out_shape=jax.ShapeDtypeStruct((B, n, W), BF16),
        compiler_params=_cp(("arbitrary",)),
        name="pool",
    )(u, lp["pool_bd"], lp["pool_scale"])


def _hyfilt_kernel(n, feats_ref, w1_ref, b1_ref, w2_ref, b2_ref, w3_ref, fr_ref, decay_ref, c_ref, s_ref,
                   hre_ref, him_ref, hn_ref, fsum_ref, fdif_ref):
    i = pl.program_id(0)
    tk = c_ref.shape[0]
    hc2 = fsum_ref.shape[1]

    @pl.when(i == 0)
    def _():
        fr = fr_ref[...]
        hid = jnp.sin(fr * (jnp.dot(feats_ref[...], w1_ref[...], precision=HIGHEST, preferred_element_type=F32)
                            + b1_ref[...]))
        hid = jnp.sin(fr * (jnp.dot(hid, w2_ref[...], precision=HIGHEST, preferred_element_type=F32) + b2_ref[...]))
        h = jnp.dot(hid, w3_ref[...], precision=HIGHEST, preferred_element_type=F32)
        dec = decay_ref[...]
        dec2 = jnp.concatenate([dec, dec], axis=1)
        row = lax.broadcasted_iota(I32, (n, 1), 0)
        fwd = h[:, 0:hc2] * dec2
        bwd = jnp.where(row == 0, 0.0, h[:, hc2:2 * hc2] * dec2)
        nrm = jnp.sum(fwd * fwd + bwd * bwd, axis=0, keepdims=True)
        scl = lax.rsqrt(nrm + EPS)
        fwd = fwd * scl
        bwd = bwd * scl
        fsum_ref[...] = fwd + bwd
        fdif_ref[...] = bwd - fwd
        alt = jnp.where(row % 2 == 0, 1.0, -1.0)
        hn_ref[...] = jnp.sum((fwd + bwd) * alt, axis=0, keepdims=True) * (1.0 / (2 * n))

    k = i * tk + lax.broadcasted_iota(I32, (tk, 1), 0)
    wk = jnp.where(k == 0, 1.0 / (2 * n), 2.0 / (2 * n))
    hre_ref[...] = wk * jnp.dot(c_ref[...], fsum_ref[...], precision=HIGHEST, preferred_element_type=F32)
    him_ref[...] = wk * jnp.dot(s_ref[...], fdif_ref[...], precision=HIGHEST, preferred_element_type=F32)


def _hyfilt_call(lp, hc):
    n = hc["n"]
    tk = min(n, 256)
    hc2 = lp["hy_w3"].shape[1] // 2
    full = lambda a: pl.BlockSpec(a.shape, lambda i: (0,) * a.ndim)
    args = [hc["feats"], lp["hy_w1"], lp["hy_b1"], lp["hy_w2"], lp["hy_b2"], lp["hy_w3"], lp["hy_freq"], hc["decay"]]
    return pl.pallas_call(
        functools.partial(_hyfilt_kernel, n),
        grid=(n // tk,),
        in_specs=[full(a) for a in args] + [pl.BlockSpec((tk, n), lambda i: (i, 0))] * 2,
        out_specs=[pl.BlockSpec((tk, hc2), lambda i: (i, 0)), pl.BlockSpec((tk, hc2), lambda i: (i, 0)),
                   pl.BlockSpec((1, hc2), lambda i: (0, 0))],
        out_shape=[jax.ShapeDtypeStruct((n, hc2), F32), jax.ShapeDtypeStruct((n, hc2), F32),
                   jax.ShapeDtypeStruct((1, hc2), F32)],
        scratch_shapes=[pltpu.VMEM((n, hc2), F32), pltpu.VMEM((n, hc2), F32)],
        compiler_params=_cp(("arbitrary",)),
        name="hyena_filter",
    )(*args, hc["c32"], hc["s32"])


def _hyena_kernel(n, rc, u_ref, cw_ref, cb_ref, c_ref, s_ref, hre_ref, him_ref, hn_ref, bias_ref, o_ref,
                  uc_ref, z_ref, p_ref, q_ref):
    ch = o_ref.shape[2]
    chunks = [(r0, lax.broadcasted_iota(I32, (rc, 1), 0) + r0) for r0 in range(0, n, rc)]
    for r0, row in chunks:
        u = u_ref[0, r0:r0 + rc, :]
        if r0 == 0:
            up = jnp.where(row == 0, 0.0, pltpu.roll(u, 1, axis=0))
        else:
            up = u_ref[0, r0 - 1:r0 - 1 + rc, :]
        if r0 + rc == n:
            un = jnp.where(row == n - 1, 0.0, pltpu.roll(u, rc - 1, axis=0))
        else:
            un = u_ref[0, r0 + 1:r0 + 1 + rc, :]
        uc_ref[r0:r0 + rc, :] = up * cw_ref[0:1, :] + u * cw_ref[1:2, :] + un * cw_ref[2:3, :] + cb_ref[...]

    def long_conv(o, emit):
        z = z_ref[...]
        zb = z.astype(BF16)
        alt_all = jnp.where(lax.broadcasted_iota(I32, (n, 1), 0) % 2 == 0, 1.0, -1.0)
        nyq = jnp.sum(z * alt_all, axis=0, keepdims=True) * hn_ref[:, o * ch:(o + 1) * ch]
        for r0, _ in chunks:
            a = _dot(c_ref[r0:r0 + rc, :], zb)
            b = _dot(s_ref[r0:r0 + rc, :], zb)
            hre = hre_ref[r0:r0 + rc, o * ch:(o + 1) * ch]
            him = him_ref[r0:r0 + rc, o * ch:(o + 1) * ch]
            p_ref[r0:r0 + rc, :] = (a * hre + b * him).astype(BF16)
            q_ref[r0:r0 + rc, :] = (b * hre - a * him).astype(BF16)
        pq = p_ref[...]
        qq = q_ref[...]
        for r0, row in chunks:
            y = _dot(c_ref[r0:r0 + rc, :], pq) + _dot(s_ref[r0:r0 + rc, :], qq)
            alt = jnp.where(row % 2 == 0, 1.0, -1.0)
            emit(r0, y + alt * nyq + z_ref[r0:r0 + rc, :] * bias_ref[o:o + 1, :])

    z_ref[...] = uc_ref[:, 0:ch]

    def emit1(r0, y):
        z_ref[r0:r0 + rc, :] = uc_ref[r0:r0 + rc, ch:2 * ch] * y

    def emit2(r0, y):
        o_ref[0, r0:r0 + rc, :] = (uc_ref[r0:r0 + rc, 2 * ch:3 * ch] * y).astype(o_ref.dtype)

    long_conv(0, emit1)
    long_conv(1, emit2)


def _hyena_call(u, lp, hc, filt):
    B, n, W = u.shape
    ch = W // 3
    hre, him, hn = filt
    once = lambda a: pl.BlockSpec(a.shape, lambda b: (0,) * a.ndim, pipeline_mode=pl.Buffered(1))
    return pl.pallas_call(
        functools.partial(_hyena_kernel, n, min(n, 512)),
        scratch_shapes=[pltpu.VMEM((n, W), F32), pltpu.VMEM((n, ch), F32),
                        pltpu.VMEM((n, ch), BF16), pltpu.VMEM((n, ch), BF16)],
        grid=(B,),
        in_specs=[pl.BlockSpec((1, n, W), lambda b: (b, 0, 0)),
                  once(lp["hy_conv_w"]), once(lp["hy_conv_b"]), once(hc["c16"]), once(hc["s16"]),
                  once(hre), once(him), once(hn), once(lp["hy_bias"])],
        out_specs=pl.BlockSpec((1, n, ch), lambda b: (b, 0, 0)),
        out_shape=jax.ShapeDtypeStruct((B, n, ch), BF16),
        compiler_params=_cp(("arbitrary",)),
        name="hyena",
    )(u, lp["hy_conv_w"], lp["hy_conv_b"], hc["c16"], hc["s16"], hre, him, hn, lp["hy_bias"])


def _outproj_kernel(x_ref, g_ref, a_ref, b_ref, c_ref, d_ref, w_ref, o_ref):
    mixed = (_dot(a_ref[0], w_ref[0]) + _dot(b_ref[0], w_ref[1])
             + _dot(c_ref[0], w_ref[2]) + _dot(d_ref[0], w_ref[3]))
    o_ref[0] = x_ref[0] + g_ref[0] * mixed


def _outproj_call(xs, gate, parts, lp):
    B, n, D = xs.shape
    tm = min(n, 512)
    gw = parts[0].shape[2]
    return pl.pallas_call(
        _outproj_kernel,
        grid=(B, n // tm),
        in_specs=[pl.BlockSpec((1, tm, D), lambda b, i: (b, i, 0)),
                  pl.BlockSpec((1, 1, D), lambda b, i: (b, 0, 0))]
        + [pl.BlockSpec((1, tm, gw), lambda b, i: (b, i, 0))] * 4
        + [pl.BlockSpec(lp["w_out"].shape, lambda b, i: (0, 0, 0))],
        out_specs=pl.BlockSpec((1, tm, D), lambda b, i: (b, i, 0)),
        out_shape=jax.ShapeDtypeStruct((B, n, D), F32),
        compiler_params=_cp(("arbitrary", "arbitrary")),
        name="outproj",
    )(xs, gate, *parts, lp["w_out"])


PREFIX_BLOCK = 256


def _prefix_lanes(m, tri):
    e, n = m.shape
    nb = n // PREFIX_BLOCK
    stk = jnp.concatenate([m[:, j * PREFIX_BLOCK:(j + 1) * PREFIX_BLOCK] for j in range(nb)], axis=0)
    pre = _dot(stk.astype(BF16), tri)
    outs = []
    carry = jnp.zeros((e, 1), F32)
    for j in range(nb):
        pj = pre[j * e:(j + 1) * e]
        outs.append(pj + carry)
        carry = carry + pj[:, PREFIX_BLOCK - 1:PREFIX_BLOCK]
    return jnp.concatenate(outs, axis=1)


SLOT_RADIX = 16


def _split3(x):
    a = x.astype(BF16).astype(F32)
    r = x - a
    b = r.astype(BF16).astype(F32)
    c = (r - b).astype(BF16).astype(F32)
    return a, b, c


def _digit_onehots(v, weights=None):
    e_n = v.shape[0]
    hi_d = jnp.floor(v * (1.0 / SLOT_RADIX))
    lo_d = v - SLOT_RADIX * hi_d
    dig = lax.broadcasted_iota(I32, (SLOT_RADIX, v.shape[1]), 0).astype(F32)
    his = [hi_d[e:e + 1, :] == dig for e in range(e_n)]
    lo = jnp.concatenate([jnp.where(lo_d[e:e + 1, :] == dig, 1.0, 0.0) for e in range(e_n)], axis=0)
    if weights is None:
        hi = jnp.concatenate([jnp.where(m, 1.0, 0.0) for m in his], axis=0)
    else:
        hi = jnp.concatenate([jnp.where(m, w[e:e + 1, :], 0.0) for w in weights for e, m in enumerate(his)], axis=0)
    return hi.astype(BF16), lo.astype(BF16)


def _route_kernel(n, cap, x_ref, sh_ref, sc_ref, g_ref, wr_ref, tri_ref, fold_ref, pre_ref, low_ref,
                  hcm_ref, idx_ref, gate_ref):
    x = x_ref[0]
    h = _rms_rows(x) * g_ref[...] * (1.0 + sc_ref[0]) + sh_ref[0]
    for c in range(h.shape[1] // LANES):
        hcm_ref[0, pl.ds(c, n, stride=SUBLANES), :] = h[:, c * LANES:(c + 1) * LANES]
    h_hi = h.astype(BF16)
    h_lo = (h - h_hi.astype(F32)).astype(BF16)
    rt = _dot(jnp.concatenate([h_hi, h_lo], axis=1), wr_ref[...]).T
    lt = rt[0:N_EXPERTS, :] + rt[N_EXPERTS:2 * N_EXPERTS, :]
    ex = jnp.exp(lt - jnp.max(lt, axis=0, keepdims=True))
    aff = ex / jnp.sum(ex, axis=0, keepdims=True)
    bits = pltpu.bitcast(aff, I32)
    thr = jnp.zeros((N_EXPERTS, 1), I32)
    for bit in range(30, -1, -1):
        cand = thr | (1 << bit)
        cnt = jnp.sum(jnp.where(bits >= cand, 1.0, 0.0), axis=1, keepdims=True)
        thr = jnp.where(cnt >= cap, cand, thr)
    gt = jnp.where(bits > thr, 1.0, 0.0)
    eq = jnp.where(bits == thr, 1.0, 0.0)
    need = cap - jnp.sum(gt, axis=1, keepdims=True)
    tri = tri_ref[...]
    sel = gt + eq * jnp.where(_prefix_lanes(eq, tri) <= need, 1.0, 0.0)
    incl = _prefix_lanes(sel, tri)
    rows = N_EXPERTS * SLOT_RADIX
    shift = SLOT_RADIX.bit_length() - 1
    same = ((lax.broadcasted_iota(I32, (rows, rows), 0) >> shift)
            == (lax.broadcasted_iota(I32, (rows, rows), 1) >> shift))
    fold = fold_ref[...]
    a_hi, a_lo = _digit_onehots(incl)
    hist = jnp.where(same, _dot_nt(a_hi, a_lo), 0.0)
    hist = _dot_split(hist, fold)
    pre = _dot_split(hist, pre_ref[...])
    tot = jnp.broadcast_to(pre[:, SLOT_RADIX - 1:SLOT_RADIX], pre.shape)
    t_hi = tot.astype(BF16)
    t_lo = (tot - t_hi.astype(F32)).astype(BF16)
    idx_ref[0] = (pre + _dot(low_ref[...], t_hi) + _dot(low_ref[...], t_lo)).astype(I32)
    g_hi, g_lo = _digit_onehots(incl - 1.0, _split3(sel * aff))
    gsum = _dot_nt(g_hi, g_lo)
    gsum = jnp.where(same, gsum[0:rows] + gsum[rows:2 * rows] + gsum[2 * rows:3 * rows], 0.0)
    gate_ref[0] = functools.reduce(jnp.add, [_dot(part.astype(BF16), fold) for part in _split3(gsum)])


def _route_consts():
    rows = N_EXPERTS * SLOT_RADIX
    i = np.arange(rows)
    j = np.arange(LANES)
    fold = (i[:, None] % SLOT_RADIX == j[None, :]).astype(np.float32)
    pre = ((j[:, None] <= j[None, :]) & (j[None, :] < SLOT_RADIX)).astype(np.float32)
    low = ((i[:, None] // SLOT_RADIX == i[None, :] // SLOT_RADIX) & (i[None, :] < i[:, None])).astype(np.float32)
    tri = np.triu(np.ones((PREFIX_BLOCK, PREFIX_BLOCK), np.float32))
    return tuple(jnp.asarray(a, dtype=BF16) for a in (tri, fold, pre, low))


def _route_call(xs, sh, sc, lp, consts):
    B, n, D = xs.shape
    cap = EC_CAPACITY * n // N_EXPERTS
    assert cap <= SLOT_RADIX * SLOT_RADIX
    rows = n * (D // LANES)
    srows = N_EXPERTS * SLOT_RADIX
    full = lambda a: pl.BlockSpec(a.shape, lambda b: (0,) * a.ndim)
    return pl.pallas_call(
        functools.partial(_route_kernel, n, cap),
        grid=(B,),
        in_specs=[pl.BlockSpec((1, n, D), lambda b: (b, 0, 0)),
                  pl.BlockSpec((1, 1, D), lambda b: (b, 0, 0)),
                  pl.BlockSpec((1, 1, D), lambda b: (b, 0, 0)),
                  full(lp["norm2_g"]), full(lp["router_split"])] + [full(a) for a in consts],
        out_specs=[pl.BlockSpec((1, rows, LANES), lambda b: (b, 0, 0)),
                   pl.BlockSpec((1, srows, LANES), lambda b: (b, 0, 0)),
                   pl.BlockSpec((1, srows, LANES), lambda b: (b, 0, 0))],
        out_shape=[jax.ShapeDtypeStruct((B, rows, LANES), F32),
                   jax.ShapeDtypeStruct((B, srows, LANES), I32),
                   jax.ShapeDtypeStruct((B, srows, LANES), F32)],
        compiler_params=_cp(("arbitrary",)),
        name="route",
    )(xs, sh, sc, lp["norm2_g"], lp["router_split"], *consts)


GATHER_UNROLL = 8


FFN_ROWS = 256


def _ffn_kernel(cap, bt, idx_ref, h_ref, gate_ref, wg_ref, wu_ref, wd_ref, o_ref, xg_ref, y_ref):
    b0 = pl.program_id(0) * bt
    e = pl.program_id(1)
    chunks = wg_ref.shape[1] // LANES
    m = bt * cap
    tile = lambda r: pl.ds(pl.multiple_of(r * SUBLANES, SUBLANES), SUBLANES)

    @pl.when(e == 0)
    def _():
        o_ref[...] = jnp.zeros_like(o_ref)

    for bi in range(bt):
        base = ((b0 + bi) * N_EXPERTS + e) * cap

        def gather(i, carry, bi=bi, base=base):
            for j in range(GATHER_UNROLL):
                r = i * GATHER_UNROLL + j
                xg_ref[tile(bi * cap + r), :] = h_ref[bi, tile(idx_ref[base + r]), :]
            return carry

        lax.fori_loop(0, cap // GATHER_UNROLL, gather, 0)
    xt = jnp.concatenate([xg_ref[pl.ds(c, m, stride=SUBLANES), :] for c in range(chunks)], axis=1).astype(BF16)
    a = _dot(xt, wg_ref[0])
    u = _dot(xt, wu_ref[0])
    hmid = (a * (1.0 / (1.0 + jnp.exp(-a))) * u).astype(BF16)
    lane = lax.broadcasted_iota(I32, (m, LANES), 1)
    gcol = jnp.sum(jnp.where(lane == e, gate_ref[...].reshape(m, LANES), 0.0), axis=1, keepdims=True)
    y = _dot(hmid, wd_ref[0]) * gcol
    for c in range(chunks):
        y_ref[pl.ds(c, m, stride=SUBLANES), :] = y[:, c * LANES:(c + 1) * LANES]
    for bi in range(bt):
        base = ((b0 + bi) * N_EXPERTS + e) * cap

        def scatter(i, carry, bi=bi, base=base):
            upd = []
            for j in range(GATHER_UNROLL):
                r = i * GATHER_UNROLL + j
                dst = tile(idx_ref[base + r])
                upd.append((dst, o_ref[bi, dst, :] + y_ref[tile(bi * cap + r), :]))
            for dst, v in upd:
                o_ref[bi, dst, :] = v
            return carry

        lax.fori_loop(0, cap // GATHER_UNROLL, scatter, 0)


def _ffn_call(idx_flat, hcm, gate, lp, cap):
    B, rows, _ = hcm.shape
    D = lp["w_gate"].shape[1]
    ff = lp["w_gate"].shape[2]
    bt = min(B, max(1, FFN_ROWS // cap))
    m = bt * cap
    return pl.pallas_call(
        functools.partial(_ffn_kernel, cap, bt),
        grid=(B // bt, N_EXPERTS),
        in_specs=[pl.BlockSpec(memory_space=pltpu.SMEM),
                  pl.BlockSpec((bt, rows, LANES), lambda b, e: (b, 0, 0)),
                  pl.BlockSpec((bt, cap, LANES), lambda b, e: (b, 0, 0)),
                  pl.BlockSpec((1, D, ff), lambda b, e: (e, 0, 0)),
                  pl.BlockSpec((1, D, ff), lambda b, e: (e, 0, 0)),
                  pl.BlockSpec((1, ff, D), lambda b, e: (e, 0, 0))],
        out_specs=pl.BlockSpec((bt, rows, LANES), lambda b, e: (b, 0, 0)),
        out_shape=jax.ShapeDtypeStruct((B, rows, LANES), F32),
        scratch_shapes=[pltpu.VMEM((m * SUBLANES, LANES), F32), pltpu.VMEM((m * SUBLANES, LANES), F32)],
        compiler_params=_cp(("arbitrary", "arbitrary")),
        name="expert_ffn",
    )(idx_flat, hcm, gate, lp["w_gate"], lp["w_up"], lp["w_down"])


def _moe_residual_kernel(x_ref, g_ref, m_ref, o_ref):
    tm = x_ref.shape[1]
    chunks = x_ref.shape[2] // LANES
    moe = jnp.concatenate([m_ref[0, pl.ds(c, tm, stride=SUBLANES), :] for c in range(chunks)], axis=1)
    o_ref[0] = x_ref[0] + g_ref[0] * moe


def _moe_residual_call(xs, gate, moe_cm):
    B, n, D = xs.shape
    tm = min(n, 512)
    return pl.pallas_call(
        _moe_residual_kernel,
        grid=(B, n // tm),
        in_specs=[pl.BlockSpec((1, tm, D), lambda b, i: (b, i, 0)),
                  pl.BlockSpec((1, 1, D), lambda b, i: (b, 0, 0)),
                  pl.BlockSpec((1, tm * (D // LANES), LANES), lambda b, i: (b, i, 0))],
        out_specs=pl.BlockSpec((1, tm, D), lambda b, i: (b, i, 0)),
        out_shape=jax.ShapeDtypeStruct((B, n, D), F32),
        compiler_params=_cp(("arbitrary", "arbitrary")),
        name="moe_residual",
    )(xs, gate, moe_cm)


def _axial_tables(n, head_pat, half, reps, roped):
    t = jnp.arange(n)
    pos = (jnp.floor_divide(t, GRID_W).astype(F32), jnp.remainder(t, GRID_W).astype(F32))
    inv = ROPE_THETA ** (-jnp.arange(half, dtype=F32) / half)
    cols_c, cols_s = [], []
    one = jnp.ones((n,), F32)
    zero = jnp.zeros((n,), F32)
    for pat in head_pat:
        if pat is None or not roped:
            cols_c.append(one), cols_s.append(zero)
            continue
        axis, f, first = pat
        ang = pos[axis] * inv[f]
        cols_c.append(jnp.cos(ang))
        cols_s.append(-jnp.sin(ang) if first else jnp.sin(ang))
    mk = lambda cols: jnp.tile(jnp.stack(cols, axis=1), (1, reps))
    return mk(cols_c), mk(cols_s)


def _partner_matrix(head_pat, half, reps):
    hw = len(head_pat)
    p = np.zeros((hw * reps, hw * reps), np.float32)
    for r in range(reps):
        for j, pat in enumerate(head_pat):
            if pat is not None:
                p[r * hw + (j + half if pat[2] else j - half), r * hw + j] = 1.0
    return jnp.asarray(p, dtype=BF16)


def _rope_patterns():
    gh = GQA_HD // 4
    gpat = [(j // (2 * gh), j % gh, (j % (2 * gh)) < gh) for j in range(GQA_HD)]
    mh = MLA_ROPE // 4
    mpat = [None] * MLA_NOPE + [(j // (2 * mh), j % mh, (j % (2 * mh)) < mh) for j in range(MLA_ROPE)]
    mpat += [None] * (MLA_PAD - MLA_HD)
    return gpat, gh, mpat, mh


def _rope_tables(n, roped):
    gpat, gh, mpat, mh = _rope_patterns()
    return _axial_tables(n, gpat, gh, GQA_HEADS, roped) + _axial_tables(n, mpat, mh, MLA_HEADS, roped)


def _rope_partners():
    gpat, gh, mpat, mh = _rope_patterns()
    return _partner_matrix(gpat, gh, GQA_HEADS), _partner_matrix(mpat, mh, 2)


def _hyena_consts(n, ch):
    f32 = F32
    k = jnp.arange(n, dtype=I32)
    ang = jnp.remainder(k[:, None] * k[None, :], 2 * n).astype(f32) * (math.pi / n)
    c32, s32 = jnp.cos(ang), jnp.sin(ang)
    t = jnp.linspace(0.0, 1.0, n, dtype=f32)[:, None]
    lag = jnp.arange(n, dtype=f32)[:, None]
    bands = jnp.linspace(1e-4, HY_BANDS - 1, HY_BANDS, dtype=f32)[None, :]
    a = (2.0 * math.pi / n) * lag * bands
    feats = jnp.concatenate([t, jnp.cos(a), -jnp.sin(a)], axis=-1)
    feats = jnp.pad(feats, ((0, 0), (0, LANES - feats.shape[1])))
    deltas = jnp.abs(jnp.linspace(math.log(HY_DECAY_TARGET) / HY_SLOW_PCT,
                                  math.log(HY_DECAY_TARGET) / HY_FAST_PCT, ch, dtype=f32))
    decay = jnp.exp(-t * deltas[None, :])
    return {"n": n, "c32": c32, "s32": s32, "c16": c32.astype(BF16), "s16": s32.astype(BF16),
            "feats": feats, "decay": decay}


def _seg_matrix(width, seg):
    i = np.arange(width)
    return jnp.asarray((i[:, None] // seg == i[None, :] // seg).astype(np.float32), dtype=BF16)


def _router_split(rw):
    d = rw.shape[0]
    hi = rw.astype(BF16)
    lo = (rw - hi.astype(F32)).astype(BF16)
    top = jnp.concatenate([hi, lo, jnp.zeros((d, LANES - 2 * N_EXPERTS), BF16)], axis=1)
    bot = jnp.concatenate([hi, jnp.zeros((d, LANES - N_EXPERTS), BF16)], axis=1)
    return jnp.concatenate([top, bot], axis=0)


def _layer_params(l, w):
    D = w["w_in"].shape[1]
    wi = w["w_in"][l]
    z = lambda c: jnp.zeros((D, c), F32)
    w_in_r = jnp.concatenate(
        [wi[:, 0:384], z(MLA_NOPE), wi[:, 384:416], z(MLA_PAD - MLA_HD), wi[:, 416:]], axis=1).astype(BF16)
    ukv = w["mla_w_ukv"][l].reshape(-1, MLA_HEADS, MLA_NOPE + MLA_V)
    kpart = jnp.pad(ukv[:, :, :MLA_NOPE], ((0, 0), (0, 0), (0, MLA_PAD - MLA_NOPE)))
    w_ukv_r = jnp.concatenate([kpart.reshape(ukv.shape[0], -1), ukv[:, :, MLA_NOPE:].reshape(ukv.shape[0], -1)],
                              axis=1).astype(BF16)
    uq = w["mla_w_uq"][l].reshape(-1, MLA_HEADS, MLA_HD)
    w_uq_r = jnp.pad(uq, ((0, 0), (0, 0), (0, MLA_PAD - MLA_HD))).reshape(uq.shape[0], -1).astype(BF16)
    padg = lambda g: jnp.tile(jnp.pad(g, (0, MLA_PAD - MLA_HD)), MLA_HEADS)[None, :]
    pw = w["pool_w"][l]
    gc = pw.shape[1]
    pool_bd = jnp.zeros((len(POOL_WINDOWS) * gc,) * 2, F32)
    for gi in range(len(POOL_WINDOWS)):
        pool_bd = pool_bd.at[gi * gc:(gi + 1) * gc, gi * gc:(gi + 1) * gc].set(pw[gi])
    nf = w["hy_f_w1"].shape[1]
    perm_g, perm_m = _rope_partners()
    return {
        "perm_g": perm_g, "perm_m": perm_m,
        "norm1_g": w["norm1_g"][l][None, :], "norm2_g": w["norm2_g"][l][None, :],
        "w_in_r": w_in_r, "w_ukv_r": w_ukv_r, "w_uq_r": w_uq_r,
        "g_k": jnp.tile(w["gqa_knorm_g"][l], GQA_KV_HEADS)[None, :],
        "g_q": jnp.tile(w["gqa_qnorm_g"][l], GQA_HEADS)[None, :],
        "g_ckv": w["mla_ckv_g"][l][None, :], "g_cq": w["mla_cq_g"][l][None, :],
        "g_mk": padg(w["mla_knorm_g"][l]), "g_mq": padg(w["mla_qnorm_g"][l]),
        "seg": _seg_matrix(GQA_HEADS * GQA_HD, GQA_HD),
        "pool_bd": pool_bd.astype(BF16), "pool_scale": w["pool_scale"][l][None, :],
        "hy_conv_w": w["hy_conv_w"][l], "hy_conv_b": w["hy_conv_b"][l][None, :],
        "hy_w1": jnp.pad(w["hy_f_w1"][l], ((0, LANES - nf), (0, 0))), "hy_b1": w["hy_f_b1"][l][None, :],
        "hy_w2": w["hy_f_w2"][l], "hy_b2": w["hy_f_b2"][l][None, :], "hy_w3": w["hy_f_w3"][l],
        "hy_freq": w["hy_freq"][l][None, :], "hy_bias": w["hy_bias"][l],
        "w_out": w["w_out"][l].reshape(4, -1, D).astype(BF16),
        "router_split": _router_split(w["router_w"][l]),
        "w_gate": w["exp_w_gate"][l].astype(BF16), "w_up": w["exp_w_up"][l].astype(BF16),
        "w_down": w["exp_w_down"][l].astype(BF16),
    }


def _moe(xs, sh, sc, gate, lp, consts):
    B, n, _ = xs.shape
    cap = EC_CAPACITY * n // N_EXPERTS
    hcm, idx_t, gate_t = _route_call(xs, sh, sc, lp, consts)
    unfold = lambda a: a[:, :, :SLOT_RADIX].reshape(B, N_EXPERTS, SLOT_RADIX * SLOT_RADIX)[:, :, :cap]
    idx_flat = unfold(idx_t).reshape(-1)
    gts = jnp.pad(jnp.swapaxes(unfold(gate_t), 1, 2), ((0, 0), (0, 0), (0, LANES - N_EXPERTS)))
    moe_cm = _ffn_call(idx_flat, hcm, gts, lp, cap)
    return _moe_residual_call(xs, gate, moe_cm)


def kernel(x, c, ctx, c_ctx, norm1_g, norm2_g, w_mod, b_mod, w_in, w_out, pool_w, pool_scale, gqa_qnorm_g, gqa_knorm_g, hy_conv_w, hy_conv_b, hy_f_w1, hy_f_b1, hy_f_w2, hy_f_b2, hy_f_w3, hy_freq, hy_bias, mla_cq_g, mla_w_uq, mla_ckv_g, mla_w_ukv, mla_qnorm_g, mla_knorm_g, router_w, exp_w_gate, exp_w_up, exp_w_down):
    w = dict(norm1_g=norm1_g, norm2_g=norm2_g, w_in=w_in, w_out=w_out, pool_w=pool_w, pool_scale=pool_scale,
             gqa_qnorm_g=gqa_qnorm_g, gqa_knorm_g=gqa_knorm_g, hy_conv_w=hy_conv_w, hy_conv_b=hy_conv_b,
             hy_f_w1=hy_f_w1, hy_f_b1=hy_f_b1, hy_f_w2=hy_f_w2, hy_f_b2=hy_f_b2, hy_f_w3=hy_f_w3, hy_freq=hy_freq,
             hy_bias=hy_bias, mla_cq_g=mla_cq_g, mla_w_uq=mla_w_uq, mla_ckv_g=mla_ckv_g, mla_w_ukv=mla_w_ukv,
             mla_qnorm_g=mla_qnorm_g, mla_knorm_g=mla_knorm_g, router_w=router_w, exp_w_gate=exp_w_gate,
             exp_w_up=exp_w_up, exp_w_down=exp_w_down)
    B, n, D = x.shape
    nc = ctx.shape[1]
    depth = w_mod.shape[0]
    ch = hy_bias.shape[2]
    rows = -(-(B + 1) // SUBLANES) * SUBLANES
    cc = jnp.concatenate([c, c_ctx[None, :], jnp.zeros((rows - B - 1, D), F32)], axis=0)
    mod = _mod_call(cc, w_mod, b_mod)
    tabs_x = _rope_tables(n, True)
    tabs_c = _rope_tables(nc, False)
    hc_x = _hyena_consts(n, ch)
    hc_c = _hyena_consts(nc, ch)
    tri = _route_consts()
    xc = ctx
    for l in range(depth):
        last = l == depth - 1
        lp = _layer_params(l, w)
        m = mod[l]
        mx = [m[:B, i * D:(i + 1) * D][:, None, :] for i in range(6)]
        mc = [jnp.broadcast_to(m[B:B + 1, i * D:(i + 1) * D][:, None, :], (B, 1, D)) for i in range(6)]
        pc = _inproj_call(xc, mc[0], mc[1], lp, tabs_c, last)
        kc, vc, mkc, mvc = pc[:4]
        kx, vx, mkx, mvx, qx, mqx, poolx, hyx = _inproj_call(x, mx[0], mx[1], lp, tabs_x, False)
        tr = lambda a: jnp.swapaxes(a, 1, 2)
        vtc, mvtc = tr(vc), tr(mvc)
        o_gqa = _attn_call(qx, [kx, kc], [tr(vx), vtc], GQA_HEADS, GQA_KV_HEADS, GQA_HD, GQA_HD, "attn_gqa")
        o_mla = _attn_call(mqx, [mkx, mkc], [tr(mvx), mvtc], MLA_HEADS, MLA_HEADS, MLA_PAD, MLA_V, "attn_mla")
        o_pool = _pool_call(poolx, lp)
        o_hy = _hyena_call(hyx, lp, hc_x, _hyfilt_call(lp, hc_x))
        x_new = _outproj_call(x, mx[2], [o_pool, o_gqa, o_hy, o_mla], lp)
        x_new = _moe(x_new, mx[3], mx[4], mx[5], lp, tri)
        if not last:
            qc, mqc, poolc, hyc = pc[4:]
            oc_gqa = _attn_call(qc, [kc], [vtc], GQA_HEADS, GQA_KV_HEADS, GQA_HD, GQA_HD, "attn_gqa_ctx")
            oc_mla = _attn_call(mqc, [mkc], [mvtc], MLA_HEADS, MLA_HEADS, MLA_PAD, MLA_V, "attn_mla_ctx")
            oc_pool = _pool_call(poolc, lp)
            oc_hy = _hyena_call(hyc, lp, hc_c, _hyfilt_call(lp, hc_c))
            xc_new = _outproj_call(xc, mc[2], [oc_pool, oc_gqa, oc_hy, oc_mla], lp)
            xc = _moe(xc_new, mc[3], mc[4], mc[5], lp, tri)
        x = x_new
    return x
```

```python
import functools
import math

import numpy as np
import jax
import jax.numpy as jnp
from jax import lax
from jax.experimental import pallas as pl
from jax.experimental.pallas import tpu as pltpu

F32 = jnp.float32
BF16 = jnp.bfloat16
I32 = jnp.int32

GRID_W = 64
ROPE_THETA = 10000.0
EPS = 1e-6
POOL_WINDOWS = (2, 4, 8, 16)
GQA_HEADS = 4
GQA_KV_HEADS = 2
GQA_HD = 64
HY_BANDS = 8
HY_DECAY_TARGET = 1e-2
HY_FAST_PCT = 0.3
HY_SLOW_PCT = 1.5
MLA_HEADS = 4
MLA_NOPE = 64
MLA_ROPE = 32
MLA_V = 64
MLA_HD = MLA_NOPE + MLA_ROPE
N_EXPERTS = 16
EC_CAPACITY = 2

LANES = 128
SUBLANES = 8
MLA_PAD = LANES
VMEM_LIMIT = 56 * 2 ** 20
HIGHEST = lax.Precision.HIGHEST


def _cp(sem, vmem=VMEM_LIMIT):
    return pltpu.CompilerParams(dimension_semantics=sem, vmem_limit_bytes=vmem)


def _dot(a, b):
    return jnp.dot(a, b, preferred_element_type=F32)


def _dot_nt(a, b):
    return lax.dot_general(a, b, (((1,), (1,)), ((), ())), preferred_element_type=F32)


def _dot_split(x, m):
    hi = x.astype(BF16)
    lo = (x - hi.astype(F32)).astype(BF16)
    return _dot(hi, m) + _dot(lo, m)


def _rms_rows(x):
    return x * lax.rsqrt(jnp.mean(x * x, axis=-1, keepdims=True) + EPS)


def _rope(y, cos, sin_s, perm):
    pw = perm.shape[0]
    yb = y.astype(BF16)
    partner = jnp.concatenate([_dot(yb[:, j:j + pw], perm) for j in range(0, y.shape[1], pw)], axis=1)
    return y * cos + partner * sin_s


def _mod_kernel(c_ref, w_ref, b_ref, o_ref):
    c = c_ref[...]
    s = c * (1.0 / (1.0 + jnp.exp(-c)))
    o_ref[0] = jnp.dot(s, w_ref[0], precision=HIGHEST, preferred_element_type=F32) + b_ref[0]


def _mod_call(cc, w_mod, b_mod):
    L, D, N = w_mod.shape
    R = cc.shape[0]
    tn = 1536
    return pl.pallas_call(
        _mod_kernel,
        grid=(L, N // tn),
        in_specs=[
            pl.BlockSpec((R, D), lambda l, j: (0, 0)),
            pl.BlockSpec((1, D, tn), lambda l, j: (l, 0, j)),
            pl.BlockSpec((1, 1, tn), lambda l, j: (l, 0, j)),
        ],
        out_specs=pl.BlockSpec((1, R, tn), lambda l, j: (l, 0, j)),
        out_shape=jax.ShapeDtypeStruct((L, R, N), F32),
        compiler_params=_cp(("arbitrary", "arbitrary")),
        name="adaln_mod",
    )(cc, w_mod, b_mod.reshape(L, 1, N))


KV_COLS = 512
ALL_COLS = 2048


def _inproj_kernel(kv_only, x_ref, sh_ref, sc_ref, g1_ref, w_ref,
                   gk_ref, gq_ref, gckv_ref, wukv_ref, gmk_ref, gcq_ref, wuq_ref, gmq_ref,
                   seg_ref, pg_ref, pm_ref, cg_ref, sg_ref, cm_ref, sm_ref, *outs):
    if kv_only:
        k_ref, v_ref, mk_ref, mv_ref = outs
    else:
        k_ref, v_ref, mk_ref, mv_ref, q_ref, mq_ref, pool_ref, hy_ref = outs
    x = x_ref[0]
    h = _rms_rows(x) * g1_ref[...] * (1.0 + sc_ref[0]) + sh_ref[0]
    p = _dot(h.astype(BF16), w_ref[...])

    def seg_norm(y, seg):
        ss = _dot((y * y).astype(BF16), seg)
        return y * lax.rsqrt(ss * (1.0 / GQA_HD) + EPS)

    def head_norm(y):
        parts = []
        for hh in range(MLA_HEADS):
            s = y[:, hh * MLA_PAD:(hh + 1) * MLA_PAD]
            ms = jnp.sum(s * s, axis=-1, keepdims=True) * (1.0 / MLA_HD)
            parts.append(s * lax.rsqrt(ms + EPS))
        return jnp.concatenate(parts, axis=1)

    cm, sm, pm = cm_ref[...], sm_ref[...], pm_ref[...]
    kk = seg_norm(p[:, 0:128], seg_ref[0:128, 0:128]) * gk_ref[...]
    kk = _rope(kk, cg_ref[:, 0:128], sg_ref[:, 0:128], pg_ref[0:128, 0:128])
    k_ref[0] = kk.astype(BF16)
    v_ref[0] = p[:, 128:256].astype(BF16)
    ckv = _rms_rows(p[:, 256:384]) * gckv_ref[...]
    kv = _dot(ckv.astype(BF16), wukv_ref[...])
    kr = p[:, 384:512]
    kraw = kv[:, 0:MLA_HEADS * MLA_PAD] + jnp.concatenate([kr] * MLA_HEADS, axis=1)
    mk = _rope(head_norm(kraw) * gmk_ref[...], cm, sm, pm)
    mk_ref[0] = mk.astype(BF16)
    mv_ref[0] = kv[:, MLA_HEADS * MLA_PAD:].astype(BF16)
    if kv_only:
        return
    qq = seg_norm(p[:, 512:768], seg_ref[...]) * gq_ref[...]
    qq = _rope(qq, cg_ref[...], sg_ref[...], pg_ref[...])
    q_ref[0] = (qq * (GQA_HD ** -0.5 * LOG2E)).astype(BF16)
    cq = _rms_rows(p[:, 768:1024]) * gcq_ref[...]
    mq = _rope(head_norm(_dot(cq.astype(BF16), wuq_ref[...])) * gmq_ref[...], cm, sm, pm)
    mq_ref[0] = (mq * (MLA_HD ** -0.5 * LOG2E)).astype(BF16)
    pool_ref[0] = p[:, 1024:1280]
    hy_ref[0] = p[:, 1280:2048]


def _inproj_call(xs, sh, sc, lp, tabs, kv_only):
    B, n, D = xs.shape
    tm = min(n, 512)
    cols = KV_COLS if kv_only else ALL_COLS
    w = lp["w_in_r"][:, :cols]
    full = lambda a: pl.BlockSpec(a.shape, lambda i, b: (0,) * a.ndim)
    tab = lambda a: pl.BlockSpec((tm, a.shape[1]), lambda i, b: (i, 0))
    row = lambda width: pl.BlockSpec((1, tm, width), lambda i, b: (b, i, 0))
    params = [lp["g_k"], lp["g_q"], lp["g_ckv"], lp["w_ukv_r"], lp["g_mk"], lp["g_cq"], lp["w_uq_r"], lp["g_mq"],
              lp["seg"], lp["perm_g"], lp["perm_m"]]
    out_w = [(128, BF16), (128, BF16), (512, BF16), (256, BF16)]
    if not kv_only:
        out_w += [(256, BF16), (512, BF16), (256, F32), (768, F32)]
    return pl.pallas_call(
        functools.partial(_inproj_kernel, kv_only),
        grid=(n // tm, B),
        in_specs=[row(D),
                  pl.BlockSpec((1, 1, D), lambda i, b: (b, 0, 0)),
                  pl.BlockSpec((1, 1, D), lambda i, b: (b, 0, 0)),
                  full(lp["norm1_g"]), full(w)] + [full(a) for a in params] + [tab(a) for a in tabs],
        out_specs=[row(wd) for wd, _ in out_w],
        out_shape=[jax.ShapeDtypeStruct((B, n, wd), dt) for wd, dt in out_w],
        compiler_params=_cp(("arbitrary", "arbitrary")),
        name="inproj_kv" if kv_only else "inproj",
    )(xs, sh, sc, lp["norm1_g"], w, *params, *tabs)


ATTN_KEY_BLOCK = 256
ATTN_Q_TILE = 512
ATTN_ONES_ROWS = 16
LOG2E = math.log2(math.e)


def _attn_kernel(heads, kv_heads, dv, nseg, q_ref, *refs):
    k_refs, vt_refs, o_ref = refs[:nseg], refs[nseg:2 * nseg], refs[2 * nseg]
    grp = heads // kv_heads
    kb = ATTN_KEY_BLOCK
    dve = dv + ATTN_ONES_ROWS
    blocks = [(i, r) for i in range(nseg) for r in range(0, k_refs[i].shape[2], kb)]

    qs = [q_ref[0, hh] for hh in range(heads)]
    run_max = [None] * heads
    acc = [None] * heads
    s_prev = [None] * heads
    for j in range(len(blocks) + 1):
        s_cur = [None] * heads
        for hh in range(heads):
            g = hh // grp
            if j < len(blocks):
                i, r = blocks[j]
                s_cur[hh] = _dot_nt(k_refs[i][0, g, r:r + kb, :], qs[hh])
            if j >= 1:
                i, r = blocks[j - 1]
                bm = jnp.max(s_prev[hh], axis=0, keepdims=True)
                m_new = bm if run_max[hh] is None else jnp.maximum(run_max[hh], bm)
                t = _dot(vt_refs[i][0, g * dve:(g + 1) * dve, r:r + kb], jnp.exp2(s_prev[hh] - m_new).astype(BF16))
                acc[hh] = t if acc[hh] is None else acc[hh] * jnp.exp2(run_max[hh] - m_new) + t
                run_max[hh] = m_new
        s_prev = s_cur
    o_ref[0] = jnp.concatenate([a[0:dv] / a[dv:dv + 1] for a in acc], axis=0).T.astype(o_ref.dtype)


def _attn_call(q, ks, vts, heads, kv_heads, dv, name):
    B, _, n, dk = q.shape
    tq = min(n, ATTN_Q_TILE)
    kspec = lambda a: pl.BlockSpec((1,) + a.shape[1:], lambda b, i: (b,) + (0,) * (a.ndim - 1))
    return pl.pallas_call(
        functools.partial(_attn_kernel, heads, kv_heads, dv, len(ks)),
        grid=(B, n // tq),
        in_specs=[pl.BlockSpec((1, heads, tq, dk), lambda b, i: (b, 0, i, 0))]
        + [kspec(a) for a in ks] + [kspec(a) for a in vts],
        out_specs=pl.BlockSpec((1, tq, heads * dv), lambda b, i: (b, i, 0)),
        out_shape=jax.ShapeDtypeStruct((B, n, heads * dv), BF16),
        compiler_params=_cp(("arbitrary", "arbitrary")),
        name=name,
    )(q, *ks, *vts)


POOL_HALO = 8


def _pool_kernel(n, tile, u_ref, wbd_ref, scale_ref, o_ref):
    kw = min(n, tile + 2 * POOL_HALO)
    gc = u_ref.shape[2] // len(POOL_WINDOWS)
    for i in range(n // tile):
        t0 = i * tile
        ws = min(max(t0 - POOL_HALO, 0), n - kw)
        uw = u_ref[0, ws:ws + kw, :]
        hi = uw.astype(BF16)
        lo = (uw - hi.astype(F32)).astype(BF16)
        t = t0 + lax.broadcasted_iota(I32, (tile, kw), 0)
        s = ws + lax.broadcasted_iota(I32, (tile, kw), 1)
        trow = t0 + lax.broadcasted_iota(I32, (tile, 1), 0)
        lane = lax.broadcasted_iota(I32, (tile, u_ref.shape[2]), 1)
        uc = u_ref[0, t0:t0 + tile, :]
        mean = jnp.zeros_like(uc)
        for gi, w in enumerate(POOL_WINDOWS):
            left = w // 2
            right = w - 1 - left
            band = jnp.where((s >= t - left) & (s <= t + right), 1.0, 0.0).astype(BF16)
            sm = _dot(band, hi) + _dot(band, lo)
            cnt = (jnp.minimum(trow + right, n - 1) - jnp.maximum(trow - left, 0) + 1).astype(F32)
            mean = jnp.where((lane >= gi * gc) & (lane < (gi + 1) * gc), sm / cnt, mean)
        d = mean - uc
        o_ref[0, t0:t0 + tile, :] = (_dot(d.astype(BF16), wbd_ref[...]) * scale_ref[...]).astype(o_ref.dtype)


def _pool_call(u, lp):
    B, n, W = u.shape
    return pl.pallas_call(
        functools.partial(_pool_kernel, n, min(n, 256)),
        grid=(B,),
        in_specs=[pl.BlockSpec((1, n, W), lambda b: (b, 0, 0)),
                  pl.BlockSpec((W, W), lambda b: (0, 0)),
                  pl.BlockSpec((1, W), lambda b: (0, 0))],
        out_specs=pl.BlockSpec((1, n, W), lambda b: (b, 0, 0)),
        out_shape=jax.ShapeDtypeStruct((B, n, W), BF16),
        compiler_params=_cp(("arbitrary",)),
        name="pool",
    )(u, lp["pool_bd"], lp["pool_scale"])


def _hyfilt_kernel(n, feats_ref, w1_ref, b1_ref, w2_ref, b2_ref, w3_ref, fr_ref, decay_ref, c_ref, s_ref,
                   hre_ref, him_ref, hn_ref, fsum_ref, fdif_ref):
    i = pl.program_id(0)
    tk = c_ref.shape[0]
    hc2 = fsum_ref.shape[1]

    @pl.when(i == 0)
    def _():
        fr = fr_ref[...]
        hid = jnp.sin(fr * (jnp.dot(feats_ref[...], w1_ref[...], precision=HIGHEST, preferred_element_type=F32)
                            + b1_ref[...]))
        hid = jnp.sin(fr * (jnp.dot(hid, w2_ref[...], precision=HIGHEST, preferred_element_type=F32) + b2_ref[...]))
        h = jnp.dot(hid, w3_ref[...], precision=HIGHEST, preferred_element_type=F32)
        dec = decay_ref[...]
        dec2 = jnp.concatenate([dec, dec], axis=1)
        row = lax.broadcasted_iota(I32, (n, 1), 0)
        fwd = h[:, 0:hc2] * dec2
        bwd = jnp.where(row == 0, 0.0, h[:, hc2:2 * hc2] * dec2)
        nrm = jnp.sum(fwd * fwd + bwd * bwd, axis=0, keepdims=True)
        scl = lax.rsqrt(nrm + EPS)
        fwd = fwd * scl
        bwd = bwd * scl
        fsum_ref[...] = fwd + bwd
        fdif_ref[...] = bwd - fwd
        alt = jnp.where(row % 2 == 0, 1.0, -1.0)
        hn_ref[...] = jnp.sum((fwd + bwd) * alt, axis=0, keepdims=True) * (1.0 / (2 * n))

    k = i * tk + lax.broadcasted_iota(I32, (tk, 1), 0)
    wk = jnp.where(k == 0, 1.0 / (2 * n), 2.0 / (2 * n))
    hre_ref[...] = wk * jnp.dot(c_ref[...], fsum_ref[...], precision=HIGHEST, preferred_element_type=F32)
    him_ref[...] = wk * jnp.dot(s_ref[...], fdif_ref[...], precision=HIGHEST, preferred_element_type=F32)


def _hyfilt_call(lp, hc):
    n = hc["n"]
    tk = min(n, 256)
    hc2 = lp["hy_w3"].shape[1] // 2
    full = lambda a: pl.BlockSpec(a.shape, lambda i: (0,) * a.ndim)
    args = [hc["feats"], lp["hy_w1"], lp["hy_b1"], lp["hy_w2"], lp["hy_b2"], lp["hy_w3"], lp["hy_freq"], hc["decay"]]
    return pl.pallas_call(
        functools.partial(_hyfilt_kernel, n),
        grid=(n // tk,),
        in_specs=[full(a) for a in args] + [pl.BlockSpec((tk, n), lambda i: (i, 0))] * 2,
        out_specs=[pl.BlockSpec((tk, hc2), lambda i: (i, 0)), pl.BlockSpec((tk, hc2), lambda i: (i, 0)),
                   pl.BlockSpec((1, hc2), lambda i: (0, 0))],
        out_shape=[jax.ShapeDtypeStruct((n, hc2), F32), jax.ShapeDtypeStruct((n, hc2), F32),
                   jax.ShapeDtypeStruct((1, hc2), F32)],
        scratch_shapes=[pltpu.VMEM((n, hc2), F32), pltpu.VMEM((n, hc2), F32)],
        compiler_params=_cp(("arbitrary",)),
        name="hyena_filter",
    )(*args, hc["c32"], hc["s32"])


def _hyena_kernel(n, rc, u_ref, cw_ref, cb_ref, c_ref, s_ref, hre_ref, him_ref, hn_ref, bias_ref, o_ref,
                  uc_ref, z_ref, p_ref, q_ref):
    ch = o_ref.shape[2]
    chunks = [(r0, lax.broadcasted_iota(I32, (rc, 1), 0) + r0) for r0 in range(0, n, rc)]
    for r0, row in chunks:
        u = u_ref[0, r0:r0 + rc, :]
        if r0 == 0:
            up = jnp.where(row == 0, 0.0, pltpu.roll(u, 1, axis=0))
        else:
            up = u_ref[0, r0 - 1:r0 - 1 + rc, :]
        if r0 + rc == n:
            un = jnp.where(row == n - 1, 0.0, pltpu.roll(u, rc - 1, axis=0))
        else:
            un = u_ref[0, r0 + 1:r0 + 1 + rc, :]
        uc_ref[r0:r0 + rc, :] = up * cw_ref[0:1, :] + u * cw_ref[1:2, :] + un * cw_ref[2:3, :] + cb_ref[...]

    def long_conv(o, emit):
        z = z_ref[...]
        zb = z.astype(BF16)
        alt_all = jnp.where(lax.broadcasted_iota(I32, (n, 1), 0) % 2 == 0, 1.0, -1.0)
        nyq = jnp.sum(z * alt_all, axis=0, keepdims=True) * hn_ref[:, o * ch:(o + 1) * ch]
        for r0, _ in chunks:
            a = _dot(c_ref[r0:r0 + rc, :], zb)
            b = _dot(s_ref[r0:r0 + rc, :], zb)
            hre = hre_ref[r0:r0 + rc, o * ch:(o + 1) * ch]
            him = him_ref[r0:r0 + rc, o * ch:(o + 1) * ch]
            p_ref[r0:r0 + rc, :] = (a * hre + b * him).astype(BF16)
            q_ref[r0:r0 + rc, :] = (b * hre - a * him).astype(BF16)
        pq = p_ref[...]
        qq = q_ref[...]
        for r0, row in chunks:
            y = _dot(c_ref[r0:r0 + rc, :], pq) + _dot(s_ref[r0:r0 + rc, :], qq)
            alt = jnp.where(row % 2 == 0, 1.0, -1.0)
            emit(r0, y + alt * nyq + z_ref[r0:r0 + rc, :] * bias_ref[o:o + 1, :])

    z_ref[...] = uc_ref[:, 0:ch]

    def emit1(r0, y):
        z_ref[r0:r0 + rc, :] = uc_ref[r0:r0 + rc, ch:2 * ch] * y

    def emit2(r0, y):
        o_ref[0, r0:r0 + rc, :] = (uc_ref[r0:r0 + rc, 2 * ch:3 * ch] * y).astype(o_ref.dtype)

    long_conv(0, emit1)
    long_conv(1, emit2)


def _hyena_call(u, lp, hc, filt):
    B, n, W = u.shape
    ch = W // 3
    hre, him, hn = filt
    once = lambda a: pl.BlockSpec(a.shape, lambda b: (0,) * a.ndim, pipeline_mode=pl.Buffered(1))
    return pl.pallas_call(
        functools.partial(_hyena_kernel, n, min(n, 512)),
        scratch_shapes=[pltpu.VMEM((n, W), F32), pltpu.VMEM((n, ch), F32),
                        pltpu.VMEM((n, ch), BF16), pltpu.VMEM((n, ch), BF16)],
        grid=(B,),
        in_specs=[pl.BlockSpec((1, n, W), lambda b: (b, 0, 0)),
                  once(lp["hy_conv_w"]), once(lp["hy_conv_b"]), once(hc["c16"]), once(hc["s16"]),
                  once(hre), once(him), once(hn), once(lp["hy_bias"])],
        out_specs=pl.BlockSpec((1, n, ch), lambda b: (b, 0, 0)),
        out_shape=jax.ShapeDtypeStruct((B, n, ch), BF16),
        compiler_params=_cp(("arbitrary",)),
        name="hyena",
    )(u, lp["hy_conv_w"], lp["hy_conv_b"], hc["c16"], hc["s16"], hre, him, hn, lp["hy_bias"])


def _outproj_kernel(x_ref, g_ref, a_ref, b_ref, c_ref, d_ref, w_ref, o_ref):
    mixed = (_dot(a_ref[0], w_ref[0]) + _dot(b_ref[0], w_ref[1])
             + _dot(c_ref[0], w_ref[2]) + _dot(d_ref[0], w_ref[3]))
    o_ref[0] = x_ref[0] + g_ref[0] * mixed


def _outproj_call(xs, gate, parts, lp):
    B, n, D = xs.shape
    tm = min(n, 512)
    gw = parts[0].shape[2]
    return pl.pallas_call(
        _outproj_kernel,
        grid=(B, n // tm),
        in_specs=[pl.BlockSpec((1, tm, D), lambda b, i: (b, i, 0)),
                  pl.BlockSpec((1, 1, D), lambda b, i: (b, 0, 0))]
        + [pl.BlockSpec((1, tm, gw), lambda b, i: (b, i, 0))] * 4
        + [pl.BlockSpec(lp["w_out"].shape, lambda b, i: (0, 0, 0))],
        out_specs=pl.BlockSpec((1, tm, D), lambda b, i: (b, i, 0)),
        out_shape=jax.ShapeDtypeStruct((B, n, D), F32),
        compiler_params=_cp(("arbitrary", "arbitrary")),
        name="outproj",
    )(xs, gate, *parts, lp["w_out"])


PREFIX_BLOCK = 256


def _prefix_lanes(m, tri):
    e, n = m.shape
    nb = n // PREFIX_BLOCK
    stk = jnp.concatenate([m[:, j * PREFIX_BLOCK:(j + 1) * PREFIX_BLOCK] for j in range(nb)], axis=0)
    pre = _dot(stk.astype(BF16), tri)
    outs = []
    carry = jnp.zeros((e, 1), F32)
    for j in range(nb):
        pj = pre[j * e:(j + 1) * e]
        outs.append(pj + carry)
        carry = carry + pj[:, PREFIX_BLOCK - 1:PREFIX_BLOCK]
    return jnp.concatenate(outs, axis=1)


SLOT_RADIX = 16


def _split3(x):
    a = x.astype(BF16).astype(F32)
    r = x - a
    b = r.astype(BF16).astype(F32)
    c = (r - b).astype(BF16).astype(F32)
    return a, b, c


def _digit_onehots(v, weights=None):
    e_n = v.shape[0]
    hi_d = jnp.floor(v * (1.0 / SLOT_RADIX))
    lo_d = v - SLOT_RADIX * hi_d
    dig = lax.broadcasted_iota(I32, (SLOT_RADIX, v.shape[1]), 0).astype(F32)
    his = [hi_d[e:e + 1, :] == dig for e in range(e_n)]
    lo = jnp.concatenate([jnp.where(lo_d[e:e + 1, :] == dig, 1.0, 0.0) for e in range(e_n)], axis=0)
    if weights is None:
        hi = jnp.concatenate([jnp.where(m, 1.0, 0.0) for m in his], axis=0)
    else:
        hi = jnp.concatenate([jnp.where(m, w[e:e + 1, :], 0.0) for w in weights for e, m in enumerate(his)], axis=0)
    return hi.astype(BF16), lo.astype(BF16)


def _route_kernel(n, cap, x_ref, sh_ref, sc_ref, g_ref, wr_ref, tri_ref, fold_ref, pre_ref, low_ref,
                  hcm_ref, idx_ref, gate_ref):
    x = x_ref[0]
    h = _rms_rows(x) * g_ref[...] * (1.0 + sc_ref[0]) + sh_ref[0]
    for c in range(h.shape[1] // LANES):
        hcm_ref[0, pl.ds(c, n, stride=SUBLANES), :] = h[:, c * LANES:(c + 1) * LANES]
    h_hi = h.astype(BF16)
    h_lo = (h - h_hi.astype(F32)).astype(BF16)
    rt = _dot(jnp.concatenate([h_hi, h_lo], axis=1), wr_ref[...]).T
    lt = rt[0:N_EXPERTS, :] + rt[N_EXPERTS:2 * N_EXPERTS, :]
    ex = jnp.exp(lt - jnp.max(lt, axis=0, keepdims=True))
    aff = ex / jnp.sum(ex, axis=0, keepdims=True)
    bits = pltpu.bitcast(aff, I32)
    thr = jnp.zeros((N_EXPERTS, 1), I32)
    for bit in range(30, -1, -1):
        cand = thr | (1 << bit)
        cnt = jnp.sum(jnp.where(bits >= cand, 1.0, 0.0), axis=1, keepdims=True)
        thr = jnp.where(cnt >= cap, cand, thr)
    gt = jnp.where(bits > thr, 1.0, 0.0)
    eq = jnp.where(bits == thr, 1.0, 0.0)
    need = cap - jnp.sum(gt, axis=1, keepdims=True)
    tri = tri_ref[...]
    sel = gt + eq * jnp.where(_prefix_lanes(eq, tri) <= need, 1.0, 0.0)
    incl = _prefix_lanes(sel, tri)
    rows = N_EXPERTS * SLOT_RADIX
    shift = SLOT_RADIX.bit_length() - 1
    same = ((lax.broadcasted_iota(I32, (rows, rows), 0) >> shift)
            == (lax.broadcasted_iota(I32, (rows, rows), 1) >> shift))
    fold = fold_ref[...]
    a_hi, a_lo = _digit_onehots(incl)
    hist = jnp.where(same, _dot_nt(a_hi, a_lo), 0.0)
    hist = _dot_split(hist, fold)
    pre = _dot_split(hist, pre_ref[...])
    tot = jnp.broadcast_to(pre[:, SLOT_RADIX - 1:SLOT_RADIX], pre.shape)
    t_hi = tot.astype(BF16)
    t_lo = (tot - t_hi.astype(F32)).astype(BF16)
    idx_ref[0] = (pre + _dot(low_ref[...], t_hi) + _dot(low_ref[...], t_lo)).astype(I32)
    g_hi, g_lo = _digit_onehots(incl - 1.0, _split3(sel * aff))
    gsum = _dot_nt(g_hi, g_lo)
    gsum = jnp.where(same, gsum[0:rows] + gsum[rows:2 * rows] + gsum[2 * rows:3 * rows], 0.0)
    gate_ref[0] = functools.reduce(jnp.add, [_dot(part.astype(BF16), fold) for part in _split3(gsum)])


def _route_consts():
    rows = N_EXPERTS * SLOT_RADIX
    i = np.arange(rows)
    j = np.arange(LANES)
    fold = (i[:, None] % SLOT_RADIX == j[None, :]).astype(np.float32)
    pre = ((j[:, None] <= j[None, :]) & (j[None, :] < SLOT_RADIX)).astype(np.float32)
    low = ((i[:, None] // SLOT_RADIX == i[None, :] // SLOT_RADIX) & (i[None, :] < i[:, None])).astype(np.float32)
    tri = np.triu(np.ones((PREFIX_BLOCK, PREFIX_BLOCK), np.float32))
    return tuple(jnp.asarray(a, dtype=BF16) for a in (tri, fold, pre, low))


def _route_call(xs, sh, sc, lp, consts):
    B, n, D = xs.shape
    cap = EC_CAPACITY * n // N_EXPERTS
    assert cap <= SLOT_RADIX * SLOT_RADIX
    rows = n * (D // LANES)
    srows = N_EXPERTS * SLOT_RADIX
    full = lambda a: pl.BlockSpec(a.shape, lambda b: (0,) * a.ndim)
    return pl.pallas_call(
        functools.partial(_route_kernel, n, cap),
        grid=(B,),
        in_specs=[pl.BlockSpec((1, n, D), lambda b: (b, 0, 0)),
                  pl.BlockSpec((1, 1, D), lambda b: (b, 0, 0)),
                  pl.BlockSpec((1, 1, D), lambda b: (b, 0, 0)),
                  full(lp["norm2_g"]), full(lp["router_split"])] + [full(a) for a in consts],
        out_specs=[pl.BlockSpec((1, rows, LANES), lambda b: (b, 0, 0)),
                   pl.BlockSpec((1, srows, LANES), lambda b: (b, 0, 0)),
                   pl.BlockSpec((1, srows, LANES), lambda b: (b, 0, 0))],
        out_shape=[jax.ShapeDtypeStruct((B, rows, LANES), F32),
                   jax.ShapeDtypeStruct((B, srows, LANES), I32),
                   jax.ShapeDtypeStruct((B, srows, LANES), F32)],
        compiler_params=_cp(("arbitrary",)),
        name="route",
    )(xs, sh, sc, lp["norm2_g"], lp["router_split"], *consts)


GATHER_UNROLL = 8


FFN_ROWS = 256


def _ffn_kernel(cap, bt, idx_ref, h_ref, gate_ref, wg_ref, wu_ref, wd_ref, o_ref, xga_ref, xgb_ref, ya_ref, yb_ref):
    b0 = pl.program_id(0) * bt
    k = pl.program_id(1)
    steps = pl.num_programs(1)
    chunks = wg_ref.shape[1] // LANES
    m = bt * cap
    tile = lambda r: pl.ds(pl.multiple_of(r * SUBLANES, SUBLANES), SUBLANES)
    e0 = 2 * k
    e1 = e0 + 1
    wrap = lambda e: jnp.where(e < 0, e + N_EXPERTS, jnp.where(e >= N_EXPERTS, e - N_EXPERTS, e))

    def gather(e, xg_ref):
        for bi in range(bt):
            base = ((b0 + bi) * N_EXPERTS + e) * cap
            for r in range(cap):
                xg_ref[tile(bi * cap + r), :] = h_ref[bi, tile(idx_ref[base + r]), :]

    def scatter(e, y_ref):
        for bi in range(bt):
            base = ((b0 + bi) * N_EXPERTS + e) * cap
            for r0 in range(0, cap, GATHER_UNROLL):
                upd = []
                for r in range(r0, r0 + GATHER_UNROLL):
                    dst = tile(idx_ref[base + r])
                    upd.append((dst, o_ref[bi, dst, :] + y_ref[tile(bi * cap + r), :]))
                for dst, v in upd:
                    o_ref[bi, dst, :] = v

    def ffn(e, w, xg_ref, y_ref):
        xt = jnp.concatenate([xg_ref[pl.ds(c, m, stride=SUBLANES), :] for c in range(chunks)], axis=1).astype(BF16)
        a = _dot(xt, wg_ref[w])
        u = _dot(xt, wu_ref[w])
        hmid = (a * (1.0 / (1.0 + jnp.exp(-a))) * u).astype(BF16)
        lane = lax.broadcasted_iota(I32, (m, LANES), 1)
        gcol = jnp.sum(jnp.where(lane == e, gate_ref[...].reshape(m, LANES), 0.0), axis=1, keepdims=True)
        y = _dot(hmid, wd_ref[w]) * gcol
        for c in range(chunks):
            y_ref[pl.ds(c, m, stride=SUBLANES), :] = y[:, c * LANES:(c + 1) * LANES]

    @pl.when(k == 0)
    def _():
        o_ref[...] = jnp.zeros_like(o_ref)
        yb_ref[...] = jnp.zeros_like(yb_ref)
        gather(e0, xga_ref)

    ffn(e0, 0, xga_ref, ya_ref)
    gather(e1, xgb_ref)
    scatter(wrap(e0 - 1), yb_ref)
    ffn(e1, 1, xgb_ref, yb_ref)
    gather(wrap(e1 + 1), xga_ref)
    scatter(e0, ya_ref)

    @pl.when(k == steps - 1)
    def _():
        scatter(e1, yb_ref)


def _ffn_call(idx_flat, hcm, gate, lp, cap):
    B, rows, _ = hcm.shape
    D = lp["w_gate"].shape[1]
    ff = lp["w_gate"].shape[2]
    bt = min(B, max(1, FFN_ROWS // cap))
    m = bt * cap
    return pl.pallas_call(
        functools.partial(_ffn_kernel, cap, bt),
        grid=(B // bt, N_EXPERTS // 2),
        in_specs=[pl.BlockSpec(memory_space=pltpu.SMEM),
                  pl.BlockSpec((bt, rows, LANES), lambda b, k: (b, 0, 0)),
                  pl.BlockSpec((bt, cap, LANES), lambda b, k: (b, 0, 0)),
                  pl.BlockSpec((2, D, ff), lambda b, k: (k, 0, 0)),
                  pl.BlockSpec((2, D, ff), lambda b, k: (k, 0, 0)),
                  pl.BlockSpec((2, ff, D), lambda b, k: (k, 0, 0))],
        out_specs=pl.BlockSpec((bt, rows, LANES), lambda b, k: (b, 0, 0)),
        out_shape=jax.ShapeDtypeStruct((B, rows, LANES), F32),
        scratch_shapes=[pltpu.VMEM((m * SUBLANES, LANES), F32)] * 4,
        compiler_params=_cp(("arbitrary", "arbitrary")),
        name="expert_ffn",
    )(idx_flat, hcm, gate, lp["w_gate"], lp["w_up"], lp["w_down"])


def _moe_residual_kernel(x_ref, g_ref, m_ref, o_ref):
    tm = x_ref.shape[1]
    chunks = x_ref.shape[2] // LANES
    moe = jnp.concatenate([m_ref[0, pl.ds(c, tm, stride=SUBLANES), :] for c in range(chunks)], axis=1)
    o_ref[0] = x_ref[0] + g_ref[0] * moe


def _moe_residual_call(xs, gate, moe_cm):
    B, n, D = xs.shape
    tm = min(n, 512)
    return pl.pallas_call(
        _moe_residual_kernel,
        grid=(B, n // tm),
        in_specs=[pl.BlockSpec((1, tm, D), lambda b, i: (b, i, 0)),
                  pl.BlockSpec((1, 1, D), lambda b, i: (b, 0, 0)),
                  pl.BlockSpec((1, tm * (D // LANES), LANES), lambda b, i: (b, i, 0))],
        out_specs=pl.BlockSpec((1, tm, D), lambda b, i: (b, i, 0)),
        out_shape=jax.ShapeDtypeStruct((B, n, D), F32),
        compiler_params=_cp(("arbitrary", "arbitrary")),
        name="moe_residual",
    )(xs, gate, moe_cm)


def _axial_tables(n, head_pat, half, reps, roped):
    t = jnp.arange(n)
    pos = (jnp.floor_divide(t, GRID_W).astype(F32), jnp.remainder(t, GRID_W).astype(F32))
    inv = ROPE_THETA ** (-jnp.arange(half, dtype=F32) / half)
    cols_c, cols_s = [], []
    one = jnp.ones((n,), F32)
    zero = jnp.zeros((n,), F32)
    for pat in head_pat:
        if pat is None or not roped:
            cols_c.append(one), cols_s.append(zero)
            continue
        axis, f, first = pat
        ang = pos[axis] * inv[f]
        cols_c.append(jnp.cos(ang))
        cols_s.append(-jnp.sin(ang) if first else jnp.sin(ang))
    mk = lambda cols: jnp.tile(jnp.stack(cols, axis=1), (1, reps))
    return mk(cols_c), mk(cols_s)


def _partner_matrix(head_pat, half, reps):
    hw = len(head_pat)
    p = np.zeros((hw * reps, hw * reps), np.float32)
    for r in range(reps):
        for j, pat in enumerate(head_pat):
            if pat is not None:
                p[r * hw + (j + half if pat[2] else j - half), r * hw + j] = 1.0
    return jnp.asarray(p, dtype=BF16)


def _rope_patterns():
    gh = GQA_HD // 4
    gpat = [(j // (2 * gh), j % gh, (j % (2 * gh)) < gh) for j in range(GQA_HD)]
    mh = MLA_ROPE // 4
    mpat = [None] * MLA_NOPE + [(j // (2 * mh), j % mh, (j % (2 * mh)) < mh) for j in range(MLA_ROPE)]
    mpat += [None] * (MLA_PAD - MLA_HD)
    return gpat, gh, mpat, mh


def _rope_tables(n, roped):
    gpat, gh, mpat, mh = _rope_patterns()
    return _axial_tables(n, gpat, gh, GQA_HEADS, roped) + _axial_tables(n, mpat, mh, MLA_HEADS, roped)


def _rope_partners():
    gpat, gh, mpat, mh = _rope_patterns()
    return _partner_matrix(gpat, gh, GQA_HEADS), _partner_matrix(mpat, mh, 2)


def _hyena_consts(n, ch):
    f32 = F32
    k = jnp.arange(n, dtype=I32)
    ang = jnp.remainder(k[:, None] * k[None, :], 2 * n).astype(f32) * (math.pi / n)
    c32, s32 = jnp.cos(ang), jnp.sin(ang)
    t = jnp.linspace(0.0, 1.0, n, dtype=f32)[:, None]
    lag = jnp.arange(n, dtype=f32)[:, None]
    bands = jnp.linspace(1e-4, HY_BANDS - 1, HY_BANDS, dtype=f32)[None, :]
    a = (2.0 * math.pi / n) * lag * bands
    feats = jnp.concatenate([t, jnp.cos(a), -jnp.sin(a)], axis=-1)
    feats = jnp.pad(feats, ((0, 0), (0, LANES - feats.shape[1])))
    deltas = jnp.abs(jnp.linspace(math.log(HY_DECAY_TARGET) / HY_SLOW_PCT,
                                  math.log(HY_DECAY_TARGET) / HY_FAST_PCT, ch, dtype=f32))
    decay = jnp.exp(-t * deltas[None, :])
    return {"n": n, "c32": c32, "s32": s32, "c16": c32.astype(BF16), "s16": s32.astype(BF16),
            "feats": feats, "decay": decay}


def _seg_matrix(width, seg):
    i = np.arange(width)
    return jnp.asarray((i[:, None] // seg == i[None, :] // seg).astype(np.float32), dtype=BF16)


def _router_split(rw):
    d = rw.shape[0]
    hi = rw.astype(BF16)
    lo = (rw - hi.astype(F32)).astype(BF16)
    top = jnp.concatenate([hi, lo, jnp.zeros((d, LANES - 2 * N_EXPERTS), BF16)], axis=1)
    bot = jnp.concatenate([hi, jnp.zeros((d, LANES - N_EXPERTS), BF16)], axis=1)
    return jnp.concatenate([top, bot], axis=0)


def _layer_params(l, w):
    D = w["w_in"].shape[1]
    wi = w["w_in"][l]
    z = lambda c: jnp.zeros((D, c), F32)
    w_in_r = jnp.concatenate(
        [wi[:, 0:384], z(MLA_NOPE), wi[:, 384:416], z(MLA_PAD - MLA_HD), wi[:, 416:]], axis=1).astype(BF16)
    ukv = w["mla_w_ukv"][l].reshape(-1, MLA_HEADS, MLA_NOPE + MLA_V)
    kpart = jnp.pad(ukv[:, :, :MLA_NOPE], ((0, 0), (0, 0), (0, MLA_PAD - MLA_NOPE)))
    w_ukv_r = jnp.concatenate([kpart.reshape(ukv.shape[0], -1), ukv[:, :, MLA_NOPE:].reshape(ukv.shape[0], -1)],
                              axis=1).astype(BF16)
    uq = w["mla_w_uq"][l].reshape(-1, MLA_HEADS, MLA_HD)
    w_uq_r = jnp.pad(uq, ((0, 0), (0, 0), (0, MLA_PAD - MLA_HD))).reshape(uq.shape[0], -1).astype(BF16)
    padg = lambda g: jnp.tile(jnp.pad(g, (0, MLA_PAD - MLA_HD)), MLA_HEADS)[None, :]
    pw = w["pool_w"][l]
    gc = pw.shape[1]
    pool_bd = jnp.zeros((len(POOL_WINDOWS) * gc,) * 2, F32)
    for gi in range(len(POOL_WINDOWS)):
        pool_bd = pool_bd.at[gi * gc:(gi + 1) * gc, gi * gc:(gi + 1) * gc].set(pw[gi])
    nf = w["hy_f_w1"].shape[1]
    perm_g, perm_m = _rope_partners()
    return {
        "perm_g": perm_g, "perm_m": perm_m,
        "norm1_g": w["norm1_g"][l][None, :], "norm2_g": w["norm2_g"][l][None, :],
        "w_in_r": w_in_r, "w_ukv_r": w_ukv_r, "w_uq_r": w_uq_r,
        "g_k": jnp.tile(w["gqa_knorm_g"][l], GQA_KV_HEADS)[None, :],
        "g_q": jnp.tile(w["gqa_qnorm_g"][l], GQA_HEADS)[None, :],
        "g_ckv": w["mla_ckv_g"][l][None, :], "g_cq": w["mla_cq_g"][l][None, :],
        "g_mk": padg(w["mla_knorm_g"][l]), "g_mq": padg(w["mla_qnorm_g"][l]),
        "seg": _seg_matrix(GQA_HEADS * GQA_HD, GQA_HD),
        "pool_bd": pool_bd.astype(BF16), "pool_scale": w["pool_scale"][l][None, :],
        "hy_conv_w": w["hy_conv_w"][l], "hy_conv_b": w["hy_conv_b"][l][None, :],
        "hy_w1": jnp.pad(w["hy_f_w1"][l], ((0, LANES - nf), (0, 0))), "hy_b1": w["hy_f_b1"][l][None, :],
        "hy_w2": w["hy_f_w2"][l], "hy_b2": w["hy_f_b2"][l][None, :], "hy_w3": w["hy_f_w3"][l],
        "hy_freq": w["hy_freq"][l][None, :], "hy_bias": w["hy_bias"][l],
        "w_out": w["w_out"][l].reshape(4, -1, D).astype(BF16),
        "router_split": _router_split(w["router_w"][l]),
        "w_gate": w["exp_w_gate"][l].astype(BF16), "w_up": w["exp_w_up"][l].astype(BF16),
        "w_down": w["exp_w_down"][l].astype(BF16),
    }


def _vt_ext(v, h):
    b, n, w = v.shape
    vt = jnp.swapaxes(v.reshape(b, n, h, w // h), 1, 3)
    vt = jnp.swapaxes(vt, 1, 2)
    ones = jnp.ones((b, h, ATTN_ONES_ROWS, n), v.dtype)
    return jnp.concatenate([vt, ones], axis=2).reshape(b, h * (w // h + ATTN_ONES_ROWS), n)


def _moe(xs, sh, sc, gate, lp, consts):
    B, n, _ = xs.shape
    cap = EC_CAPACITY * n // N_EXPERTS
    hcm, idx_t, gate_t = _route_call(xs, sh, sc, lp, consts)
    unfold = lambda a: a[:, :, :SLOT_RADIX].reshape(B, N_EXPERTS, SLOT_RADIX * SLOT_RADIX)[:, :, :cap]
    idx_flat = unfold(idx_t).reshape(-1)
    gts = jnp.pad(jnp.swapaxes(unfold(gate_t), 1, 2), ((0, 0), (0, 0), (0, LANES - N_EXPERTS)))
    moe_cm = _ffn_call(idx_flat, hcm, gts, lp, cap)
    return _moe_residual_call(xs, gate, moe_cm)


def kernel(x, c, ctx, c_ctx, norm1_g, norm2_g, w_mod, b_mod, w_in, w_out, pool_w, pool_scale, gqa_qnorm_g, gqa_knorm_g, hy_conv_w, hy_conv_b, hy_f_w1, hy_f_b1, hy_f_w2, hy_f_b2, hy_f_w3, hy_freq, hy_bias, mla_cq_g, mla_w_uq, mla_ckv_g, mla_w_ukv, mla_qnorm_g, mla_knorm_g, router_w, exp_w_gate, exp_w_up, exp_w_down):
    w = dict(norm1_g=norm1_g, norm2_g=norm2_g, w_in=w_in, w_out=w_out, pool_w=pool_w, pool_scale=pool_scale,
             gqa_qnorm_g=gqa_qnorm_g, gqa_knorm_g=gqa_knorm_g, hy_conv_w=hy_conv_w, hy_conv_b=hy_conv_b,
             hy_f_w1=hy_f_w1, hy_f_b1=hy_f_b1, hy_f_w2=hy_f_w2, hy_f_b2=hy_f_b2, hy_f_w3=hy_f_w3, hy_freq=hy_freq,
             hy_bias=hy_bias, mla_cq_g=mla_cq_g, mla_w_uq=mla_w_uq, mla_ckv_g=mla_ckv_g, mla_w_ukv=mla_w_ukv,
             mla_qnorm_g=mla_qnorm_g, mla_knorm_g=mla_knorm_g, router_w=router_w, exp_w_gate=exp_w_gate,
             exp_w_up=exp_w_up, exp_w_down=exp_w_down)
    B, n, D = x.shape
    nc = ctx.shape[1]
    depth = w_mod.shape[0]
    ch = hy_bias.shape[2]
    rows = -(-(B + 1) // SUBLANES) * SUBLANES
    cc = jnp.concatenate([c, c_ctx[None, :], jnp.zeros((rows - B - 1, D), F32)], axis=0)
    mod = _mod_call(cc, w_mod, b_mod)
    tabs_x = _rope_tables(n, True)
    tabs_c = _rope_tables(nc, False)
    hc_x = _hyena_consts(n, ch)
    hc_c = _hyena_consts(nc, ch)
    tri = _route_consts()
    xc = ctx
    for l in range(depth):
        last = l == depth - 1
        lp = _layer_params(l, w)
        m = mod[l]
        mx = [m[:B, i * D:(i + 1) * D][:, None, :] for i in range(6)]
        mc = [jnp.broadcast_to(m[B:B + 1, i * D:(i + 1) * D][:, None, :], (B, 1, D)) for i in range(6)]
        pc = _inproj_call(xc, mc[0], mc[1], lp, tabs_c, last)
        kc, vc, mkc, mvc = pc[:4]
        kx, vx, mkx, mvx, qx, mqx, poolx, hyx = _inproj_call(x, mx[0], mx[1], lp, tabs_x, False)
        hm = lambda a, h: jnp.swapaxes(a.reshape(a.shape[0], a.shape[1], h, a.shape[2] // h), 1, 2)
        kch, mkch = hm(kc, GQA_KV_HEADS), hm(mkc, MLA_HEADS)
        vtc, mvtc = _vt_ext(vc, GQA_KV_HEADS), _vt_ext(mvc, MLA_HEADS)
        o_gqa = _attn_call(hm(qx, GQA_HEADS), [hm(kx, GQA_KV_HEADS), kch], [_vt_ext(vx, GQA_KV_HEADS), vtc],
                           GQA_HEADS, GQA_KV_HEADS, GQA_HD, "attn_gqa")
        o_mla = _attn_call(hm(mqx, MLA_HEADS), [hm(mkx, MLA_HEADS), mkch], [_vt_ext(mvx, MLA_HEADS), mvtc],
                           MLA_HEADS, MLA_HEADS, MLA_V, "attn_mla")
        o_pool = _pool_call(poolx, lp)
        o_hy = _hyena_call(hyx, lp, hc_x, _hyfilt_call(lp, hc_x))
        x_new = _outproj_call(x, mx[2], [o_pool, o_gqa, o_hy, o_mla], lp)
        x_new = _moe(x_new, mx[3], mx[4], mx[5], lp, tri)
        if not last:
            qc, mqc, poolc, hyc = pc[4:]
            oc_gqa = _attn_call(hm(qc, GQA_HEADS), [kch], [vtc], GQA_HEADS, GQA_KV_HEADS, GQA_HD, "attn_gqa_ctx")
            oc_mla = _attn_call(hm(mqc, MLA_HEADS), [mkch], [mvtc], MLA_HEADS, MLA_HEADS, MLA_V, "attn_mla_ctx")
            oc_pool = _pool_call(poolc, lp)
            oc_hy = _hyena_call(hyc, lp, hc_c, _hyfilt_call(lp, hc_c))
            xc_new = _outproj_call(xc, mc[2], [oc_pool, oc_gqa, oc_hy, oc_mla], lp)
            xc = _moe(xc_new, mc[3], mc[4], mc[5], lp, tri)
        x = x_new
    return x
```

```python
import functools
import math

import numpy as np
import jax
import jax.numpy as jnp
from jax import lax
from jax.experimental import pallas as pl
from jax.experimental.pallas import tpu as pltpu

F32 = jnp.float32
BF16 = jnp.bfloat16
I32 = jnp.int32

GRID_W = 64
ROPE_THETA = 10000.0
EPS = 1e-6
POOL_WINDOWS = (2, 4, 8, 16)
GQA_HEADS = 4
GQA_KV_HEADS = 2
GQA_HD = 64
HY_BANDS = 8
HY_DECAY_TARGET = 1e-2
HY_FAST_PCT = 0.3
HY_SLOW_PCT = 1.5
MLA_HEADS = 4
MLA_NOPE = 64
MLA_ROPE = 32
MLA_V = 64
MLA_HD = MLA_NOPE + MLA_ROPE
N_EXPERTS = 16
EC_CAPACITY = 2

LANES = 128
SUBLANES = 8
MLA_PAD = LANES
VMEM_LIMIT = 56 * 2 ** 20
HIGHEST = lax.Precision.HIGHEST


def _cp(sem, vmem=VMEM_LIMIT):
    return pltpu.CompilerParams(dimension_semantics=sem, vmem_limit_bytes=vmem)


def _dot(a, b):
    return jnp.dot(a, b, preferred_element_type=F32)


def _dot_nt(a, b):
    return lax.dot_general(a, b, (((1,), (1,)), ((), ())), preferred_element_type=F32)


def _dot_split(x, m):
    hi = x.astype(BF16)
    lo = (x - hi.astype(F32)).astype(BF16)
    return _dot(hi, m) + _dot(lo, m)


def _rms_rows(x):
    return x * lax.rsqrt(jnp.mean(x * x, axis=-1, keepdims=True) + EPS)


def _rope(y, cos, sin_s, perm):
    pw = perm.shape[0]
    yb = y.astype(BF16)
    partner = jnp.concatenate([_dot(yb[:, j:j + pw], perm) for j in range(0, y.shape[1], pw)], axis=1)
    return y * cos + partner * sin_s


def _mod_kernel(c_ref, w_ref, b_ref, o_ref):
    c = c_ref[...]
    s = c * (1.0 / (1.0 + jnp.exp(-c)))
    o_ref[0] = jnp.dot(s, w_ref[0], precision=HIGHEST, preferred_element_type=F32) + b_ref[0]


def _mod_call(cc, w_mod, b_mod):
    L, D, N = w_mod.shape
    R = cc.shape[0]
    tn = 1536
    return pl.pallas_call(
        _mod_kernel,
        grid=(L, N // tn),
        in_specs=[
            pl.BlockSpec((R, D), lambda l, j: (0, 0)),
            pl.BlockSpec((1, D, tn), lambda l, j: (l, 0, j)),
            pl.BlockSpec((1, 1, tn), lambda l, j: (l, 0, j)),
        ],
        out_specs=pl.BlockSpec((1, R, tn), lambda l, j: (l, 0, j)),
        out_shape=jax.ShapeDtypeStruct((L, R, N), F32),
        compiler_params=_cp(("arbitrary", "arbitrary")),
        name="adaln_mod",
    )(cc, w_mod, b_mod.reshape(L, 1, N))


KV_COLS = 512
ALL_COLS = 2048


def _inproj_kernel(kv_only, x_ref, sh_ref, sc_ref, g1_ref, w_ref,
                   gk_ref, gq_ref, gckv_ref, wukv_ref, gmk_ref, gcq_ref, wuq_ref, gmq_ref,
                   seg_ref, pg_ref, pm_ref, cg_ref, sg_ref, cm_ref, sm_ref, *outs):
    if kv_only:
        k_ref, v_ref, mk_ref, mv_ref = outs
    else:
        k_ref, v_ref, mk_ref, mv_ref, q_ref, mq_ref, pool_ref, hy_ref = outs
    x = x_ref[0]
    h = _rms_rows(x) * g1_ref[...] * (1.0 + sc_ref[0]) + sh_ref[0]
    p = _dot(h.astype(BF16), w_ref[...])

    def seg_norm(y, seg):
        ss = _dot((y * y).astype(BF16), seg)
        return y * lax.rsqrt(ss * (1.0 / GQA_HD) + EPS)

    def head_norm(y):
        parts = []
        for hh in range(MLA_HEADS):
            s = y[:, hh * MLA_PAD:(hh + 1) * MLA_PAD]
            ms = jnp.sum(s * s, axis=-1, keepdims=True) * (1.0 / MLA_HD)
            parts.append(s * lax.rsqrt(ms + EPS))
        return jnp.concatenate(parts, axis=1)

    def put_heads(ref, y, width):
        for hh in range(ref.shape[1]):
            ref[0, hh] = y[:, hh * width:(hh + 1) * width].astype(ref.dtype)

    def put_values_t(ref, v, dv):
        vt = v.T
        dve = dv + ATTN_ONES_ROWS
        for hh in range(v.shape[1] // dv):
            ref[0, hh * dve:hh * dve + dv, :] = vt[hh * dv:(hh + 1) * dv, :].astype(ref.dtype)
            ref[0, hh * dve + dv:(hh + 1) * dve, :] = jnp.ones((ATTN_ONES_ROWS, v.shape[0]), ref.dtype)

    cm, sm, pm = cm_ref[...], sm_ref[...], pm_ref[...]
    kk = seg_norm(p[:, 0:128], seg_ref[0:128, 0:128]) * gk_ref[...]
    kk = _rope(kk, cg_ref[:, 0:128], sg_ref[:, 0:128], pg_ref[0:128, 0:128])
    put_heads(k_ref, kk, GQA_HD)
    put_values_t(v_ref, p[:, 128:256], GQA_HD)
    ckv = _rms_rows(p[:, 256:384]) * gckv_ref[...]
    kv = _dot(ckv.astype(BF16), wukv_ref[...])
    kr = p[:, 384:512]
    kraw = kv[:, 0:MLA_HEADS * MLA_PAD] + jnp.concatenate([kr] * MLA_HEADS, axis=1)
    mk = _rope(head_norm(kraw) * gmk_ref[...], cm, sm, pm)
    put_heads(mk_ref, mk, MLA_PAD)
    put_values_t(mv_ref, kv[:, MLA_HEADS * MLA_PAD:], MLA_V)
    if kv_only:
        return
    qq = seg_norm(p[:, 512:768], seg_ref[...]) * gq_ref[...]
    qq = _rope(qq, cg_ref[...], sg_ref[...], pg_ref[...])
    put_heads(q_ref, qq * (GQA_HD ** -0.5 * LOG2E), GQA_HD)
    cq = _rms_rows(p[:, 768:1024]) * gcq_ref[...]
    mq = _rope(head_norm(_dot(cq.astype(BF16), wuq_ref[...])) * gmq_ref[...], cm, sm, pm)
    put_heads(mq_ref, mq * (MLA_HD ** -0.5 * LOG2E), MLA_PAD)
    pool_ref[0] = p[:, 1024:1280]
    hy_ref[0] = p[:, 1280:2048]


def _inproj_call(xs, sh, sc, lp, tabs, kv_only):
    B, n, D = xs.shape
    tm = min(n, 512)
    cols = KV_COLS if kv_only else ALL_COLS
    w = lp["w_in_r"][:, :cols]
    full = lambda a: pl.BlockSpec(a.shape, lambda i, b: (0,) * a.ndim)
    tab = lambda a: pl.BlockSpec((tm, a.shape[1]), lambda i, b: (i, 0))
    row = lambda width: pl.BlockSpec((1, tm, width), lambda i, b: (b, i, 0))
    params = [lp["g_k"], lp["g_q"], lp["g_ckv"], lp["w_ukv_r"], lp["g_mk"], lp["g_cq"], lp["w_uq_r"], lp["g_mq"],
              lp["seg"], lp["perm_g"], lp["perm_m"]]
    ones = ATTN_ONES_ROWS
    heads = lambda h, d: (jax.ShapeDtypeStruct((B, h, n, d), BF16), pl.BlockSpec((1, h, tm, d), lambda i, b: (b, 0, i, 0)))
    vals_t = lambda h, d: (jax.ShapeDtypeStruct((B, h * (d + ones), n), BF16),
                           pl.BlockSpec((1, h * (d + ones), tm), lambda i, b: (b, 0, i)))
    rows = lambda d: (jax.ShapeDtypeStruct((B, n, d), F32), row(d))
    outs = [heads(GQA_KV_HEADS, GQA_HD), vals_t(GQA_KV_HEADS, GQA_HD), heads(MLA_HEADS, MLA_PAD), vals_t(MLA_HEADS, MLA_V)]
    if not kv_only:
        outs += [heads(GQA_HEADS, GQA_HD), heads(MLA_HEADS, MLA_PAD), rows(256), rows(768)]
    return pl.pallas_call(
        functools.partial(_inproj_kernel, kv_only),
        grid=(n // tm, B),
        in_specs=[row(D),
                  pl.BlockSpec((1, 1, D), lambda i, b: (b, 0, 0)),
                  pl.BlockSpec((1, 1, D), lambda i, b: (b, 0, 0)),
                  full(lp["norm1_g"]), full(w)] + [full(a) for a in params] + [tab(a) for a in tabs],
        out_specs=[spec for _, spec in outs],
        out_shape=[shape for shape, _ in outs],
        compiler_params=_cp(("arbitrary", "arbitrary")),
        name="inproj_kv" if kv_only else "inproj",
    )(xs, sh, sc, lp["norm1_g"], w, *params, *tabs)


ATTN_KEY_BLOCK = 256
ATTN_Q_TILE = 512
ATTN_ONES_ROWS = 16
LOG2E = math.log2(math.e)


def _attn_kernel(heads, kv_heads, dv, nseg, q_ref, *refs):
    k_refs, vt_refs, o_ref = refs[:nseg], refs[nseg:2 * nseg], refs[2 * nseg]
    grp = heads // kv_heads
    kb = ATTN_KEY_BLOCK
    dve = dv + ATTN_ONES_ROWS
    blocks = [(i, r) for i in range(nseg) for r in range(0, k_refs[i].shape[2], kb)]

    qs = [q_ref[0, hh] for hh in range(heads)]
    run_max = [None] * heads
    acc = [None] * heads
    s_prev = [None] * heads
    for j in range(len(blocks) + 1):
        s_cur = [None] * heads
        for hh in range(heads):
            g = hh // grp
            if j < len(blocks):
                i, r = blocks[j]
                s_cur[hh] = _dot_nt(k_refs[i][0, g, r:r + kb, :], qs[hh])
            if j >= 1:
                i, r = blocks[j - 1]
                bm = jnp.max(s_prev[hh], axis=0, keepdims=True)
                m_new = bm if run_max[hh] is None else jnp.maximum(run_max[hh], bm)
                t = _dot(vt_refs[i][0, g * dve:(g + 1) * dve, r:r + kb], jnp.exp2(s_prev[hh] - m_new).astype(BF16))
                acc[hh] = t if acc[hh] is None else acc[hh] * jnp.exp2(run_max[hh] - m_new) + t
                run_max[hh] = m_new
        s_prev = s_cur
    o_ref[0] = jnp.concatenate([a[0:dv] / a[dv:dv + 1] for a in acc], axis=0).T.astype(o_ref.dtype)


def _attn_call(q, ks, vts, heads, kv_heads, dv, name):
    B, _, n, dk = q.shape
    tq = min(n, ATTN_Q_TILE)
    kspec = lambda a: pl.BlockSpec((1,) + a.shape[1:], lambda b, i: (b,) + (0,) * (a.ndim - 1))
    return pl.pallas_call(
        functools.partial(_attn_kernel, heads, kv_heads, dv, len(ks)),
        grid=(B, n // tq),
        in_specs=[pl.BlockSpec((1, heads, tq, dk), lambda b, i: (b, 0, i, 0))]
        + [kspec(a) for a in ks] + [kspec(a) for a in vts],
        out_specs=pl.BlockSpec((1, tq, heads * dv), lambda b, i: (b, i, 0)),
        out_shape=jax.ShapeDtypeStruct((B, n, heads * dv), BF16),
        compiler_params=_cp(("arbitrary", "arbitrary")),
        name=name,
    )(q, *ks, *vts)


POOL_HALO = 8


def _pool_kernel(n, tile, u_ref, wbd_ref, scale_ref, o_ref):
    kw = min(n, tile + 2 * POOL_HALO)
    gc = u_ref.shape[2] // len(POOL_WINDOWS)
    for i in range(n // tile):
        t0 = i * tile
        ws = min(max(t0 - POOL_HALO, 0), n - kw)
        uw = u_ref[0, ws:ws + kw, :]
        hi = uw.astype(BF16)
        lo = (uw - hi.astype(F32)).astype(BF16)
        t = t0 + lax.broadcasted_iota(I32, (tile, kw), 0)
        s = ws + lax.broadcasted_iota(I32, (tile, kw), 1)
        trow = t0 + lax.broadcasted_iota(I32, (tile, 1), 0)
        lane = lax.broadcasted_iota(I32, (tile, u_ref.shape[2]), 1)
        uc = u_ref[0, t0:t0 + tile, :]
        mean = jnp.zeros_like(uc)
        for gi, w in enumerate(POOL_WINDOWS):
            left = w // 2
            right = w - 1 - left
            band = jnp.where((s >= t - left) & (s <= t + right), 1.0, 0.0).astype(BF16)
            sm = _dot(band, hi) + _dot(band, lo)
            cnt = (jnp.minimum(trow + right, n - 1) - jnp.maximum(trow - left, 0) + 1).astype(F32)
            mean = jnp.where((lane >= gi * gc) & (lane < (gi + 1) * gc), sm / cnt, mean)
        d = mean - uc
        o_ref[0, t0:t0 + tile, :] = (_dot(d.astype(BF16), wbd_ref[...]) * scale_ref[...]).astype(o_ref.dtype)


def _pool_call(u, lp):
    B, n, W = u.shape
    return pl.pallas_call(
        functools.partial(_pool_kernel, n, min(n, 256)),
        grid=(B,),
        in_specs=[pl.BlockSpec((1, n, W), lambda b: (b, 0, 0)),
                  pl.BlockSpec((W, W), lambda b: (0, 0)),
                  pl.BlockSpec((1, W), lambda b: (0, 0))],
        out_specs=pl.BlockSpec((1, n, W), lambda b: (b, 0, 0)),
        out_shape=jax.ShapeDtypeStruct((B, n, W), BF16),
        compiler_params=_cp(("arbitrary",)),
        name="pool",
    )(u, lp["pool_bd"], lp["pool_scale"])


HY_BLOCK = 512
HY_SPEC_DTYPE = F32


def _dot3(a_hi, a_lo, x):
    x_hi = x.astype(BF16)
    x_lo = (x - x_hi.astype(F32)).astype(BF16)
    return _dot(a_hi, x_hi) + _dot(a_hi, x_lo) + _dot(a_lo, x_hi)


def _hyfilt_kernel(n, t, feats_ref, w1_ref, b1_ref, w2_ref, b2_ref, w3_ref, fr_ref, decay_ref,
                   chi_ref, clo_ref, shi_ref, slo_ref, hre_ref, him_ref, hn_ref, tc_ref, ts_ref, edge_ref):
    di = pl.program_id(0)
    nb = n // t
    hc2 = w3_ref.shape[1] // 2

    @pl.when(di == 0)
    def _():
        fr = fr_ref[...]
        hid = jnp.sin(fr * (jnp.dot(feats_ref[...], w1_ref[...], precision=HIGHEST, preferred_element_type=F32)
                            + b1_ref[...]))
        hid = jnp.sin(fr * (jnp.dot(hid, w2_ref[...], precision=HIGHEST, preferred_element_type=F32) + b2_ref[...]))
        h = jnp.dot(hid, w3_ref[...], precision=HIGHEST, preferred_element_type=F32)
        dec = decay_ref[...]
        dec2 = jnp.concatenate([dec, dec], axis=1)
        row = lax.broadcasted_iota(I32, (n, 1), 0)
        fwd = h[:, 0:hc2] * dec2
        bwd = jnp.where(row == 0, 0.0, h[:, hc2:2 * hc2] * dec2)
        scl = lax.rsqrt(jnp.sum(fwd * fwd + bwd * bwd, axis=0, keepdims=True) + EPS)
        alt = jnp.where(lax.broadcasted_iota(I32, (t, 1), 0) % 2 == 0, 1.0, -1.0)
        for fam, filt in enumerate((fwd * scl, bwd * scl)):
            for j in range(nb):
                x = filt[j * t:(j + 1) * t]
                tc_ref[fam, j] = _dot3(chi_ref[...], clo_ref[...], x)
                ts_ref[fam, j] = _dot3(shi_ref[...], slo_ref[...], x)
                r = 2 * (fam * nb + j)
                edge_ref[r:r + 1, :] = x[0:1, :]
                edge_ref[r + 1:r + 2, :] = jnp.sum(x * alt, axis=0, keepdims=True)

    k = lax.broadcasted_iota(I32, (t, 1), 0)
    wk = jnp.where(k == 0, 1.0 / (2 * t), 2.0 / (2 * t))
    sig = jnp.where(k % 2 == 0, 1.0, -1.0)
    first = lambda fam, j: edge_ref[2 * (fam * nb + j):2 * (fam * nb + j) + 1, :]
    altsum = lambda fam, j: edge_ref[2 * (fam * nb + j) + 1:2 * (fam * nb + j) + 2, :]
    for dd in range(2 * nb - 1):
        d = dd - (nb - 1)

        def put(ref, val):
            for o in range(2):
                ref[o, 0] = val[:, o * (hc2 // 2):(o + 1) * (hc2 // 2)].astype(ref.dtype)

        @pl.when(di == dd)
        def _(d=d):
            if d == 0:
                put(hre_ref, wk * (tc_ref[0, 0] + tc_ref[1, 0]))
                put(him_ref, wk * (ts_ref[1, 0] - ts_ref[0, 0]))
                put(hn_ref, (altsum(0, 0) + altsum(1, 0)) * (1.0 / (2 * t)))
            else:
                fam, e, sgn = (0, d, -1.0) if d > 0 else (1, -d, 1.0)
                put(hre_ref, wk * (sig * (tc_ref[fam, e - 1] - first(fam, e - 1)) + tc_ref[fam, e]))
                put(him_ref, (sgn * wk) * (sig * ts_ref[fam, e - 1] + ts_ref[fam, e]))
                put(hn_ref, (altsum(fam, e - 1) - first(fam, e - 1) + altsum(fam, e)) * (1.0 / (2 * t)))


def _hyfilt_call(lp, hc):
    n, t = hc["n"], hc["t"]
    nb = n // t
    nd = 2 * nb - 1
    hc2 = lp["hy_w3"].shape[1] // 2
    ch = hc2 // 2
    full = lambda a: pl.BlockSpec(a.shape, lambda i: (0,) * a.ndim)
    args = [hc["feats"], lp["hy_w1"], lp["hy_b1"], lp["hy_w2"], lp["hy_b2"], lp["hy_w3"], lp["hy_freq"], hc["decay"],
            hc["c_hi"], hc["c_lo"], hc["s_hi"], hc["s_lo"]]
    return pl.pallas_call(
        functools.partial(_hyfilt_kernel, n, t),
        grid=(nd,),
        in_specs=[full(a) for a in args],
        out_specs=[pl.BlockSpec((2, 1, t, ch), lambda i: (0, i, 0, 0)), pl.BlockSpec((2, 1, t, ch), lambda i: (0, i, 0, 0)),
                   pl.BlockSpec((2, 1, 1, ch), lambda i: (0, i, 0, 0))],
        out_shape=[jax.ShapeDtypeStruct((2, nd, t, ch), HY_SPEC_DTYPE), jax.ShapeDtypeStruct((2, nd, t, ch), HY_SPEC_DTYPE),
                   jax.ShapeDtypeStruct((2, nd, 1, ch), F32)],
        scratch_shapes=[pltpu.VMEM((2, nb, t, hc2), F32), pltpu.VMEM((2, nb, t, hc2), F32),
                        pltpu.VMEM((4 * nb, hc2), F32)],
        compiler_params=_cp(("arbitrary",)),
        name="hyena_filter",
    )(*args)


HY_FREQ_ROWS = 32


def _hyena_kernel(n, t, u_ref, cw_ref, cb_ref, c_ref, s_ref, hre_ref, him_ref, hn_ref, bias_ref, o_ref,
                  uc_ref, z_ref, a_ref, b_ref, p_ref, q_ref):
    ch = o_ref.shape[2]
    nb = n // t
    row = lax.broadcasted_iota(I32, (t, 1), 0)
    alt = jnp.where(row % 2 == 0, 1.0, -1.0)
    sub = lax.broadcasted_iota(I32, (SUBLANES, 1), 0)
    for g in range(3):
        cols = slice(g * ch, (g + 1) * ch)
        for j in range(nb):
            r0 = j * t
            u = u_ref[0, r0:r0 + t, cols]
            prev = jnp.zeros((1, ch), F32) if j == 0 else u_ref[0, r0 - 1:r0, cols]
            nxt = jnp.zeros((1, ch), F32) if j == nb - 1 else u_ref[0, r0 + t:r0 + t + 1, cols]
            up = pltpu.roll(u, 1, axis=0)
            un = pltpu.roll(u, t - 1, axis=0)
            up = jnp.concatenate([jnp.where(sub == 0, prev, up[0:SUBLANES]), up[SUBLANES:]], axis=0)
            un = jnp.concatenate([un[:t - SUBLANES], jnp.where(sub == SUBLANES - 1, nxt, un[t - SUBLANES:])], axis=0)
            uc_ref[g, r0:r0 + t, :] = (up * cw_ref[0:1, cols] + u * cw_ref[1:2, cols] + un * cw_ref[2:3, cols]
                                       + cb_ref[:, cols])

    z_ref[...] = uc_ref[0]

    def long_conv(o, carry):
        nyq = []
        for j in range(nb):
            z = z_ref[j * t:(j + 1) * t, :]
            zb = z.astype(BF16)
            a_ref[j] = _dot(c_ref[...], zb).astype(a_ref.dtype)
            b_ref[j] = _dot(s_ref[...], zb).astype(b_ref.dtype)
            nyq.append(jnp.sum(z * alt, axis=0, keepdims=True))
        for r0 in range(0, t, HY_FREQ_ROWS):
            rows = slice(r0, r0 + HY_FREQ_ROWS)
            ab = [(a_ref[j, rows, :], b_ref[j, rows, :]) for j in range(nb)]
            for i in range(nb):
                p = q = None
                for j, (a, b) in enumerate(ab):
                    d = i - j + nb - 1
                    hre, him = hre_ref[o, d, rows, :], him_ref[o, d, rows, :]
                    tp = a * hre + b * him
                    tq = b * hre - a * him
                    p = tp if p is None else p + tp
                    q = tq if q is None else q + tq
                p_ref[i, rows, :] = p.astype(BF16)
                q_ref[i, rows, :] = q.astype(BF16)
        for i in range(nb):
            yn = functools.reduce(jnp.add, [nyq[j] * hn_ref[o, i - j + nb - 1] for j in range(nb)])
            y = _dot(c_ref[...], p_ref[i]) + _dot(s_ref[...], q_ref[i])
            blk = slice(i * t, (i + 1) * t)
            z_ref[blk, :] = uc_ref[o + 1, blk, :] * (y + alt * yn + z_ref[blk, :] * bias_ref[o])
        return carry

    lax.fori_loop(0, 2, long_conv, 0)
    o_ref[0] = z_ref[...].astype(o_ref.dtype)


def _hyena_call(u, lp, hc, filt):
    B, n, W = u.shape
    ch = W // 3
    t = hc["t"]
    hre, him, hn = filt
    once = lambda a: pl.BlockSpec(a.shape, lambda b: (0,) * a.ndim, pipeline_mode=pl.Buffered(1))
    return pl.pallas_call(
        functools.partial(_hyena_kernel, n, t),
        scratch_shapes=[pltpu.VMEM((3, n, ch), F32), pltpu.VMEM((n, ch), F32),
                        pltpu.VMEM((n // t, t, ch), HY_SPEC_DTYPE), pltpu.VMEM((n // t, t, ch), HY_SPEC_DTYPE),
                        pltpu.VMEM((n // t, t, ch), BF16), pltpu.VMEM((n // t, t, ch), BF16)],
        grid=(B,),
        in_specs=[pl.BlockSpec((1, n, W), lambda b: (b, 0, 0)),
                  once(lp["hy_conv_w"]), once(lp["hy_conv_b"]), once(hc["c16"]), once(hc["s16"]),
                  once(hre), once(him), once(hn), once(lp["hy_bias"])],
        out_specs=pl.BlockSpec((1, n, ch), lambda b: (b, 0, 0)),
        out_shape=jax.ShapeDtypeStruct((B, n, ch), BF16),
        compiler_params=_cp(("arbitrary",)),
        name="hyena",
    )(u, lp["hy_conv_w"], lp["hy_conv_b"], hc["c16"], hc["s16"], hre, him, hn, lp["hy_bias"])


def _outproj_kernel(x_ref, g_ref, a_ref, b_ref, c_ref, d_ref, w_ref, o_ref):
    mixed = (_dot(a_ref[0], w_ref[0]) + _dot(b_ref[0], w_ref[1])
             + _dot(c_ref[0], w_ref[2]) + _dot(d_ref[0], w_ref[3]))
    o_ref[0] = x_ref[0] + g_ref[0] * mixed


def _outproj_call(xs, gate, parts, lp):
    B, n, D = xs.shape
    tm = min(n, 512)
    gw = parts[0].shape[2]
    return pl.pallas_call(
        _outproj_kernel,
        grid=(B, n // tm),
        in_specs=[pl.BlockSpec((1, tm, D), lambda b, i: (b, i, 0)),
                  pl.BlockSpec((1, 1, D), lambda b, i: (b, 0, 0))]
        + [pl.BlockSpec((1, tm, gw), lambda b, i: (b, i, 0))] * 4
        + [pl.BlockSpec(lp["w_out"].shape, lambda b, i: (0, 0, 0))],
        out_specs=pl.BlockSpec((1, tm, D), lambda b, i: (b, i, 0)),
        out_shape=jax.ShapeDtypeStruct((B, n, D), F32),
        compiler_params=_cp(("arbitrary", "arbitrary")),
        name="outproj",
    )(xs, gate, *parts, lp["w_out"])


PREFIX_BLOCK = 256


def _prefix_lanes(m, tri):
    e, n = m.shape
    nb = n // PREFIX_BLOCK
    stk = jnp.concatenate([m[:, j * PREFIX_BLOCK:(j + 1) * PREFIX_BLOCK] for j in range(nb)], axis=0)
    pre = _dot(stk.astype(BF16), tri)
    outs = []
    carry = jnp.zeros((e, 1), F32)
    for j in range(nb):
        pj = pre[j * e:(j + 1) * e]
        outs.append(pj + carry)
        carry = carry + pj[:, PREFIX_BLOCK - 1:PREFIX_BLOCK]
    return jnp.concatenate(outs, axis=1)


SLOT_RADIX = 16


def _split3(x):
    a = x.astype(BF16).astype(F32)
    r = x - a
    b = r.astype(BF16).astype(F32)
    c = (r - b).astype(BF16).astype(F32)
    return a, b, c


def _digit_onehots(v, weights=None):
    e_n = v.shape[0]
    hi_d = jnp.floor(v * (1.0 / SLOT_RADIX))
    lo_d = v - SLOT_RADIX * hi_d
    dig = lax.broadcasted_iota(I32, (SLOT_RADIX, v.shape[1]), 0).astype(F32)
    his = [hi_d[e:e + 1, :] == dig for e in range(e_n)]
    lo = jnp.concatenate([jnp.where(lo_d[e:e + 1, :] == dig, 1.0, 0.0) for e in range(e_n)], axis=0)
    if weights is None:
        hi = jnp.concatenate([jnp.where(m, 1.0, 0.0) for m in his], axis=0)
    else:
        hi = jnp.concatenate([jnp.where(m, w[e:e + 1, :], 0.0) for w in weights for e, m in enumerate(his)], axis=0)
    return hi.astype(BF16), lo.astype(BF16)


def _route_kernel(n, cap, x_ref, sh_ref, sc_ref, g_ref, wr_ref, tri_ref, fold_ref, pre_ref, low_ref,
                  hcm_ref, idx_ref, gate_ref):
    x = x_ref[0]
    h = _rms_rows(x) * g_ref[...] * (1.0 + sc_ref[0]) + sh_ref[0]
    for c in range(h.shape[1] // LANES):
        hcm_ref[0, pl.ds(c, n, stride=SUBLANES), :] = h[:, c * LANES:(c + 1) * LANES]
    h_hi = h.astype(BF16)
    h_lo = (h - h_hi.astype(F32)).astype(BF16)
    rt = _dot(jnp.concatenate([h_hi, h_lo], axis=1), wr_ref[...]).T
    lt = rt[0:N_EXPERTS, :] + rt[N_EXPERTS:2 * N_EXPERTS, :]
    ex = jnp.exp(lt - jnp.max(lt, axis=0, keepdims=True))
    aff = ex / jnp.sum(ex, axis=0, keepdims=True)
    bits = pltpu.bitcast(aff, I32)
    thr = jnp.zeros((N_EXPERTS, 1), I32)
    for bit in range(30, -1, -1):
        cand = thr | (1 << bit)
        cnt = jnp.sum(jnp.where(bits >= cand, 1.0, 0.0), axis=1, keepdims=True)
        thr = jnp.where(cnt >= cap, cand, thr)
    gt = jnp.where(bits > thr, 1.0, 0.0)
    eq = jnp.where(bits == thr, 1.0, 0.0)
    need = cap - jnp.sum(gt, axis=1, keepdims=True)
    tri = tri_ref[...]
    sel = gt + eq * jnp.where(_prefix_lanes(eq, tri) <= need, 1.0, 0.0)
    incl = _prefix_lanes(sel, tri)
    rows = N_EXPERTS * SLOT_RADIX
    shift = SLOT_RADIX.bit_length() - 1
    same = ((lax.broadcasted_iota(I32, (rows, rows), 0) >> shift)
            == (lax.broadcasted_iota(I32, (rows, rows), 1) >> shift))
    fold = fold_ref[...]
    a_hi, a_lo = _digit_onehots(incl)
    hist = jnp.where(same, _dot_nt(a_hi, a_lo), 0.0)
    hist = _dot_split(hist, fold)
    pre = _dot_split(hist, pre_ref[...])
    tot = jnp.broadcast_to(pre[:, SLOT_RADIX - 1:SLOT_RADIX], pre.shape)
    t_hi = tot.astype(BF16)
    t_lo = (tot - t_hi.astype(F32)).astype(BF16)
    idx_ref[0] = (pre + _dot(low_ref[...], t_hi) + _dot(low_ref[...], t_lo)).astype(I32)
    g_hi, g_lo = _digit_onehots(incl - 1.0, _split3(sel * aff))
    gsum = _dot_nt(g_hi, g_lo)
    gsum = jnp.where(same, gsum[0:rows] + gsum[rows:2 * rows] + gsum[2 * rows:3 * rows], 0.0)
    gate_ref[0] = functools.reduce(jnp.add, [_dot(part.astype(BF16), fold) for part in _split3(gsum)])


def _route_consts():
    rows = N_EXPERTS * SLOT_RADIX
    i = np.arange(rows)
    j = np.arange(LANES)
    fold = (i[:, None] % SLOT_RADIX == j[None, :]).astype(np.float32)
    pre = ((j[:, None] <= j[None, :]) & (j[None, :] < SLOT_RADIX)).astype(np.float32)
    low = ((i[:, None] // SLOT_RADIX == i[None, :] // SLOT_RADIX) & (i[None, :] < i[:, None])).astype(np.float32)
    tri = np.triu(np.ones((PREFIX_BLOCK, PREFIX_BLOCK), np.float32))
    return tuple(jnp.asarray(a, dtype=BF16) for a in (tri, fold, pre, low))


def _route_call(xs, sh, sc, lp, consts):
    B, n, D = xs.shape
    cap = EC_CAPACITY * n // N_EXPERTS
    assert cap <= SLOT_RADIX * SLOT_RADIX
    rows = n * (D // LANES)
    srows = N_EXPERTS * SLOT_RADIX
    full = lambda a: pl.BlockSpec(a.shape, lambda b: (0,) * a.ndim)
    return pl.pallas_call(
        functools.partial(_route_kernel, n, cap),
        grid=(B,),
        in_specs=[pl.BlockSpec((1, n, D), lambda b: (b, 0, 0)),
                  pl.BlockSpec((1, 1, D), lambda b: (b, 0, 0)),
                  pl.BlockSpec((1, 1, D), lambda b: (b, 0, 0)),
                  full(lp["norm2_g"]), full(lp["router_split"])] + [full(a) for a in consts],
        out_specs=[pl.BlockSpec((1, rows, LANES), lambda b: (b, 0, 0)),
                   pl.BlockSpec((1, srows, LANES), lambda b: (b, 0, 0)),
                   pl.BlockSpec((1, srows, LANES), lambda b: (b, 0, 0))],
        out_shape=[jax.ShapeDtypeStruct((B, rows, LANES), F32),
                   jax.ShapeDtypeStruct((B, srows, LANES), I32),
                   jax.ShapeDtypeStruct((B, srows, LANES), F32)],
        compiler_params=_cp(("arbitrary",)),
        name="route",
    )(xs, sh, sc, lp["norm2_g"], lp["router_split"], *consts)


GATHER_UNROLL = 8


FFN_ROWS = 256


def _ffn_kernel(cap, bt, idx_ref, h_ref, gate_ref, wg_ref, wu_ref, wd_ref, o_ref, xga_ref, xgb_ref, ya_ref, yb_ref):
    b0 = pl.program_id(0) * bt
    k = pl.program_id(1)
    steps = pl.num_programs(1)
    chunks = wg_ref.shape[1] // LANES
    m = bt * cap
    tile = lambda r: pl.ds(pl.multiple_of(r * SUBLANES, SUBLANES), SUBLANES)
    e0 = 2 * k
    e1 = e0 + 1
    wrap = lambda e: jnp.where(e < 0, e + N_EXPERTS, jnp.where(e >= N_EXPERTS, e - N_EXPERTS, e))

    def gather(e, xg_ref):
        for bi in range(bt):
            base = ((b0 + bi) * N_EXPERTS + e) * cap
            for r in range(cap):
                xg_ref[tile(bi * cap + r), :] = h_ref[bi, tile(idx_ref[base + r]), :]

    def scatter(e, y_ref):
        for bi in range(bt):
            base = ((b0 + bi) * N_EXPERTS + e) * cap
            for r0 in range(0, cap, GATHER_UNROLL):
                upd = []
                for r in range(r0, r0 + GATHER_UNROLL):
                    dst = tile(idx_ref[base + r])
                    upd.append((dst, o_ref[bi, dst, :] + y_ref[tile(bi * cap + r), :]))
                for dst, v in upd:
                    o_ref[bi, dst, :] = v

    def ffn(e, w, xg_ref, y_ref):
        xt = jnp.concatenate([xg_ref[pl.ds(c, m, stride=SUBLANES), :] for c in range(chunks)], axis=1).astype(BF16)
        a = _dot(xt, wg_ref[w])
        u = _dot(xt, wu_ref[w])
        hmid = (a * (1.0 / (1.0 + jnp.exp(-a))) * u).astype(BF16)
        lane = lax.broadcasted_iota(I32, (m, LANES), 1)
        gcol = jnp.sum(jnp.where(lane == e, gate_ref[...].reshape(m, LANES), 0.0), axis=1, keepdims=True)
        y = _dot(hmid, wd_ref[w]) * gcol
        for c in range(chunks):
            y_ref[pl.ds(c, m, stride=SUBLANES), :] = y[:, c * LANES:(c + 1) * LANES]

    @pl.when(k == 0)
    def _():
        o_ref[...] = jnp.zeros_like(o_ref)
        yb_ref[...] = jnp.zeros_like(yb_ref)
        gather(e0, xga_ref)

    ffn(e0, 0, xga_ref, ya_ref)
    gather(e1, xgb_ref)
    scatter(wrap(e0 - 1), yb_ref)
    ffn(e1, 1, xgb_ref, yb_ref)
    gather(wrap(e1 + 1), xga_ref)
    scatter(e0, ya_ref)

    @pl.when(k == steps - 1)
    def _():
        scatter(e1, yb_ref)


def _ffn_call(idx_flat, hcm, gate, lp, cap):
    B, rows, _ = hcm.shape
    D = lp["w_gate"].shape[1]
    ff = lp["w_gate"].shape[2]
    bt = min(B, max(1, FFN_ROWS // cap))
    m = bt * cap
    return pl.pallas_call(
        functools.partial(_ffn_kernel, cap, bt),
        grid=(B // bt, N_EXPERTS // 2),
        in_specs=[pl.BlockSpec(memory_space=pltpu.SMEM),
                  pl.BlockSpec((bt, rows, LANES), lambda b, k: (b, 0, 0)),
                  pl.BlockSpec((bt, cap, LANES), lambda b, k: (b, 0, 0)),
                  pl.BlockSpec((2, D, ff), lambda b, k: (k, 0, 0)),
                  pl.BlockSpec((2, D, ff), lambda b, k: (k, 0, 0)),
                  pl.BlockSpec((2, ff, D), lambda b, k: (k, 0, 0))],
        out_specs=pl.BlockSpec((bt, rows, LANES), lambda b, k: (b, 0, 0)),
        out_shape=jax.ShapeDtypeStruct((B, rows, LANES), F32),
        scratch_shapes=[pltpu.VMEM((m * SUBLANES, LANES), F32)] * 4,
        compiler_params=_cp(("arbitrary", "arbitrary")),
        name="expert_ffn",
    )(idx_flat, hcm, gate, lp["w_gate"], lp["w_up"], lp["w_down"])


def _moe_residual_kernel(x_ref, g_ref, m_ref, o_ref):
    tm = x_ref.shape[1]
    chunks = x_ref.shape[2] // LANES
    moe = jnp.concatenate([m_ref[0, pl.ds(c, tm, stride=SUBLANES), :] for c in range(chunks)], axis=1)
    o_ref[0] = x_ref[0] + g_ref[0] * moe


def _moe_residual_call(xs, gate, moe_cm):
    B, n, D = xs.shape
    tm = min(n, 512)
    return pl.pallas_call(
        _moe_residual_kernel,
        grid=(B, n // tm),
        in_specs=[pl.BlockSpec((1, tm, D), lambda b, i: (b, i, 0)),
                  pl.BlockSpec((1, 1, D), lambda b, i: (b, 0, 0)),
                  pl.BlockSpec((1, tm * (D // LANES), LANES), lambda b, i: (b, i, 0))],
        out_specs=pl.BlockSpec((1, tm, D), lambda b, i: (b, i, 0)),
        out_shape=jax.ShapeDtypeStruct((B, n, D), F32),
        compiler_params=_cp(("arbitrary", "arbitrary")),
        name="moe_residual",
    )(xs, gate, moe_cm)


def _axial_tables(n, head_pat, half, reps, roped):
    t = jnp.arange(n)
    pos = (jnp.floor_divide(t, GRID_W).astype(F32), jnp.remainder(t, GRID_W).astype(F32))
    inv = ROPE_THETA ** (-jnp.arange(half, dtype=F32) / half)
    cols_c, cols_s = [], []
    one = jnp.ones((n,), F32)
    zero = jnp.zeros((n,), F32)
    for pat in head_pat:
        if pat is None or not roped:
            cols_c.append(one), cols_s.append(zero)
            continue
        axis, f, first = pat
        ang = pos[axis] * inv[f]
        cols_c.append(jnp.cos(ang))
        cols_s.append(-jnp.sin(ang) if first else jnp.sin(ang))
    mk = lambda cols: jnp.tile(jnp.stack(cols, axis=1), (1, reps))
    return mk(cols_c), mk(cols_s)


def _partner_matrix(head_pat, half, reps):
    hw = len(head_pat)
    p = np.zeros((hw * reps, hw * reps), np.float32)
    for r in range(reps):
        for j, pat in enumerate(head_pat):
            if pat is not None:
                p[r * hw + (j + half if pat[2] else j - half), r * hw + j] = 1.0
    return jnp.asarray(p, dtype=BF16)


def _rope_patterns():
    gh = GQA_HD // 4
    gpat = [(j // (2 * gh), j % gh, (j % (2 * gh)) < gh) for j in range(GQA_HD)]
    mh = MLA_ROPE // 4
    mpat = [None] * MLA_NOPE + [(j // (2 * mh), j % mh, (j % (2 * mh)) < mh) for j in range(MLA_ROPE)]
    mpat += [None] * (MLA_PAD - MLA_HD)
    return gpat, gh, mpat, mh


def _rope_tables(n, roped):
    gpat, gh, mpat, mh = _rope_patterns()
    return _axial_tables(n, gpat, gh, GQA_HEADS, roped) + _axial_tables(n, mpat, mh, MLA_HEADS, roped)


def _rope_partners():
    gpat, gh, mpat, mh = _rope_patterns()
    return _partner_matrix(gpat, gh, GQA_HEADS), _partner_matrix(mpat, mh, 2)


def _hyena_consts(n, ch):
    f32 = F32
    tb = min(n, HY_BLOCK)
    k = jnp.arange(tb, dtype=I32)
    ang = jnp.remainder(k[:, None] * k[None, :], 2 * tb).astype(f32) * (math.pi / tb)
    c32, s32 = jnp.cos(ang), jnp.sin(ang)
    c_hi, s_hi = c32.astype(BF16), s32.astype(BF16)
    c_lo, s_lo = (c32 - c_hi.astype(f32)).astype(BF16), (s32 - s_hi.astype(f32)).astype(BF16)
    t = jnp.linspace(0.0, 1.0, n, dtype=f32)[:, None]
    lag = jnp.arange(n, dtype=f32)[:, None]
    bands = jnp.linspace(1e-4, HY_BANDS - 1, HY_BANDS, dtype=f32)[None, :]
    a = (2.0 * math.pi / n) * lag * bands
    feats = jnp.concatenate([t, jnp.cos(a), -jnp.sin(a)], axis=-1)
    feats = jnp.pad(feats, ((0, 0), (0, LANES - feats.shape[1])))
    deltas = jnp.abs(jnp.linspace(math.log(HY_DECAY_TARGET) / HY_SLOW_PCT,
                                  math.log(HY_DECAY_TARGET) / HY_FAST_PCT, ch, dtype=f32))
    decay = jnp.exp(-t * deltas[None, :])
    return {"n": n, "t": tb, "c16": c_hi, "s16": s_hi, "c_hi": c_hi, "c_lo": c_lo, "s_hi": s_hi, "s_lo": s_lo,
            "feats": feats, "decay": decay}


def _seg_matrix(width, seg):
    i = np.arange(width)
    return jnp.asarray((i[:, None] // seg == i[None, :] // seg).astype(np.float32), dtype=BF16)


def _router_split(rw):
    d = rw.shape[0]
    hi = rw.astype(BF16)
    lo = (rw - hi.astype(F32)).astype(BF16)
    top = jnp.concatenate([hi, lo, jnp.zeros((d, LANES - 2 * N_EXPERTS), BF16)], axis=1)
    bot = jnp.concatenate([hi, jnp.zeros((d, LANES - N_EXPERTS), BF16)], axis=1)
    return jnp.concatenate([top, bot], axis=0)


def _layer_params(l, w):
    D = w["w_in"].shape[1]
    wi = w["w_in"][l]
    z = lambda c: jnp.zeros((D, c), F32)
    w_in_r = jnp.concatenate(
        [wi[:, 0:384], z(MLA_NOPE), wi[:, 384:416], z(MLA_PAD - MLA_HD), wi[:, 416:]], axis=1).astype(BF16)
    ukv = w["mla_w_ukv"][l].reshape(-1, MLA_HEADS, MLA_NOPE + MLA_V)
    kpart = jnp.pad(ukv[:, :, :MLA_NOPE], ((0, 0), (0, 0), (0, MLA_PAD - MLA_NOPE)))
    w_ukv_r = jnp.concatenate([kpart.reshape(ukv.shape[0], -1), ukv[:, :, MLA_NOPE:].reshape(ukv.shape[0], -1)],
                              axis=1).astype(BF16)
    uq = w["mla_w_uq"][l].reshape(-1, MLA_HEADS, MLA_HD)
    w_uq_r = jnp.pad(uq, ((0, 0), (0, 0), (0, MLA_PAD - MLA_HD))).reshape(uq.shape[0], -1).astype(BF16)
    padg = lambda g: jnp.tile(jnp.pad(g, (0, MLA_PAD - MLA_HD)), MLA_HEADS)[None, :]
    pw = w["pool_w"][l]
    gc = pw.shape[1]
    pool_bd = jnp.zeros((len(POOL_WINDOWS) * gc,) * 2, F32)
    for gi in range(len(POOL_WINDOWS)):
        pool_bd = pool_bd.at[gi * gc:(gi + 1) * gc, gi * gc:(gi + 1) * gc].set(pw[gi])
    nf = w["hy_f_w1"].shape[1]
    perm_g, perm_m = _rope_partners()
    return {
        "perm_g": perm_g, "perm_m": perm_m,
        "norm1_g": w["norm1_g"][l][None, :], "norm2_g": w["norm2_g"][l][None, :],
        "w_in_r": w_in_r, "w_ukv_r": w_ukv_r, "w_uq_r": w_uq_r,
        "g_k": jnp.tile(w["gqa_knorm_g"][l], GQA_KV_HEADS)[None, :],
        "g_q": jnp.tile(w["gqa_qnorm_g"][l], GQA_HEADS)[None, :],
        "g_ckv": w["mla_ckv_g"][l][None, :], "g_cq": w["mla_cq_g"][l][None, :],
        "g_mk": padg(w["mla_knorm_g"][l]), "g_mq": padg(w["mla_qnorm_g"][l]),
        "seg": _seg_matrix(GQA_HEADS * GQA_HD, GQA_HD),
        "pool_bd": pool_bd.astype(BF16), "pool_scale": w["pool_scale"][l][None, :],
        "hy_conv_w": w["hy_conv_w"][l], "hy_conv_b": w["hy_conv_b"][l][None, :],
        "hy_w1": jnp.pad(w["hy_f_w1"][l], ((0, LANES - nf), (0, 0))), "hy_b1": w["hy_f_b1"][l][None, :],
        "hy_w2": w["hy_f_w2"][l], "hy_b2": w["hy_f_b2"][l][None, :], "hy_w3": w["hy_f_w3"][l],
        "hy_freq": w["hy_freq"][l][None, :], "hy_bias": w["hy_bias"][l][:, None, :],
        "w_out": w["w_out"][l].reshape(4, -1, D).astype(BF16),
        "router_split": _router_split(w["router_w"][l]),
        "w_gate": w["exp_w_gate"][l].astype(BF16), "w_up": w["exp_w_up"][l].astype(BF16),
        "w_down": w["exp_w_down"][l].astype(BF16),
    }


def _moe(xs, sh, sc, gate, lp, consts):
    B, n, _ = xs.shape
    cap = EC_CAPACITY * n // N_EXPERTS
    hcm, idx_t, gate_t = _route_call(xs, sh, sc, lp, consts)
    unfold = lambda a: a[:, :, :SLOT_RADIX].reshape(B, N_EXPERTS, SLOT_RADIX * SLOT_RADIX)[:, :, :cap]
    idx_flat = unfold(idx_t).reshape(-1)
    gts = jnp.pad(jnp.swapaxes(unfold(gate_t), 1, 2), ((0, 0), (0, 0), (0, LANES - N_EXPERTS)))
    moe_cm = _ffn_call(idx_flat, hcm, gts, lp, cap)
    return _moe_residual_call(xs, gate, moe_cm)


def kernel(x, c, ctx, c_ctx, norm1_g, norm2_g, w_mod, b_mod, w_in, w_out, pool_w, pool_scale, gqa_qnorm_g, gqa_knorm_g, hy_conv_w, hy_conv_b, hy_f_w1, hy_f_b1, hy_f_w2, hy_f_b2, hy_f_w3, hy_freq, hy_bias, mla_cq_g, mla_w_uq, mla_ckv_g, mla_w_ukv, mla_qnorm_g, mla_knorm_g, router_w, exp_w_gate, exp_w_up, exp_w_down):
    w = dict(norm1_g=norm1_g, norm2_g=norm2_g, w_in=w_in, w_out=w_out, pool_w=pool_w, pool_scale=pool_scale,
             gqa_qnorm_g=gqa_qnorm_g, gqa_knorm_g=gqa_knorm_g, hy_conv_w=hy_conv_w, hy_conv_b=hy_conv_b,
             hy_f_w1=hy_f_w1, hy_f_b1=hy_f_b1, hy_f_w2=hy_f_w2, hy_f_b2=hy_f_b2, hy_f_w3=hy_f_w3, hy_freq=hy_freq,
             hy_bias=hy_bias, mla_cq_g=mla_cq_g, mla_w_uq=mla_w_uq, mla_ckv_g=mla_ckv_g, mla_w_ukv=mla_w_ukv,
             mla_qnorm_g=mla_qnorm_g, mla_knorm_g=mla_knorm_g, router_w=router_w, exp_w_gate=exp_w_gate,
             exp_w_up=exp_w_up, exp_w_down=exp_w_down)
    B, n, D = x.shape
    nc = ctx.shape[1]
    depth = w_mod.shape[0]
    ch = hy_bias.shape[2]
    rows = -(-(B + 1) // SUBLANES) * SUBLANES
    cc = jnp.concatenate([c, c_ctx[None, :], jnp.zeros((rows - B - 1, D), F32)], axis=0)
    mod = _mod_call(cc, w_mod, b_mod)
    tabs_x = _rope_tables(n, True)
    tabs_c = _rope_tables(nc, False)
    hc_x = _hyena_consts(n, ch)
    hc_c = _hyena_consts(nc, ch)
    tri = _route_consts()
    xc = ctx
    for l in range(depth):
        last = l == depth - 1
        lp = _layer_params(l, w)
        m = mod[l]
        mx = [m[:B, i * D:(i + 1) * D][:, None, :] for i in range(6)]
        mc = [jnp.broadcast_to(m[B:B + 1, i * D:(i + 1) * D][:, None, :], (B, 1, D)) for i in range(6)]
        pc = _inproj_call(xc, mc[0], mc[1], lp, tabs_c, last)
        kc, vc, mkc, mvc = pc[:4]
        kx, vx, mkx, mvx, qx, mqx, poolx, hyx = _inproj_call(x, mx[0], mx[1], lp, tabs_x, False)
        o_gqa = _attn_call(qx, [kx, kc], [vx, vc], GQA_HEADS, GQA_KV_HEADS, GQA_HD, "attn_gqa")
        o_mla = _attn_call(mqx, [mkx, mkc], [mvx, mvc], MLA_HEADS, MLA_HEADS, MLA_V, "attn_mla")
        o_pool = _pool_call(poolx, lp)
        o_hy = _hyena_call(hyx, lp, hc_x, _hyfilt_call(lp, hc_x))
        x_new = _outproj_call(x, mx[2], [o_pool, o_gqa, o_hy, o_mla], lp)
        x_new = _moe(x_new, mx[3], mx[4], mx[5], lp, tri)
        if not last:
            qc, mqc, poolc, hyc = pc[4:]
            oc_gqa = _attn_call(qc, [kc], [vc], GQA_HEADS, GQA_KV_HEADS, GQA_HD, "attn_gqa_ctx")
            oc_mla = _attn_call(mqc, [mkc], [mvc], MLA_HEADS, MLA_HEADS, MLA_V, "attn_mla_ctx")
            oc_pool = _pool_call(poolc, lp)
            oc_hy = _hyena_call(hyc, lp, hc_c, _hyfilt_call(lp, hc_c))
            xc_new = _outproj_call(xc, mc[2], [oc_pool, oc_gqa, oc_hy, oc_mla], lp)
            xc = _moe(xc_new, mc[3], mc[4], mc[5], lp, tri)
        x = x_new
    return x
```

```python
import functools
import math

import numpy as np
import jax
import jax.numpy as jnp
from jax import lax
from jax.experimental import pallas as pl
from jax.experimental.pallas import tpu as pltpu

F32 = jnp.float32
BF16 = jnp.bfloat16
I32 = jnp.int32

GRID_W = 64
ROPE_THETA = 10000.0
EPS = 1e-6
POOL_WINDOWS = (2, 4, 8, 16)
GQA_HEADS = 4
GQA_KV_HEADS = 2
GQA_HD = 64
HY_BANDS = 8
HY_DECAY_TARGET = 1e-2
HY_FAST_PCT = 0.3
HY_SLOW_PCT = 1.5
MLA_HEADS = 4
MLA_NOPE = 64
MLA_ROPE = 32
MLA_V = 64
MLA_HD = MLA_NOPE + MLA_ROPE
N_EXPERTS = 16
EC_CAPACITY = 2

LANES = 128
SUBLANES = 8
MLA_PAD = LANES
VMEM_LIMIT = 56 * 2 ** 20
HIGHEST = lax.Precision.HIGHEST


def _cp(sem, vmem=VMEM_LIMIT):
    return pltpu.CompilerParams(dimension_semantics=sem, vmem_limit_bytes=vmem)


def _dot(a, b):
    return jnp.dot(a, b, preferred_element_type=F32)


def _dot_nt(a, b):
    return lax.dot_general(a, b, (((1,), (1,)), ((), ())), preferred_element_type=F32)


def _dot_split(x, m):
    hi = x.astype(BF16)
    lo = (x - hi.astype(F32)).astype(BF16)
    return _dot(hi, m) + _dot(lo, m)


def _rms_rows(x):
    return x * lax.rsqrt(jnp.mean(x * x, axis=-1, keepdims=True) + EPS)


def _rope(y, cos, sin_s, perm):
    pw = perm.shape[0]
    yb = y.astype(BF16)
    partner = jnp.concatenate([_dot(yb[:, j:j + pw], perm) for j in range(0, y.shape[1], pw)], axis=1)
    return y * cos + partner * sin_s


def _mod_kernel(c_ref, w_ref, b_ref, o_ref):
    c = c_ref[...]
    s = c * (1.0 / (1.0 + jnp.exp(-c)))
    o_ref[0] = jnp.dot(s, w_ref[0], precision=HIGHEST, preferred_element_type=F32) + b_ref[0]


def _mod_call(cc, w_mod, b_mod):
    L, D, N = w_mod.shape
    R = cc.shape[0]
    tn = 1536
    return pl.pallas_call(
        _mod_kernel,
        grid=(L, N // tn),
        in_specs=[
            pl.BlockSpec((R, D), lambda l, j: (0, 0)),
            pl.BlockSpec((1, D, tn), lambda l, j: (l, 0, j)),
            pl.BlockSpec((1, 1, tn), lambda l, j: (l, 0, j)),
        ],
        out_specs=pl.BlockSpec((1, R, tn), lambda l, j: (l, 0, j)),
        out_shape=jax.ShapeDtypeStruct((L, R, N), F32),
        compiler_params=_cp(("arbitrary", "arbitrary")),
        name="adaln_mod",
    )(cc, w_mod, b_mod.reshape(L, 1, N))


KV_COLS = 512
ALL_COLS = 2048
INPROJ_ROWS = 512
INPROJ_SUB_ROWS = 256


def _inproj_kernel(kv_only, x_ref, sh_ref, sc_ref, g1_ref, w_ref,
                   gk_ref, gq_ref, gckv_ref, wukv_ref, gmk_ref, gcq_ref, wuq_ref, gmq_ref,
                   seg_ref, pg_ref, pm_ref, cg_ref, sg_ref, cm_ref, sm_ref, *outs):
    if kv_only:
        k_ref, v_ref, mk_ref, mv_ref = outs
    else:
        k_ref, v_ref, mk_ref, mv_ref, q_ref, mq_ref, pool_ref, hy_ref = outs
    tm = x_ref.shape[1]
    sub = min(tm, INPROJ_SUB_ROWS)
    projs = []
    for r0 in range(0, tm, sub):
        x = x_ref[0, r0:r0 + sub, :]
        h = _rms_rows(x) * g1_ref[...] * (1.0 + sc_ref[0]) + sh_ref[0]
        projs.append((slice(r0, r0 + sub), _dot(h.astype(BF16), w_ref[...])))
    for rows, p in projs:
        _inproj_heads(kv_only, rows, p, gk_ref, gq_ref, gckv_ref, wukv_ref, gmk_ref, gcq_ref, wuq_ref, gmq_ref,
                      seg_ref, pg_ref, pm_ref, cg_ref, sg_ref, cm_ref, sm_ref, *outs)


def _inproj_heads(kv_only, rows, p, gk_ref, gq_ref, gckv_ref, wukv_ref, gmk_ref, gcq_ref, wuq_ref, gmq_ref,
                  seg_ref, pg_ref, pm_ref, cg_ref, sg_ref, cm_ref, sm_ref, *outs):
    if kv_only:
        k_ref, v_ref, mk_ref, mv_ref = outs
    else:
        k_ref, v_ref, mk_ref, mv_ref, q_ref, mq_ref, pool_ref, hy_ref = outs

    def seg_norm(y, seg):
        ss = _dot((y * y).astype(BF16), seg)
        return y * lax.rsqrt(ss * (1.0 / GQA_HD) + EPS)

    def head_norm(y):
        parts = []
        for hh in range(MLA_HEADS):
            s = y[:, hh * MLA_PAD:(hh + 1) * MLA_PAD]
            ms = jnp.sum(s * s, axis=-1, keepdims=True) * (1.0 / MLA_HD)
            parts.append(s * lax.rsqrt(ms + EPS))
        return jnp.concatenate(parts, axis=1)

    def put_heads(ref, y, width):
        for hh in range(ref.shape[1]):
            ref[0, hh, rows, :] = y[:, hh * width:(hh + 1) * width].astype(ref.dtype)

    def put_values_t(ref, v, dv):
        vt = v.T
        dve = dv + ATTN_ONES_ROWS
        for hh in range(v.shape[1] // dv):
            ref[0, hh * dve:hh * dve + dv, rows] = vt[hh * dv:(hh + 1) * dv, :].astype(ref.dtype)
            ref[0, hh * dve + dv:(hh + 1) * dve, rows] = jnp.ones((ATTN_ONES_ROWS, v.shape[0]), ref.dtype)

    cg, sg = cg_ref[rows, :], sg_ref[rows, :]
    cm, sm, pm = cm_ref[rows, :], sm_ref[rows, :], pm_ref[...]
    kk = seg_norm(p[:, 0:128], seg_ref[0:128, 0:128]) * gk_ref[...]
    kk = _rope(kk, cg[:, 0:128], sg[:, 0:128], pg_ref[0:128, 0:128])
    put_heads(k_ref, kk, GQA_HD)
    put_values_t(v_ref, p[:, 128:256], GQA_HD)
    ckv = _rms_rows(p[:, 256:384]) * gckv_ref[...]
    kv = _dot(ckv.astype(BF16), wukv_ref[...])
    kr = p[:, 384:512]
    kraw = kv[:, 0:MLA_HEADS * MLA_PAD] + jnp.concatenate([kr] * MLA_HEADS, axis=1)
    mk = _rope(head_norm(kraw) * gmk_ref[...], cm, sm, pm)
    put_heads(mk_ref, mk, MLA_PAD)
    put_values_t(mv_ref, kv[:, MLA_HEADS * MLA_PAD:], MLA_V)
    if kv_only:
        return
    qq = seg_norm(p[:, 512:768], seg_ref[...]) * gq_ref[...]
    qq = _rope(qq, cg, sg, pg_ref[...])
    put_heads(q_ref, qq * (GQA_HD ** -0.5 * LOG2E), GQA_HD)
    cq = _rms_rows(p[:, 768:1024]) * gcq_ref[...]
    mq = _rope(head_norm(_dot(cq.astype(BF16), wuq_ref[...])) * gmq_ref[...], cm, sm, pm)
    put_heads(mq_ref, mq * (MLA_HD ** -0.5 * LOG2E), MLA_PAD)
    pool_ref[0, rows, :] = p[:, 1024:1280]
    hy_ref[0, rows, :] = p[:, 1280:2048]


def _inproj_call(xs, sh, sc, lp, tabs, kv_only):
    B, n, D = xs.shape
    tm = min(n, INPROJ_ROWS)
    cols = KV_COLS if kv_only else ALL_COLS
    w = lp["w_in_r"][:, :cols]
    full = lambda a: pl.BlockSpec(a.shape, lambda i, b: (0,) * a.ndim)
    tab = lambda a: pl.BlockSpec((tm, a.shape[1]), lambda i, b: (i, 0))
    row = lambda width: pl.BlockSpec((1, tm, width), lambda i, b: (b, i, 0))
    params = [lp["g_k"], lp["g_q"], lp["g_ckv"], lp["w_ukv_r"], lp["g_mk"], lp["g_cq"], lp["w_uq_r"], lp["g_mq"],
              lp["seg"], lp["perm_g"], lp["perm_m"]]
    ones = ATTN_ONES_ROWS
    heads = lambda h, d: (jax.ShapeDtypeStruct((B, h, n, d), BF16), pl.BlockSpec((1, h, tm, d), lambda i, b: (b, 0, i, 0)))
    vals_t = lambda h, d: (jax.ShapeDtypeStruct((B, h * (d + ones), n), BF16),
                           pl.BlockSpec((1, h * (d + ones), tm), lambda i, b: (b, 0, i)))
    rows = lambda d: (jax.ShapeDtypeStruct((B, n, d), F32), row(d))
    outs = [heads(GQA_KV_HEADS, GQA_HD), vals_t(GQA_KV_HEADS, GQA_HD), heads(MLA_HEADS, MLA_PAD), vals_t(MLA_HEADS, MLA_V)]
    if not kv_only:
        outs += [heads(GQA_HEADS, GQA_HD), heads(MLA_HEADS, MLA_PAD), rows(256), rows(768)]
    return pl.pallas_call(
        functools.partial(_inproj_kernel, kv_only),
        grid=(n // tm, B),
        in_specs=[row(D),
                  pl.BlockSpec((1, 1, D), lambda i, b: (b, 0, 0)),
                  pl.BlockSpec((1, 1, D), lambda i, b: (b, 0, 0)),
                  full(lp["norm1_g"]), full(w)] + [full(a) for a in params] + [tab(a) for a in tabs],
        out_specs=[spec for _, spec in outs],
        out_shape=[shape for shape, _ in outs],
        compiler_params=_cp(("arbitrary", "arbitrary")),
        name="inproj_kv" if kv_only else "inproj",
    )(xs, sh, sc, lp["norm1_g"], w, *params, *tabs)


ATTN_KEY_BLOCK = 256
ATTN_Q_TILE = 512
ATTN_ONES_ROWS = 16
LOG2E = math.log2(math.e)


def _attn_kernel(heads, kv_heads, dv, nseg, q_ref, *refs):
    k_refs, vt_refs, o_ref = refs[:nseg], refs[nseg:2 * nseg], refs[2 * nseg]
    grp = heads // kv_heads
    kb = ATTN_KEY_BLOCK
    dve = dv + ATTN_ONES_ROWS
    blocks = [(i, r) for i in range(nseg) for r in range(0, k_refs[i].shape[2], kb)]

    qs = [q_ref[0, hh] for hh in range(heads)]
    run_max = [None] * heads
    acc = [None] * heads
    s_prev = [None] * heads
    for j in range(len(blocks) + 1):
        s_cur = [None] * heads
        for hh in range(heads):
            g = hh // grp
            if j < len(blocks):
                i, r = blocks[j]
                s_cur[hh] = _dot_nt(k_refs[i][0, g, r:r + kb, :], qs[hh])
            if j >= 1:
                i, r = blocks[j - 1]
                bm = jnp.max(s_prev[hh], axis=0, keepdims=True)
                m_new = bm if run_max[hh] is None else jnp.maximum(run_max[hh], bm)
                t = _dot(vt_refs[i][0, g * dve:(g + 1) * dve, r:r + kb], jnp.exp2(s_prev[hh] - m_new).astype(BF16))
                acc[hh] = t if acc[hh] is None else acc[hh] * jnp.exp2(run_max[hh] - m_new) + t
                run_max[hh] = m_new
        s_prev = s_cur
    o_ref[0] = jnp.concatenate([a[0:dv] / a[dv:dv + 1] for a in acc], axis=0).T.astype(o_ref.dtype)


def _attn_call(q, ks, vts, heads, kv_heads, dv, name):
    B, _, n, dk = q.shape
    tq = min(n, ATTN_Q_TILE)
    kspec = lambda a: pl.BlockSpec((1,) + a.shape[1:], lambda b, i: (b,) + (0,) * (a.ndim - 1))
    return pl.pallas_call(
        functools.partial(_attn_kernel, heads, kv_heads, dv, len(ks)),
        grid=(B, n // tq),
        in_specs=[pl.BlockSpec((1, heads, tq, dk), lambda b, i: (b, 0, i, 0))]
        + [kspec(a) for a in ks] + [kspec(a) for a in vts],
        out_specs=pl.BlockSpec((1, tq, heads * dv), lambda b, i: (b, i, 0)),
        out_shape=jax.ShapeDtypeStruct((B, n, heads * dv), BF16),
        compiler_params=_cp(("arbitrary", "arbitrary")),
        name=name,
    )(q, *ks, *vts)


POOL_HALO = 8
assert all(b == 2 * a for a, b in zip(POOL_WINDOWS, POOL_WINDOWS[1:])) and POOL_WINDOWS[0] == 2
assert POOL_WINDOWS[-1] // 2 <= POOL_HALO


def _pool_kernel(n, u_ref, cnt_ref, wbd_ref, scale_ref, o_ref, ue_ref):
    w = u_ref.shape[2]
    gc = w // len(POOL_WINDOWS)
    ne = n + 2 * POOL_HALO
    u = u_ref[0]
    ue_ref[0:POOL_HALO, :] = jnp.zeros((POOL_HALO, w), F32)
    ue_ref[POOL_HALO:POOL_HALO + n, :] = u
    ue_ref[POOL_HALO + n:ne, :] = jnp.zeros((POOL_HALO, w), F32)
    ue = ue_ref[...]
    down = lambda x, k: pltpu.roll(x, k, axis=0)
    up = lambda x, k: pltpu.roll(x, ne - k, axis=0)
    sums = [down(ue, 1) + ue]
    for win in POOL_WINDOWS[:-1]:
        sums.append(down(sums[-1], win // 2) + up(sums[-1], win // 2))
    lane = lax.broadcasted_iota(I32, (n, w), 1)
    total = sums[-1][POOL_HALO:POOL_HALO + n]
    for gi in range(len(POOL_WINDOWS) - 2, -1, -1):
        total = jnp.where(lane < (gi + 1) * gc, sums[gi][POOL_HALO:POOL_HALO + n], total)
    d = total / cnt_ref[...] - u
    o_ref[0] = (_dot(d.astype(BF16), wbd_ref[...]) * scale_ref[...]).astype(o_ref.dtype)


def _pool_counts(n, width):
    t = np.arange(n)
    gc = width // len(POOL_WINDOWS)
    cols = []
    for win in POOL_WINDOWS:
        left = win // 2
        right = win - 1 - left
        cnt = np.minimum(t + right, n - 1) - np.maximum(t - left, 0) + 1
        cols.append(np.repeat(cnt[:, None], gc, axis=1))
    return jnp.asarray(np.concatenate(cols, axis=1), dtype=F32)


def _pool_call(u, lp):
    B, n, W = u.shape
    return pl.pallas_call(
        functools.partial(_pool_kernel, n),
        grid=(B,),
        in_specs=[pl.BlockSpec((1, n, W), lambda b: (b, 0, 0)),
                  pl.BlockSpec((n, W), lambda b: (0, 0)),
                  pl.BlockSpec((W, W), lambda b: (0, 0)),
                  pl.BlockSpec((1, W), lambda b: (0, 0))],
        out_specs=pl.BlockSpec((1, n, W), lambda b: (b, 0, 0)),
        out_shape=jax.ShapeDtypeStruct((B, n, W), BF16),
        scratch_shapes=[pltpu.VMEM((n + 2 * POOL_HALO, W), F32)],
        compiler_params=_cp(("arbitrary",)),
        name="pool",
    )(u, _pool_counts(n, W), lp["pool_bd"], lp["pool_scale"])


HY_BLOCK = 512
HY_SPEC_DTYPE = F32


def _dot3(a_hi, a_lo, x):
    x_hi = x.astype(BF16)
    x_lo = (x - x_hi.astype(F32)).astype(BF16)
    return _dot(a_hi, x_hi) + _dot(a_hi, x_lo) + _dot(a_lo, x_hi)


def _hyfilt_kernel(n, t, feats_ref, w1_ref, b1_ref, w2_ref, b2_ref, w3_ref, fr_ref, decay_ref,
                   chi_ref, clo_ref, shi_ref, slo_ref, hre_ref, him_ref, hn_ref, tc_ref, ts_ref, edge_ref):
    di = pl.program_id(0)
    nb = n // t
    hc2 = w3_ref.shape[1] // 2

    @pl.when(di == 0)
    def _():
        fr = fr_ref[...]
        hid = jnp.sin(fr * (jnp.dot(feats_ref[...], w1_ref[...], precision=HIGHEST, preferred_element_type=F32)
                            + b1_ref[...]))
        hid = jnp.sin(fr * (jnp.dot(hid, w2_ref[...], precision=HIGHEST, preferred_element_type=F32) + b2_ref[...]))
        h = jnp.dot(hid, w3_ref[...], precision=HIGHEST, preferred_element_type=F32)
        dec = decay_ref[...]
        dec2 = jnp.concatenate([dec, dec], axis=1)
        row = lax.broadcasted_iota(I32, (n, 1), 0)
        fwd = h[:, 0:hc2] * dec2
        bwd = jnp.where(row == 0, 0.0, h[:, hc2:2 * hc2] * dec2)
        scl = lax.rsqrt(jnp.sum(fwd * fwd + bwd * bwd, axis=0, keepdims=True) + EPS)
        alt = jnp.where(lax.broadcasted_iota(I32, (t, 1), 0) % 2 == 0, 1.0, -1.0)
        for fam, filt in enumerate((fwd * scl, bwd * scl)):
            for j in range(nb):
                x = filt[j * t:(j + 1) * t]
                tc_ref[fam, j] = _dot3(chi_ref[...], clo_ref[...], x)
                ts_ref[fam, j] = _dot3(shi_ref[...], slo_ref[...], x)
                r = 2 * (fam * nb + j)
                edge_ref[r:r + 1, :] = x[0:1, :]
                edge_ref[r + 1:r + 2, :] = jnp.sum(x * alt, axis=0, keepdims=True)

    k = lax.broadcasted_iota(I32, (t, 1), 0)
    wk = jnp.where(k == 0, 1.0 / (2 * t), 2.0 / (2 * t))
    sig = jnp.where(k % 2 == 0, 1.0, -1.0)
    first = lambda fam, j: edge_ref[2 * (fam * nb + j):2 * (fam * nb + j) + 1, :]
    altsum = lambda fam, j: edge_ref[2 * (fam * nb + j) + 1:2 * (fam * nb + j) + 2, :]
    for dd in range(2 * nb - 1):
        d = dd - (nb - 1)

        def put(ref, val):
            for o in range(2):
                ref[o, 0] = val[:, o * (hc2 // 2):(o + 1) * (hc2 // 2)].astype(ref.dtype)

        @pl.when(di == dd)
        def _(d=d):
            if d == 0:
                put(hre_ref, wk * (tc_ref[0, 0] + tc_ref[1, 0]))
                put(him_ref, wk * (ts_ref[1, 0] - ts_ref[0, 0]))
                put(hn_ref, (altsum(0, 0) + altsum(1, 0)) * (1.0 / (2 * t)))
            else:
                fam, e, sgn = (0, d, -1.0) if d > 0 else (1, -d, 1.0)
                put(hre_ref, wk * (sig * (tc_ref[fam, e - 1] - first(fam, e - 1)) + tc_ref[fam, e]))
                put(him_ref, (sgn * wk) * (sig * ts_ref[fam, e - 1] + ts_ref[fam, e]))
                put(hn_ref, (altsum(fam, e - 1) - first(fam, e - 1) + altsum(fam, e)) * (1.0 / (2 * t)))


def _hyfilt_call(lp, hc):
    n, t = hc["n"], hc["t"]
    nb = n // t
    nd = 2 * nb - 1
    hc2 = lp["hy_w3"].shape[1] // 2
    ch = hc2 // 2
    full = lambda a: pl.BlockSpec(a.shape, lambda i: (0,) * a.ndim)
    args = [hc["feats"], lp["hy_w1"], lp["hy_b1"], lp["hy_w2"], lp["hy_b2"], lp["hy_w3"], lp["hy_freq"], hc["decay"],
            hc["c_hi"], hc["c_lo"], hc["s_hi"], hc["s_lo"]]
    return pl.pallas_call(
        functools.partial(_hyfilt_kernel, n, t),
        grid=(nd,),
        in_specs=[full(a) for a in args],
        out_specs=[pl.BlockSpec((2, 1, t, ch), lambda i: (0, i, 0, 0)), pl.BlockSpec((2, 1, t, ch), lambda i: (0, i, 0, 0)),
                   pl.BlockSpec((2, 1, 1, ch), lambda i: (0, i, 0, 0))],
        out_shape=[jax.ShapeDtypeStruct((2, nd, t, ch), HY_SPEC_DTYPE), jax.ShapeDtypeStruct((2, nd, t, ch), HY_SPEC_DTYPE),
                   jax.ShapeDtypeStruct((2, nd, 1, ch), F32)],
        scratch_shapes=[pltpu.VMEM((2, nb, t, hc2), F32), pltpu.VMEM((2, nb, t, hc2), F32),
                        pltpu.VMEM((4 * nb, hc2), F32)],
        compiler_params=_cp(("arbitrary",)),
        name="hyena_filter",
    )(*args)


HY_FREQ_ROWS = 32


def _hyena_kernel(n, t, u_ref, cw_ref, cb_ref, c_ref, s_ref, hre_ref, him_ref, hn_ref, bias_ref, o_ref,
                  uc_ref, z_ref, a_ref, b_ref, p_ref, q_ref):
    ch = o_ref.shape[2]
    nb = n // t
    row = lax.broadcasted_iota(I32, (t, 1), 0)
    alt = jnp.where(row % 2 == 0, 1.0, -1.0)
    sub = lax.broadcasted_iota(I32, (SUBLANES, 1), 0)
    for g in range(3):
        cols = slice(g * ch, (g + 1) * ch)
        for j in range(nb):
            r0 = j * t
            u = u_ref[0, r0:r0 + t, cols]
            prev = jnp.zeros((1, ch), F32) if j == 0 else u_ref[0, r0 - 1:r0, cols]
            nxt = jnp.zeros((1, ch), F32) if j == nb - 1 else u_ref[0, r0 + t:r0 + t + 1, cols]
            up = pltpu.roll(u, 1, axis=0)
            un = pltpu.roll(u, t - 1, axis=0)
            up = jnp.concatenate([jnp.where(sub == 0, prev, up[0:SUBLANES]), up[SUBLANES:]], axis=0)
            un = jnp.concatenate([un[:t - SUBLANES], jnp.where(sub == SUBLANES - 1, nxt, un[t - SUBLANES:])], axis=0)
            uc_ref[g, r0:r0 + t, :] = (up * cw_ref[0:1, cols] + u * cw_ref[1:2, cols] + un * cw_ref[2:3, cols]
                                       + cb_ref[:, cols])

    z_ref[...] = uc_ref[0]

    def long_conv(o, carry):
        nyq = []
        for j in range(nb):
            z = z_ref[j * t:(j + 1) * t, :]
            zb = z.astype(BF16)
            a_ref[j] = _dot(c_ref[...], zb).astype(a_ref.dtype)
            b_ref[j] = _dot(s_ref[...], zb).astype(b_ref.dtype)
            nyq.append(jnp.sum(z * alt, axis=0, keepdims=True))
        for r0 in range(0, t, HY_FREQ_ROWS):
            rows = slice(r0, r0 + HY_FREQ_ROWS)
            ab = [(a_ref[j, rows, :], b_ref[j, rows, :]) for j in range(nb)]
            for i in range(nb):
                p = q = None
                for j, (a, b) in enumerate(ab):
                    d = i - j + nb - 1
                    hre, him = hre_ref[o, d, rows, :], him_ref[o, d, rows, :]
                    tp = a * hre + b * him
                    tq = b * hre - a * him
                    p = tp if p is None else p + tp
                    q = tq if q is None else q + tq
                p_ref[i, rows, :] = p.astype(BF16)
                q_ref[i, rows, :] = q.astype(BF16)
        for i in range(nb):
            yn = functools.reduce(jnp.add, [nyq[j] * hn_ref[o, i - j + nb - 1] for j in range(nb)])
            y = _dot(c_ref[...], p_ref[i]) + _dot(s_ref[...], q_ref[i])
            blk = slice(i * t, (i + 1) * t)
            z_ref[blk, :] = uc_ref[o + 1, blk, :] * (y + alt * yn + z_ref[blk, :] * bias_ref[o])
        return carry

    lax.fori_loop(0, 2, long_conv, 0)
    o_ref[0] = z_ref[...].astype(o_ref.dtype)


def _hyena_call(u, lp, hc, filt):
    B, n, W = u.shape
    ch = W // 3
    t = hc["t"]
    hre, him, hn = filt
    once = lambda a: pl.BlockSpec(a.shape, lambda b: (0,) * a.ndim, pipeline_mode=pl.Buffered(1))
    return pl.pallas_call(
        functools.partial(_hyena_kernel, n, t),
        scratch_shapes=[pltpu.VMEM((3, n, ch), F32), pltpu.VMEM((n, ch), F32),
                        pltpu.VMEM((n // t, t, ch), HY_SPEC_DTYPE), pltpu.VMEM((n // t, t, ch), HY_SPEC_DTYPE),
                        pltpu.VMEM((n // t, t, ch), BF16), pltpu.VMEM((n // t, t, ch), BF16)],
        grid=(B,),
        in_specs=[pl.BlockSpec((1, n, W), lambda b: (b, 0, 0)),
                  once(lp["hy_conv_w"]), once(lp["hy_conv_b"]), once(hc["c16"]), once(hc["s16"]),
                  once(hre), once(him), once(hn), once(lp["hy_bias"])],
        out_specs=pl.BlockSpec((1, n, ch), lambda b: (b, 0, 0)),
        out_shape=jax.ShapeDtypeStruct((B, n, ch), BF16),
        compiler_params=_cp(("arbitrary",)),
        name="hyena",
    )(u, lp["hy_conv_w"], lp["hy_conv_b"], hc["c16"], hc["s16"], hre, him, hn, lp["hy_bias"])


def _outproj_kernel(x_ref, g_ref, a_ref, b_ref, c_ref, d_ref, w_ref, o_ref):
    mixed = (_dot(a_ref[0], w_ref[0]) + _dot(b_ref[0], w_ref[1])
             + _dot(c_ref[0], w_ref[2]) + _dot(d_ref[0], w_ref[3]))
    o_ref[0] = x_ref[0] + g_ref[0] * mixed


def _outproj_call(xs, gate, parts, lp):
    B, n, D = xs.shape
    tm = min(n, 512)
    gw = parts[0].shape[2]
    return pl.pallas_call(
        _outproj_kernel,
        grid=(B, n // tm),
        in_specs=[pl.BlockSpec((1, tm, D), lambda b, i: (b, i, 0)),
                  pl.BlockSpec((1, 1, D), lambda b, i: (b, 0, 0))]
        + [pl.BlockSpec((1, tm, gw), lambda b, i: (b, i, 0))] * 4
        + [pl.BlockSpec(lp["w_out"].shape, lambda b, i: (0, 0, 0))],
        out_specs=pl.BlockSpec((1, tm, D), lambda b, i: (b, i, 0)),
        out_shape=jax.ShapeDtypeStruct((B, n, D), F32),
        compiler_params=_cp(("arbitrary", "arbitrary")),
        name="outproj",
    )(xs, gate, *parts, lp["w_out"])


PREFIX_BLOCK = 256


def _prefix_lanes(m, tri):
    e, n = m.shape
    nb = n // PREFIX_BLOCK
    stk = jnp.concatenate([m[:, j * PREFIX_BLOCK:(j + 1) * PREFIX_BLOCK] for j in range(nb)], axis=0)
    pre = _dot(stk.astype(BF16), tri)
    outs = []
    carry = jnp.zeros((e, 1), F32)
    for j in range(nb):
        pj = pre[j * e:(j + 1) * e]
        outs.append(pj + carry)
        carry = carry + pj[:, PREFIX_BLOCK - 1:PREFIX_BLOCK]
    return jnp.concatenate(outs, axis=1)


SLOT_RADIX = 16
RADIX_BITS = 4


def _split3(x):
    a = x.astype(BF16).astype(F32)
    r = x - a
    b = r.astype(BF16).astype(F32)
    c = (r - b).astype(BF16).astype(F32)
    return a, b, c


def _digit_onehots(v, weights=None):
    e_n = v.shape[0]
    hi_d = jnp.floor(v * (1.0 / SLOT_RADIX))
    lo_d = v - SLOT_RADIX * hi_d
    dig = lax.broadcasted_iota(I32, (SLOT_RADIX, v.shape[1]), 0).astype(F32)
    his = [hi_d[e:e + 1, :] == dig for e in range(e_n)]
    lo = jnp.concatenate([jnp.where(lo_d[e:e + 1, :] == dig, 1.0, 0.0) for e in range(e_n)], axis=0)
    if weights is None:
        hi = jnp.concatenate([jnp.where(m, 1.0, 0.0) for m in his], axis=0)
    else:
        hi = jnp.concatenate([jnp.where(m, w[e:e + 1, :], 0.0) for w in weights for e, m in enumerate(his)], axis=0)
    return hi.astype(BF16), lo.astype(BF16)


def _route_kernel(n, cap, x_ref, sh_ref, sc_ref, g_ref, wr_ref, tri_ref, fold_ref, pre_ref, low_ref,
                  hcm_ref, idx_ref, gate_ref):
    x = x_ref[0]
    h = _rms_rows(x) * g_ref[...] * (1.0 + sc_ref[0]) + sh_ref[0]
    for c in range(h.shape[1] // LANES):
        hcm_ref[0, pl.ds(c, n, stride=SUBLANES), :] = h[:, c * LANES:(c + 1) * LANES]
    h_hi = h.astype(BF16)
    h_lo = (h - h_hi.astype(F32)).astype(BF16)
    rt = _dot(jnp.concatenate([h_hi, h_lo], axis=1), wr_ref[...]).T
    lt = rt[0:N_EXPERTS, :] + rt[N_EXPERTS:2 * N_EXPERTS, :]
    ex = jnp.exp(lt - jnp.max(lt, axis=0, keepdims=True))
    aff = ex / jnp.sum(ex, axis=0, keepdims=True)
    bits = pltpu.bitcast(aff, I32)
    thr = jnp.zeros((N_EXPERTS, 1), I32)
    for shift in range(32 - RADIX_BITS, -1, -RADIX_BITS):
        top = (1 << (31 - shift)) if shift + RADIX_BITS > 31 else (1 << RADIX_BITS)
        digit = jnp.zeros((N_EXPERTS, 1), I32)
        for c in range(1, top):
            cnt = jnp.sum(jnp.where(bits >= (thr | (c << shift)), 1.0, 0.0), axis=1, keepdims=True)
            digit = digit + jnp.where(cnt >= cap, 1, 0)
        thr = thr | (digit << shift)
    gt = jnp.where(bits > thr, 1.0, 0.0)
    eq = jnp.where(bits == thr, 1.0, 0.0)
    need = cap - jnp.sum(gt, axis=1, keepdims=True)
    tri = tri_ref[...]
    sel = gt + eq * jnp.where(_prefix_lanes(eq, tri) <= need, 1.0, 0.0)
    incl = _prefix_lanes(sel, tri)
    rows = N_EXPERTS * SLOT_RADIX
    shift = SLOT_RADIX.bit_length() - 1
    same = ((lax.broadcasted_iota(I32, (rows, rows), 0) >> shift)
            == (lax.broadcasted_iota(I32, (rows, rows), 1) >> shift))
    fold = fold_ref[...]
    a_hi, a_lo = _digit_onehots(incl)
    hist = jnp.where(same, _dot_nt(a_hi, a_lo), 0.0)
    hist = _dot_split(hist, fold)
    pre = _dot_split(hist, pre_ref[...])
    tot = jnp.broadcast_to(pre[:, SLOT_RADIX - 1:SLOT_RADIX], pre.shape)
    t_hi = tot.astype(BF16)
    t_lo = (tot - t_hi.astype(F32)).astype(BF16)
    idx_ref[0] = (pre + _dot(low_ref[...], t_hi) + _dot(low_ref[...], t_lo)).astype(I32)
    g_hi, g_lo = _digit_onehots(incl - 1.0, _split3(sel * aff))
    gsum = _dot_nt(g_hi, g_lo)
    gsum = jnp.where(same, gsum[0:rows] + gsum[rows:2 * rows] + gsum[2 * rows:3 * rows], 0.0)
    gate_ref[0] = functools.reduce(jnp.add, [_dot(part.astype(BF16), fold) for part in _split3(gsum)])


def _route_consts():
    rows = N_EXPERTS * SLOT_RADIX
    i = np.arange(rows)
    j = np.arange(LANES)
    fold = (i[:, None] % SLOT_RADIX == j[None, :]).astype(np.float32)
    pre = ((j[:, None] <= j[None, :]) & (j[None, :] < SLOT_RADIX)).astype(np.float32)
    low = ((i[:, None] // SLOT_RADIX == i[None, :] // SLOT_RADIX) & (i[None, :] < i[:, None])).astype(np.float32)
    tri = np.triu(np.ones((PREFIX_BLOCK, PREFIX_BLOCK), np.float32))
    return tuple(jnp.asarray(a, dtype=BF16) for a in (tri, fold, pre, low))


def _route_call(xs, sh, sc, lp, consts):
    B, n, D = xs.shape
    cap = EC_CAPACITY * n // N_EXPERTS
    assert cap <= SLOT_RADIX * SLOT_RADIX
    rows = n * (D // LANES)
    srows = N_EXPERTS * SLOT_RADIX
    full = lambda a: pl.BlockSpec(a.shape, lambda b: (0,) * a.ndim)
    return pl.pallas_call(
        functools.partial(_route_kernel, n, cap),
        grid=(B,),
        in_specs=[pl.BlockSpec((1, n, D), lambda b: (b, 0, 0)),
                  pl.BlockSpec((1, 1, D), lambda b: (b, 0, 0)),
                  pl.BlockSpec((1, 1, D), lambda b: (b, 0, 0)),
                  full(lp["norm2_g"]), full(lp["router_split"])] + [full(a) for a in consts],
        out_specs=[pl.BlockSpec((1, rows, LANES), lambda b: (b, 0, 0)),
                   pl.BlockSpec((1, srows, LANES), lambda b: (b, 0, 0)),
                   pl.BlockSpec((1, srows, LANES), lambda b: (b, 0, 0))],
        out_shape=[jax.ShapeDtypeStruct((B, rows, LANES), F32),
                   jax.ShapeDtypeStruct((B, srows, LANES), I32),
                   jax.ShapeDtypeStruct((B, srows, LANES), F32)],
        compiler_params=_cp(("arbitrary",)),
        name="route",
    )(xs, sh, sc, lp["norm2_g"], lp["router_split"], *consts)


GATHER_UNROLL = 8


FFN_ROWS = 256


def _ffn_kernel(cap, bt, idx_ref, h_ref, gate_ref, wg_ref, wu_ref, wd_ref, o_ref, xga_ref, xgb_ref, ya_ref, yb_ref):
    b0 = pl.program_id(0) * bt
    k = pl.program_id(1)
    steps = pl.num_programs(1)
    chunks = wg_ref.shape[1] // LANES
    m = bt * cap
    tile = lambda r: pl.ds(pl.multiple_of(r * SUBLANES, SUBLANES), SUBLANES)
    e0 = 2 * k
    e1 = e0 + 1
    wrap = lambda e: jnp.where(e < 0, e + N_EXPERTS, jnp.where(e >= N_EXPERTS, e - N_EXPERTS, e))

    def gather(e, xg_ref):
        for bi in range(bt):
            base = ((b0 + bi) * N_EXPERTS + e) * cap
            for r in range(cap):
                xg_ref[tile(bi * cap + r), :] = h_ref[bi, tile(idx_ref[base + r]), :]

    def scatter(e, y_ref):
        for bi in range(bt):
            base = ((b0 + bi) * N_EXPERTS + e) * cap
            for r0 in range(0, cap, GATHER_UNROLL):
                upd = []
                for r in range(r0, r0 + GATHER_UNROLL):
                    dst = tile(idx_ref[base + r])
                    upd.append((dst, o_ref[bi, dst, :] + y_ref[tile(bi * cap + r), :]))
                for dst, v in upd:
                    o_ref[bi, dst, :] = v

    def ffn(e, w, xg_ref, y_ref):
        xt = jnp.concatenate([xg_ref[pl.ds(c, m, stride=SUBLANES), :] for c in range(chunks)], axis=1).astype(BF16)
        a = _dot(xt, wg_ref[w])
        u = _dot(xt, wu_ref[w])
        hmid = (a * (1.0 / (1.0 + jnp.exp(-a))) * u).astype(BF16)
        lane = lax.broadcasted_iota(I32, (m, LANES), 1)
        gcol = jnp.sum(jnp.where(lane == e, gate_ref[...].reshape(m, LANES), 0.0), axis=1, keepdims=True)
        y = _dot(hmid, wd_ref[w]) * gcol
        for c in range(chunks):
            y_ref[pl.ds(c, m, stride=SUBLANES), :] = y[:, c * LANES:(c + 1) * LANES]

    @pl.when(k == 0)
    def _():
        o_ref[...] = jnp.zeros_like(o_ref)
        yb_ref[...] = jnp.zeros_like(yb_ref)
        gather(e0, xga_ref)

    ffn(e0, 0, xga_ref, ya_ref)
    gather(e1, xgb_ref)
    scatter(wrap(e0 - 1), yb_ref)
    ffn(e1, 1, xgb_ref, yb_ref)
    gather(wrap(e1 + 1), xga_ref)
    scatter(e0, ya_ref)

    @pl.when(k == steps - 1)
    def _():
        scatter(e1, yb_ref)


def _ffn_call(idx_flat, hcm, gate, lp, cap):
    B, rows, _ = hcm.shape
    D = lp["w_gate"].shape[1]
    ff = lp["w_gate"].shape[2]
    bt = min(B, max(1, FFN_ROWS // cap))
    m = bt * cap
    return pl.pallas_call(
        functools.partial(_ffn_kernel, cap, bt),
        grid=(B // bt, N_EXPERTS // 2),
        in_specs=[pl.BlockSpec(memory_space=pltpu.SMEM),
                  pl.BlockSpec((bt, rows, LANES), lambda b, k: (b, 0, 0)),
                  pl.BlockSpec((bt, cap, LANES), lambda b, k: (b, 0, 0)),
                  pl.BlockSpec((2, D, ff), lambda b, k: (k, 0, 0)),
                  pl.BlockSpec((2, D, ff), lambda b, k: (k, 0, 0)),
                  pl.BlockSpec((2, ff, D), lambda b, k: (k, 0, 0))],
        out_specs=pl.BlockSpec((bt, rows, LANES), lambda b, k: (b, 0, 0)),
        out_shape=jax.ShapeDtypeStruct((B, rows, LANES), F32),
        scratch_shapes=[pltpu.VMEM((m * SUBLANES, LANES), F32)] * 4,
        compiler_params=_cp(("arbitrary", "arbitrary")),
        name="expert_ffn",
    )(idx_flat, hcm, gate, lp["w_gate"], lp["w_up"], lp["w_down"])


def _moe_residual_kernel(x_ref, g_ref, m_ref, o_ref):
    tm = x_ref.shape[1]
    chunks = x_ref.shape[2] // LANES
    moe = jnp.concatenate([m_ref[0, pl.ds(c, tm, stride=SUBLANES), :] for c in range(chunks)], axis=1)
    o_ref[0] = x_ref[0] + g_ref[0] * moe


def _moe_residual_call(xs, gate, moe_cm):
    B, n, D = xs.shape
    tm = min(n, 512)
    return pl.pallas_call(
        _moe_residual_kernel,
        grid=(B, n // tm),
        in_specs=[pl.BlockSpec((1, tm, D), lambda b, i: (b, i, 0)),
                  pl.BlockSpec((1, 1, D), lambda b, i: (b, 0, 0)),
                  pl.BlockSpec((1, tm * (D // LANES), LANES), lambda b, i: (b, i, 0))],
        out_specs=pl.BlockSpec((1, tm, D), lambda b, i: (b, i, 0)),
        out_shape=jax.ShapeDtypeStruct((B, n, D), F32),
        compiler_params=_cp(("arbitrary", "arbitrary")),
        name="moe_residual",
    )(xs, gate, moe_cm)


def _axial_tables(n, head_pat, half, reps, roped):
    t = jnp.arange(n)
    pos = (jnp.floor_divide(t, GRID_W).astype(F32), jnp.remainder(t, GRID_W).astype(F32))
    inv = ROPE_THETA ** (-jnp.arange(half, dtype=F32) / half)
    cols_c, cols_s = [], []
    one = jnp.ones((n,), F32)
    zero = jnp.zeros((n,), F32)
    for pat in head_pat:
        if pat is None or not roped:
            cols_c.append(one), cols_s.append(zero)
            continue
        axis, f, first = pat
        ang = pos[axis] * inv[f]
        cols_c.append(jnp.cos(ang))
        cols_s.append(-jnp.sin(ang) if first else jnp.sin(ang))
    mk = lambda cols: jnp.tile(jnp.stack(cols, axis=1), (1, reps))
    return mk(cols_c), mk(cols_s)


def _partner_matrix(head_pat, half, reps):
    hw = len(head_pat)
    p = np.zeros((hw * reps, hw * reps), np.float32)
    for r in range(reps):
        for j, pat in enumerate(head_pat):
            if pat is not None:
                p[r * hw + (j + half if pat[2] else j - half), r * hw + j] = 1.0
    return jnp.asarray(p, dtype=BF16)


def _rope_patterns():
    gh = GQA_HD // 4
    gpat = [(j // (2 * gh), j % gh, (j % (2 * gh)) < gh) for j in range(GQA_HD)]
    mh = MLA_ROPE // 4
    mpat = [None] * MLA_NOPE + [(j // (2 * mh), j % mh, (j % (2 * mh)) < mh) for j in range(MLA_ROPE)]
    mpat += [None] * (MLA_PAD - MLA_HD)
    return gpat, gh, mpat, mh


def _rope_tables(n, roped):
    gpat, gh, mpat, mh = _rope_patterns()
    return _axial_tables(n, gpat, gh, GQA_HEADS, roped) + _axial_tables(n, mpat, mh, MLA_HEADS, roped)


def _rope_partners():
    gpat, gh, mpat, mh = _rope_patterns()
    return _partner_matrix(gpat, gh, GQA_HEADS), _partner_matrix(mpat, mh, 2)


def _hyena_consts(n, ch):
    f32 = F32
    tb = min(n, HY_BLOCK)
    k = jnp.arange(tb, dtype=I32)
    ang = jnp.remainder(k[:, None] * k[None, :], 2 * tb).astype(f32) * (math.pi / tb)
    c32, s32 = jnp.cos(ang), jnp.sin(ang)
    c_hi, s_hi = c32.astype(BF16), s32.astype(BF16)
    c_lo, s_lo = (c32 - c_hi.astype(f32)).astype(BF16), (s32 - s_hi.astype(f32)).astype(BF16)
    t = jnp.linspace(0.0, 1.0, n, dtype=f32)[:, None]
    lag = jnp.arange(n, dtype=f32)[:, None]
    bands = jnp.linspace(1e-4, HY_BANDS - 1, HY_BANDS, dtype=f32)[None, :]
    a = (2.0 * math.pi / n) * lag * bands
    feats = jnp.concatenate([t, jnp.cos(a), -jnp.sin(a)], axis=-1)
    feats = jnp.pad(feats, ((0, 0), (0, LANES - feats.shape[1])))
    deltas = jnp.abs(jnp.linspace(math.log(HY_DECAY_TARGET) / HY_SLOW_PCT,
                                  math.log(HY_DECAY_TARGET) / HY_FAST_PCT, ch, dtype=f32))
    decay = jnp.exp(-t * deltas[None, :])
    return {"n": n, "t": tb, "c16": c_hi, "s16": s_hi, "c_hi": c_hi, "c_lo": c_lo, "s_hi": s_hi, "s_lo": s_lo,
            "feats": feats, "decay": decay}


def _seg_matrix(width, seg):
    i = np.arange(width)
    return jnp.asarray((i[:, None] // seg == i[None, :] // seg).astype(np.float32), dtype=BF16)


def _router_split(rw):
    d = rw.shape[0]
    hi = rw.astype(BF16)
    lo = (rw - hi.astype(F32)).astype(BF16)
    top = jnp.concatenate([hi, lo, jnp.zeros((d, LANES - 2 * N_EXPERTS), BF16)], axis=1)
    bot = jnp.concatenate([hi, jnp.zeros((d, LANES - N_EXPERTS), BF16)], axis=1)
    return jnp.concatenate([top, bot], axis=0)


def _layer_params(l, w):
    D = w["w_in"].shape[1]
    wi = w["w_in"][l]
    z = lambda c: jnp.zeros((D, c), F32)
    w_in_r = jnp.concatenate(
        [wi[:, 0:384], z(MLA_NOPE), wi[:, 384:416], z(MLA_PAD - MLA_HD), wi[:, 416:]], axis=1).astype(BF16)
    ukv = w["mla_w_ukv"][l].reshape(-1, MLA_HEADS, MLA_NOPE + MLA_V)
    kpart = jnp.pad(ukv[:, :, :MLA_NOPE], ((0, 0), (0, 0), (0, MLA_PAD - MLA_NOPE)))
    w_ukv_r = jnp.concatenate([kpart.reshape(ukv.shape[0], -1), ukv[:, :, MLA_NOPE:].reshape(ukv.shape[0], -1)],
                              axis=1).astype(BF16)
    uq = w["mla_w_uq"][l].reshape(-1, MLA_HEADS, MLA_HD)
    w_uq_r = jnp.pad(uq, ((0, 0), (0, 0), (0, MLA_PAD - MLA_HD))).reshape(uq.shape[0], -1).astype(BF16)
    padg = lambda g: jnp.tile(jnp.pad(g, (0, MLA_PAD - MLA_HD)), MLA_HEADS)[None, :]
    pw = w["pool_w"][l]
    gc = pw.shape[1]
    pool_bd = jnp.zeros((len(POOL_WINDOWS) * gc,) * 2, F32)
    for gi in range(len(POOL_WINDOWS)):
        pool_bd = pool_bd.at[gi * gc:(gi + 1) * gc, gi * gc:(gi + 1) * gc].set(pw[gi])
    nf = w["hy_f_w1"].shape[1]
    perm_g, perm_m = _rope_partners()
    return {
        "perm_g": perm_g, "perm_m": perm_m,
        "norm1_g": w["norm1_g"][l][None, :], "norm2_g": w["norm2_g"][l][None, :],
        "w_in_r": w_in_r, "w_ukv_r": w_ukv_r, "w_uq_r": w_uq_r,
        "g_k": jnp.tile(w["gqa_knorm_g"][l], GQA_KV_HEADS)[None, :],
        "g_q": jnp.tile(w["gqa_qnorm_g"][l], GQA_HEADS)[None, :],
        "g_ckv": w["mla_ckv_g"][l][None, :], "g_cq": w["mla_cq_g"][l][None, :],
        "g_mk": padg(w["mla_knorm_g"][l]), "g_mq": padg(w["mla_qnorm_g"][l]),
        "seg": _seg_matrix(GQA_HEADS * GQA_HD, GQA_HD),
        "pool_bd": pool_bd.astype(BF16), "pool_scale": w["pool_scale"][l][None, :],
        "hy_conv_w": w["hy_conv_w"][l], "hy_conv_b": w["hy_conv_b"][l][None, :],
        "hy_w1": jnp.pad(w["hy_f_w1"][l], ((0, LANES - nf), (0, 0))), "hy_b1": w["hy_f_b1"][l][None, :],
        "hy_w2": w["hy_f_w2"][l], "hy_b2": w["hy_f_b2"][l][None, :], "hy_w3": w["hy_f_w3"][l],
        "hy_freq": w["hy_freq"][l][None, :], "hy_bias": w["hy_bias"][l][:, None, :],
        "w_out": w["w_out"][l].reshape(4, -1, D).astype(BF16),
        "router_split": _router_split(w["router_w"][l]),
        "w_gate": w["exp_w_gate"][l].astype(BF16), "w_up": w["exp_w_up"][l].astype(BF16),
        "w_down": w["exp_w_down"][l].astype(BF16),
    }


def _moe(xs, sh, sc, gate, lp, consts):
    B, n, _ = xs.shape
    cap = EC_CAPACITY * n // N_EXPERTS
    hcm, idx_t, gate_t = _route_call(xs, sh, sc, lp, consts)
    unfold = lambda a: a[:, :, :SLOT_RADIX].reshape(B, N_EXPERTS, SLOT_RADIX * SLOT_RADIX)[:, :, :cap]
    idx_flat = unfold(idx_t).reshape(-1)
    gts = jnp.pad(jnp.swapaxes(unfold(gate_t), 1, 2), ((0, 0), (0, 0), (0, LANES - N_EXPERTS)))
    moe_cm = _ffn_call(idx_flat, hcm, gts, lp, cap)
    return _moe_residual_call(xs, gate, moe_cm)


def kernel(x, c, ctx, c_ctx, norm1_g, norm2_g, w_mod, b_mod, w_in, w_out, pool_w, pool_scale, gqa_qnorm_g, gqa_knorm_g, hy_conv_w, hy_conv_b, hy_f_w1, hy_f_b1, hy_f_w2, hy_f_b2, hy_f_w3, hy_freq, hy_bias, mla_cq_g, mla_w_uq, mla_ckv_g, mla_w_ukv, mla_qnorm_g, mla_knorm_g, router_w, exp_w_gate, exp_w_up, exp_w_down):
    w = dict(norm1_g=norm1_g, norm2_g=norm2_g, w_in=w_in, w_out=w_out, pool_w=pool_w, pool_scale=pool_scale,
             gqa_qnorm_g=gqa_qnorm_g, gqa_knorm_g=gqa_knorm_g, hy_conv_w=hy_conv_w, hy_conv_b=hy_conv_b,
             hy_f_w1=hy_f_w1, hy_f_b1=hy_f_b1, hy_f_w2=hy_f_w2, hy_f_b2=hy_f_b2, hy_f_w3=hy_f_w3, hy_freq=hy_freq,
             hy_bias=hy_bias, mla_cq_g=mla_cq_g, mla_w_uq=mla_w_uq, mla_ckv_g=mla_ckv_g, mla_w_ukv=mla_w_ukv,
             mla_qnorm_g=mla_qnorm_g, mla_knorm_g=mla_knorm_g, router_w=router_w, exp_w_gate=exp_w_gate,
             exp_w_up=exp_w_up, exp_w_down=exp_w_down)
    B, n, D = x.shape
    nc = ctx.shape[1]
    depth = w_mod.shape[0]
    ch = hy_bias.shape[2]
    rows = -(-(B + 1) // SUBLANES) * SUBLANES
    cc = jnp.concatenate([c, c_ctx[None, :], jnp.zeros((rows - B - 1, D), F32)], axis=0)
    mod = _mod_call(cc, w_mod, b_mod)
    tabs_x = _rope_tables(n, True)
    tabs_c = _rope_tables(nc, False)
    hc_x = _hyena_consts(n, ch)
    hc_c = _hyena_consts(nc, ch)
    tri = _route_consts()
    xc = ctx
    for l in range(depth):
        last = l == depth - 1
        lp = _layer_params(l, w)
        m = mod[l]
        mx = [m[:B, i * D:(i + 1) * D][:, None, :] for i in range(6)]
        mc = [jnp.broadcast_to(m[B:B + 1, i * D:(i + 1) * D][:, None, :], (B, 1, D)) for i in range(6)]
        pc = _inproj_call(xc, mc[0], mc[1], lp, tabs_c, last)
        kc, vc, mkc, mvc = pc[:4]
        kx, vx, mkx, mvx, qx, mqx, poolx, hyx = _inproj_call(x, mx[0], mx[1], lp, tabs_x, False)
        o_gqa = _attn_call(qx, [kx, kc], [vx, vc], GQA_HEADS, GQA_KV_HEADS, GQA_HD, "attn_gqa")
        o_mla = _attn_call(mqx, [mkx, mkc], [mvx, mvc], MLA_HEADS, MLA_HEADS, MLA_V, "attn_mla")
        o_pool = _pool_call(poolx, lp)
        o_hy = _hyena_call(hyx, lp, hc_x, _hyfilt_call(lp, hc_x))
        x_new = _outproj_call(x, mx[2], [o_pool, o_gqa, o_hy, o_mla], lp)
        x_new = _moe(x_new, mx[3], mx[4], mx[5], lp, tri)
        if not last:
            qc, mqc, poolc, hyc = pc[4:]
            oc_gqa = _attn_call(qc, [kc], [vc], GQA_HEADS, GQA_KV_HEADS, GQA_HD, "attn_gqa_ctx")
            oc_mla = _attn_call(mqc, [mkc], [mvc], MLA_HEADS, MLA_HEADS, MLA_V, "attn_mla_ctx")
            oc_pool = _pool_call(poolc, lp)
            oc_hy = _hyena_call(hyc, lp, hc_c, _hyfilt_call(lp, hc_c))
            xc_new = _outproj_call(xc, mc[2], [oc_pool, oc_gqa, oc_hy, oc_mla], lp)
            xc = _moe(xc_new, mc[3], mc[4], mc[5], lp, tri)
        x = x_new
    return x
```

```python
import functools
import math

import numpy as np
import jax
import jax.numpy as jnp
from jax import lax
from jax.experimental import pallas as pl
from jax.experimental.pallas import tpu as pltpu

F32 = jnp.float32
BF16 = jnp.bfloat16
I32 = jnp.int32

GRID_W = 64
ROPE_THETA = 10000.0
EPS = 1e-6
POOL_WINDOWS = (2, 4, 8, 16)
GQA_HEADS = 4
GQA_KV_HEADS = 2
GQA_HD = 64
HY_BANDS = 8
HY_DECAY_TARGET = 1e-2
HY_FAST_PCT = 0.3
HY_SLOW_PCT = 1.5
MLA_HEADS = 4
MLA_NOPE = 64
MLA_ROPE = 32
MLA_V = 64
MLA_HD = MLA_NOPE + MLA_ROPE
N_EXPERTS = 16
EC_CAPACITY = 2

LANES = 128
SUBLANES = 8
MLA_PAD = LANES
VMEM_LIMIT = 56 * 2 ** 20
HIGHEST = lax.Precision.HIGHEST


def _cp(sem, vmem=VMEM_LIMIT):
    return pltpu.CompilerParams(dimension_semantics=sem, vmem_limit_bytes=vmem)


def _dot(a, b):
    return jnp.dot(a, b, preferred_element_type=F32)


def _dot_nt(a, b):
    return lax.dot_general(a, b, (((1,), (1,)), ((), ())), preferred_element_type=F32)


def _dot_split(x, m):
    hi = x.astype(BF16)
    lo = (x - hi.astype(F32)).astype(BF16)
    return _dot(hi, m) + _dot(lo, m)


def _rms_rows(x):
    return x * lax.rsqrt(jnp.mean(x * x, axis=-1, keepdims=True) + EPS)


def _rope(y, cos, sin_s, perm):
    pw = perm.shape[0]
    yb = y.astype(BF16)
    partner = jnp.concatenate([_dot(yb[:, j:j + pw], perm) for j in range(0, y.shape[1], pw)], axis=1)
    return y * cos + partner * sin_s


def _mod_kernel(c_ref, w_ref, b_ref, o_ref):
    c = c_ref[...]
    s = c * (1.0 / (1.0 + jnp.exp(-c)))
    o_ref[0] = jnp.dot(s, w_ref[0], precision=HIGHEST, preferred_element_type=F32) + b_ref[0]


def _mod_call(cc, w_mod, b_mod):
    L, D, N = w_mod.shape
    R = cc.shape[0]
    tn = 1536
    return pl.pallas_call(
        _mod_kernel,
        grid=(L, N // tn),
        in_specs=[
            pl.BlockSpec((R, D), lambda l, j: (0, 0)),
            pl.BlockSpec((1, D, tn), lambda l, j: (l, 0, j)),
            pl.BlockSpec((1, 1, tn), lambda l, j: (l, 0, j)),
        ],
        out_specs=pl.BlockSpec((1, R, tn), lambda l, j: (l, 0, j)),
        out_shape=jax.ShapeDtypeStruct((L, R, N), F32),
        compiler_params=_cp(("arbitrary", "arbitrary")),
        name="adaln_mod",
    )(cc, w_mod, b_mod.reshape(L, 1, N))


MOD_PARTS = 6


def _mod_vec(mod3, layer, rows, part, sample0, per_sample):
    return mod3, (layer * rows + sample0) * MOD_PARTS + part, MOD_PARTS if per_sample else 0


def _mod_spec(mv, batch_axis):
    arr, base, stride = mv
    return pl.BlockSpec((1, 1, arr.shape[2]), lambda *g: (base + stride * g[batch_axis], 0, 0))


KV_COLS = 512
ALL_COLS = 2048
INPROJ_ROWS = 512
INPROJ_SUB_ROWS = 256


def _inproj_kernel(kv_only, with_moe, x_ref, sh_ref, sc_ref, g1_ref, w_ref,
                   gk_ref, gq_ref, gckv_ref, wukv_ref, gmk_ref, gcq_ref, wuq_ref, gmq_ref,
                   seg_ref, pg_ref, pm_ref, cg_ref, sg_ref, cm_ref, sm_ref, *rest):
    if with_moe:
        moe_ref, g2_ref, outs = rest[0], rest[1], rest[2:]
        if not kv_only:
            outs, xo_ref = outs[:-1], outs[-1]
    else:
        outs = rest
    tm = x_ref.shape[1]
    sub = min(tm, INPROJ_SUB_ROWS)
    chunks = x_ref.shape[2] // LANES
    projs = []
    for r0 in range(0, tm, sub):
        x = x_ref[0, r0:r0 + sub, :]
        if with_moe:
            moe = jnp.concatenate([moe_ref[0, pl.ds(r0 * chunks + c, sub, stride=chunks), :] for c in range(chunks)],
                                  axis=1)
            x = x + g2_ref[0] * moe
            if not kv_only:
                xo_ref[0, r0:r0 + sub, :] = x
        h = _rms_rows(x) * g1_ref[...] * (1.0 + sc_ref[0]) + sh_ref[0]
        projs.append((slice(r0, r0 + sub), _dot(h.astype(BF16), w_ref[...])))
    for rows, p in projs:
        _inproj_heads(kv_only, rows, p, gk_ref, gq_ref, gckv_ref, wukv_ref, gmk_ref, gcq_ref, wuq_ref, gmq_ref,
                      seg_ref, pg_ref, pm_ref, cg_ref, sg_ref, cm_ref, sm_ref, *outs)


def _inproj_heads(kv_only, rows, p, gk_ref, gq_ref, gckv_ref, wukv_ref, gmk_ref, gcq_ref, wuq_ref, gmq_ref,
                  seg_ref, pg_ref, pm_ref, cg_ref, sg_ref, cm_ref, sm_ref, *outs):
    if kv_only:
        k_ref, v_ref, mk_ref, mv_ref = outs
    else:
        k_ref, v_ref, mk_ref, mv_ref, q_ref, mq_ref, pool_ref, hy_ref = outs

    def seg_norm(y, seg):
        ss = _dot((y * y).astype(BF16), seg)
        return y * lax.rsqrt(ss * (1.0 / GQA_HD) + EPS)

    def head_norm(y):
        parts = []
        for hh in range(MLA_HEADS):
            s = y[:, hh * MLA_PAD:(hh + 1) * MLA_PAD]
            ms = jnp.sum(s * s, axis=-1, keepdims=True) * (1.0 / MLA_HD)
            parts.append(s * lax.rsqrt(ms + EPS))
        return jnp.concatenate(parts, axis=1)

    def put_heads(ref, y, width):
        for hh in range(ref.shape[1]):
            ref[0, hh, rows, :] = y[:, hh * width:(hh + 1) * width].astype(ref.dtype)

    def put_values_t(ref, v, dv):
        vt = v.T
        dve = dv + ATTN_ONES_ROWS
        for hh in range(v.shape[1] // dv):
            ref[0, hh * dve:hh * dve + dv, rows] = vt[hh * dv:(hh + 1) * dv, :].astype(ref.dtype)
            ref[0, hh * dve + dv:(hh + 1) * dve, rows] = jnp.ones((ATTN_ONES_ROWS, v.shape[0]), ref.dtype)

    cg, sg = cg_ref[rows, :], sg_ref[rows, :]
    cm, sm, pm = cm_ref[rows, :], sm_ref[rows, :], pm_ref[...]
    kk = seg_norm(p[:, 0:128], seg_ref[0:128, 0:128]) * gk_ref[...]
    kk = _rope(kk, cg[:, 0:128], sg[:, 0:128], pg_ref[0:128, 0:128])
    put_heads(k_ref, kk, GQA_HD)
    put_values_t(v_ref, p[:, 128:256], GQA_HD)
    ckv = _rms_rows(p[:, 256:384]) * gckv_ref[...]
    kv = _dot(ckv.astype(BF16), wukv_ref[...])
    kr = p[:, 384:512]
    kraw = kv[:, 0:MLA_HEADS * MLA_PAD] + jnp.concatenate([kr] * MLA_HEADS, axis=1)
    mk = _rope(head_norm(kraw) * gmk_ref[...], cm, sm, pm)
    put_heads(mk_ref, mk, MLA_PAD)
    put_values_t(mv_ref, kv[:, MLA_HEADS * MLA_PAD:], MLA_V)
    if kv_only:
        return
    qq = seg_norm(p[:, 512:768], seg_ref[...]) * gq_ref[...]
    qq = _rope(qq, cg, sg, pg_ref[...])
    put_heads(q_ref, qq * (GQA_HD ** -0.5 * LOG2E), GQA_HD)
    cq = _rms_rows(p[:, 768:1024]) * gcq_ref[...]
    mq = _rope(head_norm(_dot(cq.astype(BF16), wuq_ref[...])) * gmq_ref[...], cm, sm, pm)
    put_heads(mq_ref, mq * (MLA_HD ** -0.5 * LOG2E), MLA_PAD)
    pool_ref[0, rows, :] = p[:, 1024:1280]
    hy_ref[0, rows, :] = p[:, 1280:2048]


def _inproj_call(xs, sh, sc, lp, tabs, kv_only, moe=None):
    B, n, D = xs.shape
    tm = min(n, INPROJ_ROWS)
    cols = KV_COLS if kv_only else ALL_COLS
    w = lp["w_in_r"][:, :cols]
    full = lambda a: pl.BlockSpec(a.shape, lambda i, b: (0,) * a.ndim)
    tab = lambda a: pl.BlockSpec((tm, a.shape[1]), lambda i, b: (i, 0))
    row = lambda width: pl.BlockSpec((1, tm, width), lambda i, b: (b, i, 0))
    params = [lp["g_k"], lp["g_q"], lp["g_ckv"], lp["w_ukv_r"], lp["g_mk"], lp["g_cq"], lp["w_uq_r"], lp["g_mq"],
              lp["seg"], lp["perm_g"], lp["perm_m"]]
    ones = ATTN_ONES_ROWS
    heads = lambda h, d: (jax.ShapeDtypeStruct((B, h, n, d), BF16), pl.BlockSpec((1, h, tm, d), lambda i, b: (b, 0, i, 0)))
    vals_t = lambda h, d: (jax.ShapeDtypeStruct((B, h * (d + ones), n), BF16),
                           pl.BlockSpec((1, h * (d + ones), tm), lambda i, b: (b, 0, i)))
    rows = lambda d: (jax.ShapeDtypeStruct((B, n, d), F32), row(d))
    outs = [heads(GQA_KV_HEADS, GQA_HD), vals_t(GQA_KV_HEADS, GQA_HD), heads(MLA_HEADS, MLA_PAD), vals_t(MLA_HEADS, MLA_V)]
    if not kv_only:
        outs += [heads(GQA_HEADS, GQA_HD), heads(MLA_HEADS, MLA_PAD), rows(256), rows(768)]
    extra_specs, extra_args = [], []
    if moe is not None:
        extra_specs = [pl.BlockSpec((1, tm * (D // LANES), LANES), lambda i, b: (b, i, 0)), _mod_spec(moe[1], 1)]
        extra_args = [moe[0], moe[1][0]]
        if not kv_only:
            outs.append(rows(D))
    return pl.pallas_call(
        functools.partial(_inproj_kernel, kv_only, moe is not None),
        grid=(n // tm, B),
        in_specs=[row(D), _mod_spec(sh, 1), _mod_spec(sc, 1), full(lp["norm1_g"]), full(w)]
        + [full(a) for a in params] + [tab(a) for a in tabs] + extra_specs,
        out_specs=[spec for _, spec in outs],
        out_shape=[shape for shape, _ in outs],
        compiler_params=_cp(("arbitrary", "arbitrary")),
        name="inproj_kv" if kv_only else "inproj",
    )(xs, sh[0], sc[0], lp["norm1_g"], w, *params, *tabs, *extra_args)


ATTN_KEY_BLOCK = 256
ATTN_Q_TILE = 512
ATTN_ONES_ROWS = 16
LOG2E = math.log2(math.e)


def _attn_kernel(heads, kv_heads, dv, nseg, q_ref, *refs):
    k_refs, vt_refs, o_ref = refs[:nseg], refs[nseg:2 * nseg], refs[2 * nseg]
    grp = heads // kv_heads
    kb = ATTN_KEY_BLOCK
    dve = dv + ATTN_ONES_ROWS
    blocks = [(i, r) for i in range(nseg) for r in range(0, k_refs[i].shape[2], kb)]

    qs = [q_ref[0, hh] for hh in range(heads)]
    run_max = [None] * heads
    acc = [None] * heads
    s_prev = [None] * heads
    for j in range(len(blocks) + 1):
        s_cur = [None] * heads
        for hh in range(heads):
            g = hh // grp
            if j < len(blocks):
                i, r = blocks[j]
                s_cur[hh] = _dot_nt(k_refs[i][0, g, r:r + kb, :], qs[hh])
            if j >= 1:
                i, r = blocks[j - 1]
                bm = jnp.max(s_prev[hh], axis=0, keepdims=True)
                m_new = bm if run_max[hh] is None else jnp.maximum(run_max[hh], bm)
                t = _dot(vt_refs[i][0, g * dve:(g + 1) * dve, r:r + kb], jnp.exp2(s_prev[hh] - m_new).astype(BF16))
                acc[hh] = t if acc[hh] is None else acc[hh] * jnp.exp2(run_max[hh] - m_new) + t
                run_max[hh] = m_new
        s_prev = s_cur
    o_ref[0] = jnp.concatenate([a[0:dv] / a[dv:dv + 1] for a in acc], axis=0).T.astype(o_ref.dtype)


def _attn_call(q, ks, vts, heads, kv_heads, dv, name):
    B, _, n, dk = q.shape
    tq = min(n, ATTN_Q_TILE)
    kspec = lambda a: pl.BlockSpec((1,) + a.shape[1:], lambda b, i: (b,) + (0,) * (a.ndim - 1))
    return pl.pallas_call(
        functools.partial(_attn_kernel, heads, kv_heads, dv, len(ks)),
        grid=(B, n // tq),
        in_specs=[pl.BlockSpec((1, heads, tq, dk), lambda b, i: (b, 0, i, 0))]
        + [kspec(a) for a in ks] + [kspec(a) for a in vts],
        out_specs=pl.BlockSpec((1, tq, heads * dv), lambda b, i: (b, i, 0)),
        out_shape=jax.ShapeDtypeStruct((B, n, heads * dv), BF16),
        compiler_params=_cp(("arbitrary", "arbitrary")),
        name=name,
    )(q, *ks, *vts)


POOL_HALO = 8
assert all(b == 2 * a for a, b in zip(POOL_WINDOWS, POOL_WINDOWS[1:])) and POOL_WINDOWS[0] == 2
assert POOL_WINDOWS[-1] // 2 <= POOL_HALO


def _pool_kernel(n, u_ref, cnt_ref, wbd_ref, scale_ref, o_ref, ue_ref):
    w = u_ref.shape[2]
    gc = w // len(POOL_WINDOWS)
    ne = n + 2 * POOL_HALO
    u = u_ref[0]
    ue_ref[0:POOL_HALO, :] = jnp.zeros((POOL_HALO, w), F32)
    ue_ref[POOL_HALO:POOL_HALO + n, :] = u
    ue_ref[POOL_HALO + n:ne, :] = jnp.zeros((POOL_HALO, w), F32)
    ue = ue_ref[...]
    down = lambda x, k: pltpu.roll(x, k, axis=0)
    up = lambda x, k: pltpu.roll(x, ne - k, axis=0)
    sums = [down(ue, 1) + ue]
    for win in POOL_WINDOWS[:-1]:
        sums.append(down(sums[-1], win // 2) + up(sums[-1], win // 2))
    lane = lax.broadcasted_iota(I32, (n, w), 1)
    total = sums[-1][POOL_HALO:POOL_HALO + n]
    for gi in range(len(POOL_WINDOWS) - 2, -1, -1):
        total = jnp.where(lane < (gi + 1) * gc, sums[gi][POOL_HALO:POOL_HALO + n], total)
    d = total / cnt_ref[...] - u
    o_ref[0] = (_dot(d.astype(BF16), wbd_ref[...]) * scale_ref[...]).astype(o_ref.dtype)


def _pool_counts(n, width):
    t = np.arange(n)
    gc = width // len(POOL_WINDOWS)
    cols = []
    for win in POOL_WINDOWS:
        left = win // 2
        right = win - 1 - left
        cnt = np.minimum(t + right, n - 1) - np.maximum(t - left, 0) + 1
        cols.append(np.repeat(cnt[:, None], gc, axis=1))
    return jnp.asarray(np.concatenate(cols, axis=1), dtype=F32)


def _pool_call(u, lp):
    B, n, W = u.shape
    return pl.pallas_call(
        functools.partial(_pool_kernel, n),
        grid=(B,),
        in_specs=[pl.BlockSpec((1, n, W), lambda b: (b, 0, 0)),
                  pl.BlockSpec((n, W), lambda b: (0, 0)),
                  pl.BlockSpec((W, W), lambda b: (0, 0)),
                  pl.BlockSpec((1, W), lambda b: (0, 0))],
        out_specs=pl.BlockSpec((1, n, W), lambda b: (b, 0, 0)),
        out_shape=jax.ShapeDtypeStruct((B, n, W), BF16),
        scratch_shapes=[pltpu.VMEM((n + 2 * POOL_HALO, W), F32)],
        compiler_params=_cp(("arbitrary",)),
        name="pool",
    )(u, _pool_counts(n, W), lp["pool_bd"], lp["pool_scale"])


HY_BLOCK = 512
HY_SPEC_DTYPE = F32


def _dot3(a_hi, a_lo, x):
    x_hi = x.astype(BF16)
    x_lo = (x - x_hi.astype(F32)).astype(BF16)
    return _dot(a_hi, x_hi) + _dot(a_hi, x_lo) + _dot(a_lo, x_hi)


def _hyfilt_kernel(n, t, feats_ref, w1_ref, b1_ref, w2_ref, b2_ref, w3_ref, fr_ref, decay_ref,
                   chi_ref, clo_ref, shi_ref, slo_ref, hre_ref, him_ref, hn_ref, tc_ref, ts_ref, edge_ref):
    di = pl.program_id(0)
    nb = n // t
    hc2 = w3_ref.shape[1] // 2

    @pl.when(di == 0)
    def _():
        fr = fr_ref[...]
        hid = jnp.sin(fr * (jnp.dot(feats_ref[...], w1_ref[...], precision=HIGHEST, preferred_element_type=F32)
                            + b1_ref[...]))
        hid = jnp.sin(fr * (jnp.dot(hid, w2_ref[...], precision=HIGHEST, preferred_element_type=F32) + b2_ref[...]))
        h = jnp.dot(hid, w3_ref[...], precision=HIGHEST, preferred_element_type=F32)
        dec = decay_ref[...]
        dec2 = jnp.concatenate([dec, dec], axis=1)
        row = lax.broadcasted_iota(I32, (n, 1), 0)
        fwd = h[:, 0:hc2] * dec2
        bwd = jnp.where(row == 0, 0.0, h[:, hc2:2 * hc2] * dec2)
        scl = lax.rsqrt(jnp.sum(fwd * fwd + bwd * bwd, axis=0, keepdims=True) + EPS)
        alt = jnp.where(lax.broadcasted_iota(I32, (t, 1), 0) % 2 == 0, 1.0, -1.0)
        for fam, filt in enumerate((fwd * scl, bwd * scl)):
            for j in range(nb):
                x = filt[j * t:(j + 1) * t]
                tc_ref[fam, j] = _dot3(chi_ref[...], clo_ref[...], x)
                ts_ref[fam, j] = _dot3(shi_ref[...], slo_ref[...], x)
                r = 2 * (fam * nb + j)
                edge_ref[r:r + 1, :] = x[0:1, :]
                edge_ref[r + 1:r + 2, :] = jnp.sum(x * alt, axis=0, keepdims=True)

    k = lax.broadcasted_iota(I32, (t, 1), 0)
    wk = jnp.where(k == 0, 1.0 / (2 * t), 2.0 / (2 * t))
    sig = jnp.where(k % 2 == 0, 1.0, -1.0)
    first = lambda fam, j: edge_ref[2 * (fam * nb + j):2 * (fam * nb + j) + 1, :]
    altsum = lambda fam, j: edge_ref[2 * (fam * nb + j) + 1:2 * (fam * nb + j) + 2, :]
    for dd in range(2 * nb - 1):
        d = dd - (nb - 1)

        def put(ref, val):
            for o in range(2):
                ref[o, 0] = val[:, o * (hc2 // 2):(o + 1) * (hc2 // 2)].astype(ref.dtype)

        @pl.when(di == dd)
        def _(d=d):
            if d == 0:
                put(hre_ref, wk * (tc_ref[0, 0] + tc_ref[1, 0]))
                put(him_ref, wk * (ts_ref[1, 0] - ts_ref[0, 0]))
                put(hn_ref, (altsum(0, 0) + altsum(1, 0)) * (1.0 / (2 * t)))
            else:
                fam, e, sgn = (0, d, -1.0) if d > 0 else (1, -d, 1.0)
                put(hre_ref, wk * (sig * (tc_ref[fam, e - 1] - first(fam, e - 1)) + tc_ref[fam, e]))
                put(him_ref, (sgn * wk) * (sig * ts_ref[fam, e - 1] + ts_ref[fam, e]))
                put(hn_ref, (altsum(fam, e - 1) - first(fam, e - 1) + altsum(fam, e)) * (1.0 / (2 * t)))


def _hyfilt_call(lp, hc):
    n, t = hc["n"], hc["t"]
    nb = n // t
    nd = 2 * nb - 1
    hc2 = lp["hy_w3"].shape[1] // 2
    ch = hc2 // 2
    full = lambda a: pl.BlockSpec(a.shape, lambda i: (0,) * a.ndim)
    args = [hc["feats"], lp["hy_w1"], lp["hy_b1"], lp["hy_w2"], lp["hy_b2"], lp["hy_w3"], lp["hy_freq"], hc["decay"],
            hc["c_hi"], hc["c_lo"], hc["s_hi"], hc["s_lo"]]
    return pl.pallas_call(
        functools.partial(_hyfilt_kernel, n, t),
        grid=(nd,),
        in_specs=[full(a) for a in args],
        out_specs=[pl.BlockSpec((2, 1, t, ch), lambda i: (0, i, 0, 0)), pl.BlockSpec((2, 1, t, ch), lambda i: (0, i, 0, 0)),
                   pl.BlockSpec((2, 1, 1, ch), lambda i: (0, i, 0, 0))],
        out_shape=[jax.ShapeDtypeStruct((2, nd, t, ch), HY_SPEC_DTYPE), jax.ShapeDtypeStruct((2, nd, t, ch), HY_SPEC_DTYPE),
                   jax.ShapeDtypeStruct((2, nd, 1, ch), F32)],
        scratch_shapes=[pltpu.VMEM((2, nb, t, hc2), F32), pltpu.VMEM((2, nb, t, hc2), F32),
                        pltpu.VMEM((4 * nb, hc2), F32)],
        compiler_params=_cp(("arbitrary",)),
        name="hyena_filter",
    )(*args)


HY_FREQ_ROWS = 32


def _hyena_kernel(n, t, u_ref, cw_ref, cb_ref, c_ref, s_ref, hre_ref, him_ref, hn_ref, bias_ref, o_ref,
                  uc_ref, z_ref, a_ref, b_ref, p_ref, q_ref):
    ch = o_ref.shape[2]
    nb = n // t
    row = lax.broadcasted_iota(I32, (t, 1), 0)
    alt = jnp.where(row % 2 == 0, 1.0, -1.0)
    sub = lax.broadcasted_iota(I32, (SUBLANES, 1), 0)
    for g in range(3):
        cols = slice(g * ch, (g + 1) * ch)
        for j in range(nb):
            r0 = j * t
            u = u_ref[0, r0:r0 + t, cols]
            prev = jnp.zeros((1, ch), F32) if j == 0 else u_ref[0, r0 - 1:r0, cols]
            nxt = jnp.zeros((1, ch), F32) if j == nb - 1 else u_ref[0, r0 + t:r0 + t + 1, cols]
            up = pltpu.roll(u, 1, axis=0)
            un = pltpu.roll(u, t - 1, axis=0)
            up = jnp.concatenate([jnp.where(sub == 0, prev, up[0:SUBLANES]), up[SUBLANES:]], axis=0)
            un = jnp.concatenate([un[:t - SUBLANES], jnp.where(sub == SUBLANES - 1, nxt, un[t - SUBLANES:])], axis=0)
            uc_ref[g, r0:r0 + t, :] = (up * cw_ref[0:1, cols] + u * cw_ref[1:2, cols] + un * cw_ref[2:3, cols]
                                       + cb_ref[:, cols])

    z_ref[...] = uc_ref[0]

    def long_conv(o, carry):
        nyq = []
        for j in range(nb):
            z = z_ref[j * t:(j + 1) * t, :]
            zb = z.astype(BF16)
            a_ref[j] = _dot(c_ref[...], zb).astype(a_ref.dtype)
            b_ref[j] = _dot(s_ref[...], zb).astype(b_ref.dtype)
            nyq.append(jnp.sum(z * alt, axis=0, keepdims=True))
        for r0 in range(0, t, HY_FREQ_ROWS):
            rows = slice(r0, r0 + HY_FREQ_ROWS)
            ab = [(a_ref[j, rows, :], b_ref[j, rows, :]) for j in range(nb)]
            for i in range(nb):
                p = q = None
                for j, (a, b) in enumerate(ab):
                    d = i - j + nb - 1
                    hre, him = hre_ref[o, d, rows, :], him_ref[o, d, rows, :]
                    tp = a * hre + b * him
                    tq = b * hre - a * him
                    p = tp if p is None else p + tp
                    q = tq if q is None else q + tq
                p_ref[i, rows, :] = p.astype(BF16)
                q_ref[i, rows, :] = q.astype(BF16)
        for i in range(nb):
            yn = functools.reduce(jnp.add, [nyq[j] * hn_ref[o, i - j + nb - 1] for j in range(nb)])
            y = _dot(c_ref[...], p_ref[i]) + _dot(s_ref[...], q_ref[i])
            blk = slice(i * t, (i + 1) * t)
            z_ref[blk, :] = uc_ref[o + 1, blk, :] * (y + alt * yn + z_ref[blk, :] * bias_ref[o])
        return carry

    lax.fori_loop(0, 2, long_conv, 0)
    o_ref[0] = z_ref[...].astype(o_ref.dtype)


def _hyena_call(u, lp, hc, filt):
    B, n, W = u.shape
    ch = W // 3
    t = hc["t"]
    hre, him, hn = filt
    once = lambda a: pl.BlockSpec(a.shape, lambda b: (0,) * a.ndim, pipeline_mode=pl.Buffered(1))
    return pl.pallas_call(
        functools.partial(_hyena_kernel, n, t),
        scratch_shapes=[pltpu.VMEM((3, n, ch), F32), pltpu.VMEM((n, ch), F32),
                        pltpu.VMEM((n // t, t, ch), HY_SPEC_DTYPE), pltpu.VMEM((n // t, t, ch), HY_SPEC_DTYPE),
                        pltpu.VMEM((n // t, t, ch), BF16), pltpu.VMEM((n // t, t, ch), BF16)],
        grid=(B,),
        in_specs=[pl.BlockSpec((1, n, W), lambda b: (b, 0, 0)),
                  once(lp["hy_conv_w"]), once(lp["hy_conv_b"]), once(hc["c16"]), once(hc["s16"]),
                  once(hre), once(him), once(hn), once(lp["hy_bias"])],
        out_specs=pl.BlockSpec((1, n, ch), lambda b: (b, 0, 0)),
        out_shape=jax.ShapeDtypeStruct((B, n, ch), BF16),
        compiler_params=_cp(("arbitrary",)),
        name="hyena",
    )(u, lp["hy_conv_w"], lp["hy_conv_b"], hc["c16"], hc["s16"], hre, him, hn, lp["hy_bias"])


def _outproj_kernel(x_ref, g_ref, a_ref, b_ref, c_ref, d_ref, w_ref, o_ref):
    mixed = (_dot(a_ref[0], w_ref[0]) + _dot(b_ref[0], w_ref[1])
             + _dot(c_ref[0], w_ref[2]) + _dot(d_ref[0], w_ref[3]))
    o_ref[0] = x_ref[0] + g_ref[0] * mixed


def _outproj_call(xs, gate, parts, lp):
    B, n, D = xs.shape
    tm = min(n, 512)
    gw = parts[0].shape[2]
    return pl.pallas_call(
        _outproj_kernel,
        grid=(B, n // tm),
        in_specs=[pl.BlockSpec((1, tm, D), lambda b, i: (b, i, 0)), _mod_spec(gate, 0)]
        + [pl.BlockSpec((1, tm, gw), lambda b, i: (b, i, 0))] * 4
        + [pl.BlockSpec(lp["w_out"].shape, lambda b, i: (0, 0, 0))],
        out_specs=pl.BlockSpec((1, tm, D), lambda b, i: (b, i, 0)),
        out_shape=jax.ShapeDtypeStruct((B, n, D), F32),
        compiler_params=_cp(("arbitrary", "arbitrary")),
        name="outproj",
    )(xs, gate[0], *parts, lp["w_out"])


PREFIX_BLOCK = 256


def _prefix_lanes(m, tri):
    e, n = m.shape
    nb = n // PREFIX_BLOCK
    stk = jnp.concatenate([m[:, j * PREFIX_BLOCK:(j + 1) * PREFIX_BLOCK] for j in range(nb)], axis=0)
    pre = _dot(stk.astype(BF16), tri)
    outs = []
    carry = jnp.zeros((e, 1), F32)
    for j in range(nb):
        pj = pre[j * e:(j + 1) * e]
        outs.append(pj + carry)
        carry = carry + pj[:, PREFIX_BLOCK - 1:PREFIX_BLOCK]
    return jnp.concatenate(outs, axis=1)


SLOT_RADIX = 16
RADIX_BITS = 4


def _split3(x):
    a = x.astype(BF16).astype(F32)
    r = x - a
    b = r.astype(BF16).astype(F32)
    c = (r - b).astype(BF16).astype(F32)
    return a, b, c


def _digit_onehots(v, weights=None):
    e_n = v.shape[0]
    hi_d = jnp.floor(v * (1.0 / SLOT_RADIX))
    lo_d = v - SLOT_RADIX * hi_d
    dig = lax.broadcasted_iota(I32, (SLOT_RADIX, v.shape[1]), 0).astype(F32)
    his = [hi_d[e:e + 1, :] == dig for e in range(e_n)]
    lo = jnp.concatenate([jnp.where(lo_d[e:e + 1, :] == dig, 1.0, 0.0) for e in range(e_n)], axis=0)
    if weights is None:
        hi = jnp.concatenate([jnp.where(m, 1.0, 0.0) for m in his], axis=0)
    else:
        hi = jnp.concatenate([jnp.where(m, w[e:e + 1, :], 0.0) for w in weights for e, m in enumerate(his)], axis=0)
    return hi.astype(BF16), lo.astype(BF16)


def _route_kernel(n, cap, x_ref, sh_ref, sc_ref, g_ref, wr_ref, tri_ref, fold_ref, pre_ref, low_ref,
                  hcm_ref, idx_ref, gate_ref):
    x = x_ref[0]
    h = _rms_rows(x) * g_ref[...] * (1.0 + sc_ref[0]) + sh_ref[0]
    for c in range(h.shape[1] // LANES):
        hcm_ref[0, pl.ds(c, n, stride=SUBLANES), :] = h[:, c * LANES:(c + 1) * LANES]
    h_hi = h.astype(BF16)
    h_lo = (h - h_hi.astype(F32)).astype(BF16)
    rt = _dot(jnp.concatenate([h_hi, h_lo], axis=1), wr_ref[...]).T
    lt = rt[0:N_EXPERTS, :] + rt[N_EXPERTS:2 * N_EXPERTS, :]
    ex = jnp.exp(lt - jnp.max(lt, axis=0, keepdims=True))
    aff = ex / jnp.sum(ex, axis=0, keepdims=True)
    bits = pltpu.bitcast(aff, I32)
    thr = jnp.zeros((N_EXPERTS, 1), I32)
    for shift in range(32 - RADIX_BITS, -1, -RADIX_BITS):
        top = (1 << (31 - shift)) if shift + RADIX_BITS > 31 else (1 << RADIX_BITS)
        digit = jnp.zeros((N_EXPERTS, 1), I32)
        for c in range(1, top):
            cnt = jnp.sum(jnp.where(bits >= (thr | (c << shift)), 1.0, 0.0), axis=1, keepdims=True)
            digit = digit + jnp.where(cnt >= cap, 1, 0)
        thr = thr | (digit << shift)
    gt = jnp.where(bits > thr, 1.0, 0.0)
    eq = jnp.where(bits == thr, 1.0, 0.0)
    need = cap - jnp.sum(gt, axis=1, keepdims=True)
    tri = tri_ref[...]
    sel = gt + eq * jnp.where(_prefix_lanes(eq, tri) <= need, 1.0, 0.0)
    incl = _prefix_lanes(sel, tri)
    rows = N_EXPERTS * SLOT_RADIX
    shift = SLOT_RADIX.bit_length() - 1
    same = ((lax.broadcasted_iota(I32, (rows, rows), 0) >> shift)
            == (lax.broadcasted_iota(I32, (rows, rows), 1) >> shift))
    fold = fold_ref[...]
    a_hi, a_lo = _digit_onehots(incl)
    hist = jnp.where(same, _dot_nt(a_hi, a_lo), 0.0)
    hist = _dot_split(hist, fold)
    pre = _dot_split(hist, pre_ref[...])
    tot = jnp.broadcast_to(pre[:, SLOT_RADIX - 1:SLOT_RADIX], pre.shape)
    t_hi = tot.astype(BF16)
    t_lo = (tot - t_hi.astype(F32)).astype(BF16)
    idx_ref[0] = (pre + _dot(low_ref[...], t_hi) + _dot(low_ref[...], t_lo)).astype(I32)
    g_hi, g_lo = _digit_onehots(incl - 1.0, _split3(sel * aff))
    gsum = _dot_nt(g_hi, g_lo)
    gsum = jnp.where(same, gsum[0:rows] + gsum[rows:2 * rows] + gsum[2 * rows:3 * rows], 0.0)
    gate_ref[0] = functools.reduce(jnp.add, [_dot(part.astype(BF16), fold) for part in _split3(gsum)])


def _route_consts():
    rows = N_EXPERTS * SLOT_RADIX
    i = np.arange(rows)
    j = np.arange(LANES)
    fold = (i[:, None] % SLOT_RADIX == j[None, :]).astype(np.float32)
    pre = ((j[:, None] <= j[None, :]) & (j[None, :] < SLOT_RADIX)).astype(np.float32)
    low = ((i[:, None] // SLOT_RADIX == i[None, :] // SLOT_RADIX) & (i[None, :] < i[:, None])).astype(np.float32)
    tri = np.triu(np.ones((PREFIX_BLOCK, PREFIX_BLOCK), np.float32))
    return tuple(jnp.asarray(a, dtype=BF16) for a in (tri, fold, pre, low))


def _route_call(xs, sh, sc, lp, consts):
    B, n, D = xs.shape
    cap = EC_CAPACITY * n // N_EXPERTS
    assert cap <= SLOT_RADIX * SLOT_RADIX
    rows = n * (D // LANES)
    srows = N_EXPERTS * SLOT_RADIX
    full = lambda a: pl.BlockSpec(a.shape, lambda b: (0,) * a.ndim)
    return pl.pallas_call(
        functools.partial(_route_kernel, n, cap),
        grid=(B,),
        in_specs=[pl.BlockSpec((1, n, D), lambda b: (b, 0, 0)), _mod_spec(sh, 0), _mod_spec(sc, 0),
                  full(lp["norm2_g"]), full(lp["router_split"])] + [full(a) for a in consts],
        out_specs=[pl.BlockSpec((1, rows, LANES), lambda b: (b, 0, 0)),
                   pl.BlockSpec((1, srows, LANES), lambda b: (b, 0, 0)),
                   pl.BlockSpec((1, srows, LANES), lambda b: (b, 0, 0))],
        out_shape=[jax.ShapeDtypeStruct((B, rows, LANES), F32),
                   jax.ShapeDtypeStruct((B, srows, LANES), I32),
                   jax.ShapeDtypeStruct((B, srows, LANES), F32)],
        compiler_params=_cp(("arbitrary",)),
        name="route",
    )(xs, sh[0], sc[0], lp["norm2_g"], lp["router_split"], *consts)


GATHER_UNROLL = 8


FFN_ROWS = 256


def _ffn_kernel(cap, bt, idx_ref, h_ref, gate_ref, wg_ref, wu_ref, wd_ref, o_ref, xga_ref, xgb_ref, ya_ref, yb_ref):
    b0 = pl.program_id(0) * bt
    k = pl.program_id(1)
    steps = pl.num_programs(1)
    chunks = wg_ref.shape[1] // LANES
    m = bt * cap
    tile = lambda r: pl.ds(pl.multiple_of(r * SUBLANES, SUBLANES), SUBLANES)
    e0 = 2 * k
    e1 = e0 + 1
    wrap = lambda e: jnp.where(e < 0, e + N_EXPERTS, jnp.where(e >= N_EXPERTS, e - N_EXPERTS, e))

    def gather(e, xg_ref):
        for bi in range(bt):
            base = ((b0 + bi) * N_EXPERTS + e) * cap
            for r in range(cap):
                xg_ref[tile(bi * cap + r), :] = h_ref[bi, tile(idx_ref[base + r]), :]

    def scatter(e, y_ref):
        for bi in range(bt):
            base = ((b0 + bi) * N_EXPERTS + e) * cap
            for r0 in range(0, cap, GATHER_UNROLL):
                upd = []
                for r in range(r0, r0 + GATHER_UNROLL):
                    dst = tile(idx_ref[base + r])
                    upd.append((dst, o_ref[bi, dst, :] + y_ref[tile(bi * cap + r), :]))
                for dst, v in upd:
                    o_ref[bi, dst, :] = v

    def ffn(e, w, xg_ref, y_ref):
        xt = jnp.concatenate([xg_ref[pl.ds(c, m, stride=SUBLANES), :] for c in range(chunks)], axis=1).astype(BF16)
        a = _dot(xt, wg_ref[w])
        u = _dot(xt, wu_ref[w])
        hmid = (a * (1.0 / (1.0 + jnp.exp(-a))) * u).astype(BF16)
        lane = lax.broadcasted_iota(I32, (m, LANES), 1)
        gcol = jnp.sum(jnp.where(lane == e, gate_ref[...].reshape(m, LANES), 0.0), axis=1, keepdims=True)
        y = _dot(hmid, wd_ref[w]) * gcol
        for c in range(chunks):
            y_ref[pl.ds(c, m, stride=SUBLANES), :] = y[:, c * LANES:(c + 1) * LANES]

    @pl.when(k == 0)
    def _():
        o_ref[...] = jnp.zeros_like(o_ref)
        yb_ref[...] = jnp.zeros_like(yb_ref)
        gather(e0, xga_ref)

    ffn(e0, 0, xga_ref, ya_ref)
    gather(e1, xgb_ref)
    scatter(wrap(e0 - 1), yb_ref)
    ffn(e1, 1, xgb_ref, yb_ref)
    gather(wrap(e1 + 1), xga_ref)
    scatter(e0, ya_ref)

    @pl.when(k == steps - 1)
    def _():
        scatter(e1, yb_ref)


def _ffn_call(idx_flat, hcm, gate, lp, cap):
    B, rows, _ = hcm.shape
    D = lp["w_gate"].shape[1]
    ff = lp["w_gate"].shape[2]
    bt = min(B, max(1, FFN_ROWS // cap))
    m = bt * cap
    return pl.pallas_call(
        functools.partial(_ffn_kernel, cap, bt),
        grid=(B // bt, N_EXPERTS // 2),
        in_specs=[pl.BlockSpec(memory_space=pltpu.SMEM),
                  pl.BlockSpec((bt, rows, LANES), lambda b, k: (b, 0, 0)),
                  pl.BlockSpec((bt, cap, LANES), lambda b, k: (b, 0, 0)),
                  pl.BlockSpec((2, D, ff), lambda b, k: (k, 0, 0)),
                  pl.BlockSpec((2, D, ff), lambda b, k: (k, 0, 0)),
                  pl.BlockSpec((2, ff, D), lambda b, k: (k, 0, 0))],
        out_specs=pl.BlockSpec((bt, rows, LANES), lambda b, k: (b, 0, 0)),
        out_shape=jax.ShapeDtypeStruct((B, rows, LANES), F32),
        scratch_shapes=[pltpu.VMEM((m * SUBLANES, LANES), F32)] * 4,
        compiler_params=_cp(("arbitrary", "arbitrary")),
        name="expert_ffn",
    )(idx_flat, hcm, gate, lp["w_gate"], lp["w_up"], lp["w_down"])


def _moe_residual_kernel(x_ref, g_ref, m_ref, o_ref):
    tm = x_ref.shape[1]
    chunks = x_ref.shape[2] // LANES
    moe = jnp.concatenate([m_ref[0, pl.ds(c, tm, stride=SUBLANES), :] for c in range(chunks)], axis=1)
    o_ref[0] = x_ref[0] + g_ref[0] * moe


def _moe_residual_call(xs, gate, moe_cm):
    B, n, D = xs.shape
    tm = min(n, 512)
    return pl.pallas_call(
        _moe_residual_kernel,
        grid=(B, n // tm),
        in_specs=[pl.BlockSpec((1, tm, D), lambda b, i: (b, i, 0)), _mod_spec(gate, 0),
                  pl.BlockSpec((1, tm * (D // LANES), LANES), lambda b, i: (b, i, 0))],
        out_specs=pl.BlockSpec((1, tm, D), lambda b, i: (b, i, 0)),
        out_shape=jax.ShapeDtypeStruct((B, n, D), F32),
        compiler_params=_cp(("arbitrary", "arbitrary")),
        name="moe_residual",
    )(xs, gate[0], moe_cm)


def _axial_tables(n, head_pat, half, reps, roped):
    width = reps * len(head_pat)
    if not roped:
        return jnp.ones((n, width), F32), jnp.zeros((n, width), F32)
    pats = list(head_pat) * reps
    on_col = np.array([p is not None and p[0] == 1 for p in pats])
    freq = np.array([0 if p is None else p[1] for p in pats])
    sign = np.array([0.0 if p is None else (-1.0 if p[2] else 1.0) for p in pats], np.float32)
    t = jnp.arange(n)
    row = jnp.floor_divide(t, GRID_W).astype(F32)[:, None]
    col = jnp.remainder(t, GRID_W).astype(F32)[:, None]
    inv = (ROPE_THETA ** (-jnp.arange(half, dtype=F32) / half))[freq][None, :]
    ang = jnp.where(on_col[None, :], col, row) * inv
    return jnp.where(sign[None, :] == 0.0, 1.0, jnp.cos(ang)), sign[None, :] * jnp.sin(ang)


def _partner_matrix(head_pat, half, reps):
    hw = len(head_pat)
    p = np.zeros((hw * reps, hw * reps), np.float32)
    for r in range(reps):
        for j, pat in enumerate(head_pat):
            if pat is not None:
                p[r * hw + (j + half if pat[2] else j - half), r * hw + j] = 1.0
    return jnp.asarray(p, dtype=BF16)


def _rope_patterns():
    gh = GQA_HD // 4
    gpat = [(j // (2 * gh), j % gh, (j % (2 * gh)) < gh) for j in range(GQA_HD)]
    mh = MLA_ROPE // 4
    mpat = [None] * MLA_NOPE + [(j // (2 * mh), j % mh, (j % (2 * mh)) < mh) for j in range(MLA_ROPE)]
    mpat += [None] * (MLA_PAD - MLA_HD)
    return gpat, gh, mpat, mh


def _rope_tables(n, roped):
    gpat, gh, mpat, mh = _rope_patterns()
    return _axial_tables(n, gpat, gh, GQA_HEADS, roped) + _axial_tables(n, mpat, mh, MLA_HEADS, roped)


def _rope_partners():
    gpat, gh, mpat, mh = _rope_patterns()
    return _partner_matrix(gpat, gh, GQA_HEADS), _partner_matrix(mpat, mh, 2)


def _hyena_consts(n, ch):
    f32 = F32
    tb = min(n, HY_BLOCK)
    k = jnp.arange(tb, dtype=I32)
    ang = jnp.remainder(k[:, None] * k[None, :], 2 * tb).astype(f32) * (math.pi / tb)
    c32, s32 = jnp.cos(ang), jnp.sin(ang)
    c_hi, s_hi = c32.astype(BF16), s32.astype(BF16)
    c_lo, s_lo = (c32 - c_hi.astype(f32)).astype(BF16), (s32 - s_hi.astype(f32)).astype(BF16)
    t = jnp.linspace(0.0, 1.0, n, dtype=f32)[:, None]
    lag = jnp.arange(n, dtype=f32)[:, None]
    bands = jnp.linspace(1e-4, HY_BANDS - 1, HY_BANDS, dtype=f32)[None, :]
    a = (2.0 * math.pi / n) * lag * bands
    feats = jnp.concatenate([t, jnp.cos(a), -jnp.sin(a)], axis=-1)
    feats = jnp.pad(feats, ((0, 0), (0, LANES - feats.shape[1])))
    deltas = jnp.abs(jnp.linspace(math.log(HY_DECAY_TARGET) / HY_SLOW_PCT,
                                  math.log(HY_DECAY_TARGET) / HY_FAST_PCT, ch, dtype=f32))
    decay = jnp.exp(-t * deltas[None, :])
    return {"n": n, "t": tb, "c16": c_hi, "s16": s_hi, "c_hi": c_hi, "c_lo": c_lo, "s_hi": s_hi, "s_lo": s_lo,
            "feats": feats, "decay": decay}


def _seg_matrix(width, seg):
    i = np.arange(width)
    return jnp.asarray((i[:, None] // seg == i[None, :] // seg).astype(np.float32), dtype=BF16)


def _router_split(rw):
    d = rw.shape[0]
    hi = rw.astype(BF16)
    lo = (rw - hi.astype(F32)).astype(BF16)
    top = jnp.concatenate([hi, lo, jnp.zeros((d, LANES - 2 * N_EXPERTS), BF16)], axis=1)
    bot = jnp.concatenate([hi, jnp.zeros((d, LANES - N_EXPERTS), BF16)], axis=1)
    return jnp.concatenate([top, bot], axis=0)


def _layer_params(l, w):
    D = w["w_in"].shape[1]
    wi = w["w_in"][l]
    z = lambda c: jnp.zeros((D, c), F32)
    w_in_r = jnp.concatenate(
        [wi[:, 0:384], z(MLA_NOPE), wi[:, 384:416], z(MLA_PAD - MLA_HD), wi[:, 416:]], axis=1).astype(BF16)
    ukv = w["mla_w_ukv"][l].reshape(-1, MLA_HEADS, MLA_NOPE + MLA_V)
    kpart = jnp.pad(ukv[:, :, :MLA_NOPE], ((0, 0), (0, 0), (0, MLA_PAD - MLA_NOPE)))
    w_ukv_r = jnp.concatenate([kpart.reshape(ukv.shape[0], -1), ukv[:, :, MLA_NOPE:].reshape(ukv.shape[0], -1)],
                              axis=1).astype(BF16)
    uq = w["mla_w_uq"][l].reshape(-1, MLA_HEADS, MLA_HD)
    w_uq_r = jnp.pad(uq, ((0, 0), (0, 0), (0, MLA_PAD - MLA_HD))).reshape(uq.shape[0], -1).astype(BF16)
    padg = lambda g: jnp.tile(jnp.pad(g, (0, MLA_PAD - MLA_HD)), MLA_HEADS)[None, :]
    pw = w["pool_w"][l]
    gc = pw.shape[1]
    pool_bd = jnp.zeros((len(POOL_WINDOWS) * gc,) * 2, F32)
    for gi in range(len(POOL_WINDOWS)):
        pool_bd = pool_bd.at[gi * gc:(gi + 1) * gc, gi * gc:(gi + 1) * gc].set(pw[gi])
    nf = w["hy_f_w1"].shape[1]
    perm_g, perm_m = _rope_partners()
    return {
        "perm_g": perm_g, "perm_m": perm_m,
        "norm1_g": w["norm1_g"][l][None, :], "norm2_g": w["norm2_g"][l][None, :],
        "w_in_r": w_in_r, "w_ukv_r": w_ukv_r, "w_uq_r": w_uq_r,
        "g_k": jnp.tile(w["gqa_knorm_g"][l], GQA_KV_HEADS)[None, :],
        "g_q": jnp.tile(w["gqa_qnorm_g"][l], GQA_HEADS)[None, :],
        "g_ckv": w["mla_ckv_g"][l][None, :], "g_cq": w["mla_cq_g"][l][None, :],
        "g_mk": padg(w["mla_knorm_g"][l]), "g_mq": padg(w["mla_qnorm_g"][l]),
        "seg": _seg_matrix(GQA_HEADS * GQA_HD, GQA_HD),
        "pool_bd": pool_bd.astype(BF16), "pool_scale": w["pool_scale"][l][None, :],
        "hy_conv_w": w["hy_conv_w"][l], "hy_conv_b": w["hy_conv_b"][l][None, :],
        "hy_w1": jnp.pad(w["hy_f_w1"][l], ((0, LANES - nf), (0, 0))), "hy_b1": w["hy_f_b1"][l][None, :],
        "hy_w2": w["hy_f_w2"][l], "hy_b2": w["hy_f_b2"][l][None, :], "hy_w3": w["hy_f_w3"][l],
        "hy_freq": w["hy_freq"][l][None, :], "hy_bias": w["hy_bias"][l][:, None, :],
        "w_out": w["w_out"][l].reshape(4, -1, D).astype(BF16),
        "router_split": _router_split(w["router_w"][l]),
        "w_gate": w["exp_w_gate"][l].astype(BF16), "w_up": w["exp_w_up"][l].astype(BF16),
        "w_down": w["exp_w_down"][l].astype(BF16),
    }


def _moe(xs, sh, sc, lp, consts):
    B, n, _ = xs.shape
    cap = EC_CAPACITY * n // N_EXPERTS
    hcm, idx_t, gate_t = _route_call(xs, sh, sc, lp, consts)
    unfold = lambda a: a[:, :, :SLOT_RADIX].reshape(B, N_EXPERTS, SLOT_RADIX * SLOT_RADIX)[:, :, :cap]
    idx_flat = unfold(idx_t).reshape(-1)
    gts = jnp.pad(jnp.swapaxes(unfold(gate_t), 1, 2), ((0, 0), (0, 0), (0, LANES - N_EXPERTS)))
    return _ffn_call(idx_flat, hcm, gts, lp, cap)


def kernel(x, c, ctx, c_ctx, norm1_g, norm2_g, w_mod, b_mod, w_in, w_out, pool_w, pool_scale, gqa_qnorm_g, gqa_knorm_g, hy_conv_w, hy_conv_b, hy_f_w1, hy_f_b1, hy_f_w2, hy_f_b2, hy_f_w3, hy_freq, hy_bias, mla_cq_g, mla_w_uq, mla_ckv_g, mla_w_ukv, mla_qnorm_g, mla_knorm_g, router_w, exp_w_gate, exp_w_up, exp_w_down):
    w = dict(norm1_g=norm1_g, norm2_g=norm2_g, w_in=w_in, w_out=w_out, pool_w=pool_w, pool_scale=pool_scale,
             gqa_qnorm_g=gqa_qnorm_g, gqa_knorm_g=gqa_knorm_g, hy_conv_w=hy_conv_w, hy_conv_b=hy_conv_b,
             hy_f_w1=hy_f_w1, hy_f_b1=hy_f_b1, hy_f_w2=hy_f_w2, hy_f_b2=hy_f_b2, hy_f_w3=hy_f_w3, hy_freq=hy_freq,
             hy_bias=hy_bias, mla_cq_g=mla_cq_g, mla_w_uq=mla_w_uq, mla_ckv_g=mla_ckv_g, mla_w_ukv=mla_w_ukv,
             mla_qnorm_g=mla_qnorm_g, mla_knorm_g=mla_knorm_g, router_w=router_w, exp_w_gate=exp_w_gate,
             exp_w_up=exp_w_up, exp_w_down=exp_w_down)
    B, n, D = x.shape
    nc = ctx.shape[1]
    depth = w_mod.shape[0]
    ch = hy_bias.shape[2]
    rows = -(-(B + 1) // SUBLANES) * SUBLANES
    cc = jnp.concatenate([c, c_ctx[None, :], jnp.zeros((rows - B - 1, D), F32)], axis=0)
    mod3 = _mod_call(cc, w_mod, b_mod).reshape(depth * rows * MOD_PARTS, 1, D)
    tabs_x = _rope_tables(n, True)
    tabs_c = _rope_tables(nc, False)
    hc_x = _hyena_consts(n, ch)
    hc_c = _hyena_consts(nc, ch)
    tri = _route_consts()
    xc, moe_c = ctx, None
    moe_x = None
    for l in range(depth):
        last = l == depth - 1
        lp = _layer_params(l, w)
        mx = [_mod_vec(mod3, l, rows, i, 0, True) for i in range(MOD_PARTS)]
        mc = [_mod_vec(mod3, l, rows, i, B, False) for i in range(MOD_PARTS)]
        pc = _inproj_call(xc, mc[0], mc[1], lp, tabs_c, last, moe_c)
        kc, vc, mkc, mvc = pc[:4]
        px = _inproj_call(x, mx[0], mx[1], lp, tabs_x, False, moe_x)
        kx, vx, mkx, mvx, qx, mqx, poolx, hyx = px[:8]
        if moe_x is not None:
            x = px[8]
        o_gqa = _attn_call(qx, [kx, kc], [vx, vc], GQA_HEADS, GQA_KV_HEADS, GQA_HD, "attn_gqa")
        o_mla = _attn_call(mqx, [mkx, mkc], [mvx, mvc], MLA_HEADS, MLA_HEADS, MLA_V, "attn_mla")
        o_pool = _pool_call(poolx, lp)
        o_hy = _hyena_call(hyx, lp, hc_x, _hyfilt_call(lp, hc_x))
        x = _outproj_call(x, mx[2], [o_pool, o_gqa, o_hy, o_mla], lp)
        moe_x = (_moe(x, mx[3], mx[4], lp, tri), mx[5])
        if not last:
            qc, mqc, poolc, hyc = pc[4:8]
            if moe_c is not None:
                xc = pc[8]
            oc_gqa = _attn_call(qc, [kc], [vc], GQA_HEADS, GQA_KV_HEADS, GQA_HD, "attn_gqa_ctx")
            oc_mla = _attn_call(mqc, [mkc], [mvc], MLA_HEADS, MLA_HEADS, MLA_V, "attn_mla_ctx")
            oc_pool = _pool_call(poolc, lp)
            oc_hy = _hyena_call(hyc, lp, hc_c, _hyfilt_call(lp, hc_c))
            xc = _outproj_call(xc, mc[2], [oc_pool, oc_gqa, oc_hy, oc_mla], lp)
            moe_c = (_moe(xc, mc[3], mc[4], lp, tri), mc[5])
    return _moe_residual_call(x, moe_x[1], moe_x[0])
```

```python
import functools
import math

import numpy as np
import jax
import jax.numpy as jnp
from jax import lax
from jax.experimental import pallas as pl
from jax.experimental.pallas import tpu as pltpu

F32 = jnp.float32
BF16 = jnp.bfloat16
I32 = jnp.int32

GRID_W = 64
ROPE_THETA = 10000.0
EPS = 1e-6
POOL_WINDOWS = (2, 4, 8, 16)
GQA_HEADS = 4
GQA_KV_HEADS = 2
GQA_HD = 64
HY_BANDS = 8
HY_DECAY_TARGET = 1e-2
HY_FAST_PCT = 0.3
HY_SLOW_PCT = 1.5
MLA_HEADS = 4
MLA_NOPE = 64
MLA_ROPE = 32
MLA_V = 64
MLA_HD = MLA_NOPE + MLA_ROPE
N_EXPERTS = 16
EC_CAPACITY = 2

LANES = 128
SUBLANES = 8
MLA_PAD = LANES
VMEM_LIMIT = 56 * 2 ** 20
HIGHEST = lax.Precision.HIGHEST


def _cp(sem, vmem=VMEM_LIMIT):
    return pltpu.CompilerParams(dimension_semantics=sem, vmem_limit_bytes=vmem)


def _dot(a, b):
    return jnp.dot(a, b, preferred_element_type=F32)


def _dot_nt(a, b):
    return lax.dot_general(a, b, (((1,), (1,)), ((), ())), preferred_element_type=F32)


def _dot_split(x, m):
    hi = x.astype(BF16)
    lo = (x - hi.astype(F32)).astype(BF16)
    return _dot(hi, m) + _dot(lo, m)


def _rms_rows(x):
    return x * lax.rsqrt(jnp.mean(x * x, axis=-1, keepdims=True) + EPS)


def _rope(y, cos, sin_s, perm):
    pw = perm.shape[0]
    yb = y.astype(BF16)
    partner = jnp.concatenate([_dot(yb[:, j:j + pw], perm) for j in range(0, y.shape[1], pw)], axis=1)
    return y * cos + partner * sin_s


def _mod_kernel(c_ref, w_ref, b_ref, o_ref):
    c = c_ref[...]
    s = c * (1.0 / (1.0 + jnp.exp(-c)))
    o_ref[0] = jnp.dot(s, w_ref[0], precision=HIGHEST, preferred_element_type=F32) + b_ref[0]


def _mod_call(cc, w_mod, b_mod):
    L, D, N = w_mod.shape
    R = cc.shape[0]
    tn = 1536
    return pl.pallas_call(
        _mod_kernel,
        grid=(L, N // tn),
        in_specs=[
            pl.BlockSpec((R, D), lambda l, j: (0, 0)),
            pl.BlockSpec((1, D, tn), lambda l, j: (l, 0, j)),
            pl.BlockSpec((1, 1, tn), lambda l, j: (l, 0, j)),
        ],
        out_specs=pl.BlockSpec((1, R, tn), lambda l, j: (l, 0, j)),
        out_shape=jax.ShapeDtypeStruct((L, R, N), F32),
        compiler_params=_cp(("arbitrary", "arbitrary")),
        name="adaln_mod",
    )(cc, w_mod, b_mod.reshape(L, 1, N))


MOD_PARTS = 6


def _mod_vec(mod3, layer, rows, part, sample0, per_sample):
    return mod3, (layer * rows + sample0) * MOD_PARTS + part, MOD_PARTS if per_sample else 0


def _mod_spec(mv, batch_axis):
    arr, base, stride = mv
    return pl.BlockSpec((1, 1, arr.shape[2]), lambda *g: (base + stride * g[batch_axis], 0, 0))


KV_COLS = 512
ALL_COLS = 2048
INPROJ_ROWS = 512
INPROJ_SUB_ROWS = 256


def _inproj_kernel(kv_only, with_moe, x_ref, sh_ref, sc_ref, g1_ref, w_ref,
                   gk_ref, gq_ref, gckv_ref, wukv_ref, gmk_ref, gcq_ref, wuq_ref, gmq_ref,
                   seg_ref, pg_ref, pm_ref, cg_ref, sg_ref, cm_ref, sm_ref, *rest):
    if with_moe:
        moe_ref, g2_ref, outs = rest[0], rest[1], rest[2:]
        if not kv_only:
            outs, xo_ref = outs[:-1], outs[-1]
    else:
        outs = rest
    tm = x_ref.shape[1]
    sub = min(tm, INPROJ_SUB_ROWS)
    chunks = x_ref.shape[2] // LANES
    projs = []
    for r0 in range(0, tm, sub):
        x = x_ref[0, r0:r0 + sub, :]
        if with_moe:
            moe = jnp.concatenate([moe_ref[0, pl.ds(r0 * chunks + c, sub, stride=chunks), :] for c in range(chunks)],
                                  axis=1)
            x = x + g2_ref[0] * moe
            if not kv_only:
                xo_ref[0, r0:r0 + sub, :] = x
        h = _rms_rows(x) * g1_ref[...] * (1.0 + sc_ref[0]) + sh_ref[0]
        projs.append((slice(r0, r0 + sub), _dot(h.astype(BF16), w_ref[...])))
    for rows, p in projs:
        _inproj_heads(kv_only, rows, p, gk_ref, gq_ref, gckv_ref, wukv_ref, gmk_ref, gcq_ref, wuq_ref, gmq_ref,
                      seg_ref, pg_ref, pm_ref, cg_ref, sg_ref, cm_ref, sm_ref, *outs)


def _inproj_heads(kv_only, rows, p, gk_ref, gq_ref, gckv_ref, wukv_ref, gmk_ref, gcq_ref, wuq_ref, gmq_ref,
                  seg_ref, pg_ref, pm_ref, cg_ref, sg_ref, cm_ref, sm_ref, *outs):
    if kv_only:
        k_ref, v_ref, mk_ref, mv_ref = outs
    else:
        k_ref, v_ref, mk_ref, mv_ref, q_ref, mq_ref, pool_ref, hy_ref = outs

    def seg_norm(y, seg):
        ss = _dot((y * y).astype(BF16), seg)
        return y * lax.rsqrt(ss * (1.0 / GQA_HD) + EPS)

    def head_norm(y):
        parts = []
        for hh in range(MLA_HEADS):
            s = y[:, hh * MLA_PAD:(hh + 1) * MLA_PAD]
            ms = jnp.sum(s * s, axis=-1, keepdims=True) * (1.0 / MLA_HD)
            parts.append(s * lax.rsqrt(ms + EPS))
        return jnp.concatenate(parts, axis=1)

    def put_heads(ref, y, width):
        for hh in range(ref.shape[1]):
            ref[0, hh, rows, :] = y[:, hh * width:(hh + 1) * width].astype(ref.dtype)

    def put_values_t(ref, v, dv):
        vt = v.T
        dve = dv + ATTN_ONES_ROWS
        for hh in range(v.shape[1] // dv):
            ref[0, hh * dve:hh * dve + dv, rows] = vt[hh * dv:(hh + 1) * dv, :].astype(ref.dtype)
            ref[0, hh * dve + dv:(hh + 1) * dve, rows] = jnp.ones((ATTN_ONES_ROWS, v.shape[0]), ref.dtype)

    cg, sg = cg_ref[rows, :], sg_ref[rows, :]
    cm, sm, pm = cm_ref[rows, :], sm_ref[rows, :], pm_ref[...]
    kk = seg_norm(p[:, 0:128], seg_ref[0:128, 0:128]) * gk_ref[...]
    kk = _rope(kk, cg[:, 0:128], sg[:, 0:128], pg_ref[0:128, 0:128])
    put_heads(k_ref, kk, GQA_HD)
    put_values_t(v_ref, p[:, 128:256], GQA_HD)
    ckv = _rms_rows(p[:, 256:384]) * gckv_ref[...]
    kv = _dot(ckv.astype(BF16), wukv_ref[...])
    kr = p[:, 384:512]
    kraw = kv[:, 0:MLA_HEADS * MLA_PAD] + jnp.concatenate([kr] * MLA_HEADS, axis=1)
    mk = _rope(head_norm(kraw) * gmk_ref[...], cm, sm, pm)
    put_heads(mk_ref, mk, MLA_PAD)
    put_values_t(mv_ref, kv[:, MLA_HEADS * MLA_PAD:], MLA_V)
    if kv_only:
        return
    qq = seg_norm(p[:, 512:768], seg_ref[...]) * gq_ref[...]
    qq = _rope(qq, cg, sg, pg_ref[...])
    put_heads(q_ref, qq * (GQA_HD ** -0.5 * LOG2E), GQA_HD)
    cq = _rms_rows(p[:, 768:1024]) * gcq_ref[...]
    mq = _rope(head_norm(_dot(cq.astype(BF16), wuq_ref[...])) * gmq_ref[...], cm, sm, pm)
    put_heads(mq_ref, mq * (MLA_HD ** -0.5 * LOG2E), MLA_PAD)
    pool_ref[0, rows, :] = p[:, 1024:1280]
    hy_ref[0, rows, :] = p[:, 1280:2048]


def _inproj_call(xs, sh, sc, lp, tabs, kv_only, moe=None):
    B, n, D = xs.shape
    tm = min(n, INPROJ_ROWS)
    cols = KV_COLS if kv_only else ALL_COLS
    w = lp["w_in_r"][:, :cols]
    full = lambda a: pl.BlockSpec(a.shape, lambda i, b: (0,) * a.ndim)
    tab = lambda a: pl.BlockSpec((tm, a.shape[1]), lambda i, b: (i, 0))
    row = lambda width: pl.BlockSpec((1, tm, width), lambda i, b: (b, i, 0))
    params = [lp["g_k"], lp["g_q"], lp["g_ckv"], lp["w_ukv_r"], lp["g_mk"], lp["g_cq"], lp["w_uq_r"], lp["g_mq"],
              lp["seg"], lp["perm_g"], lp["perm_m"]]
    ones = ATTN_ONES_ROWS
    heads = lambda h, d: (jax.ShapeDtypeStruct((B, h, n, d), BF16), pl.BlockSpec((1, h, tm, d), lambda i, b: (b, 0, i, 0)))
    vals_t = lambda h, d: (jax.ShapeDtypeStruct((B, h * (d + ones), n), BF16),
                           pl.BlockSpec((1, h * (d + ones), tm), lambda i, b: (b, 0, i)))
    rows = lambda d: (jax.ShapeDtypeStruct((B, n, d), F32), row(d))
    outs = [heads(GQA_KV_HEADS, GQA_HD), vals_t(GQA_KV_HEADS, GQA_HD), heads(MLA_HEADS, MLA_PAD), vals_t(MLA_HEADS, MLA_V)]
    if not kv_only:
        outs += [heads(GQA_HEADS, GQA_HD), heads(MLA_HEADS, MLA_PAD), rows(256), rows(768)]
    extra_specs, extra_args = [], []
    if moe is not None:
        extra_specs = [pl.BlockSpec((1, tm * (D // LANES), LANES), lambda i, b: (b, i, 0)), _mod_spec(moe[1], 1)]
        extra_args = [moe[0], moe[1][0]]
        if not kv_only:
            outs.append(rows(D))
    return pl.pallas_call(
        functools.partial(_inproj_kernel, kv_only, moe is not None),
        grid=(n // tm, B),
        in_specs=[row(D), _mod_spec(sh, 1), _mod_spec(sc, 1), full(lp["norm1_g"]), full(w)]
        + [full(a) for a in params] + [tab(a) for a in tabs] + extra_specs,
        out_specs=[spec for _, spec in outs],
        out_shape=[shape for shape, _ in outs],
        compiler_params=_cp(("arbitrary", "arbitrary")),
        name="inproj_kv" if kv_only else "inproj",
    )(xs, sh[0], sc[0], lp["norm1_g"], w, *params, *tabs, *extra_args)


ATTN_KEY_BLOCK = 256
ATTN_Q_TILE = 512
ATTN_ONES_ROWS = 16
LOG2E = math.log2(math.e)


def _attn_kernel(heads, kv_heads, dv, nseg, q_ref, *refs):
    k_refs, vt_refs, o_ref = refs[:nseg], refs[nseg:2 * nseg], refs[2 * nseg]
    grp = heads // kv_heads
    kb = ATTN_KEY_BLOCK
    dve = dv + ATTN_ONES_ROWS
    blocks = [(i, r) for i in range(nseg) for r in range(0, k_refs[i].shape[2], kb)]

    qs = [q_ref[0, hh] for hh in range(heads)]
    run_max = [None] * heads
    acc = [None] * heads
    s_prev = [None] * heads
    for j in range(len(blocks) + 1):
        s_cur = [None] * heads
        for hh in range(heads):
            g = hh // grp
            if j < len(blocks):
                i, r = blocks[j]
                s_cur[hh] = _dot_nt(k_refs[i][0, g, r:r + kb, :], qs[hh])
            if j >= 1:
                i, r = blocks[j - 1]
                bm = jnp.max(s_prev[hh], axis=0, keepdims=True)
                m_new = bm if run_max[hh] is None else jnp.maximum(run_max[hh], bm)
                t = _dot(vt_refs[i][0, g * dve:(g + 1) * dve, r:r + kb], jnp.exp2(s_prev[hh] - m_new).astype(BF16))
                acc[hh] = t if acc[hh] is None else acc[hh] * jnp.exp2(run_max[hh] - m_new) + t
                run_max[hh] = m_new
        s_prev = s_cur
    o_ref[0] = jnp.concatenate([a[0:dv] / a[dv:dv + 1] for a in acc], axis=0).T.astype(o_ref.dtype)


def _attn_call(q, ks, vts, heads, kv_heads, dv, name):
    B, _, n, dk = q.shape
    tq = min(n, ATTN_Q_TILE)
    kspec = lambda a: pl.BlockSpec((1,) + a.shape[1:], lambda b, i: (b,) + (0,) * (a.ndim - 1))
    return pl.pallas_call(
        functools.partial(_attn_kernel, heads, kv_heads, dv, len(ks)),
        grid=(B, n // tq),
        in_specs=[pl.BlockSpec((1, heads, tq, dk), lambda b, i: (b, 0, i, 0))]
        + [kspec(a) for a in ks] + [kspec(a) for a in vts],
        out_specs=pl.BlockSpec((1, tq, heads * dv), lambda b, i: (b, i, 0)),
        out_shape=jax.ShapeDtypeStruct((B, n, heads * dv), BF16),
        compiler_params=_cp(("arbitrary", "arbitrary")),
        name=name,
    )(q, *ks, *vts)


POOL_HALO = 8
assert all(b == 2 * a for a, b in zip(POOL_WINDOWS, POOL_WINDOWS[1:])) and POOL_WINDOWS[0] == 2
assert POOL_WINDOWS[-1] // 2 <= POOL_HALO


def _pool_kernel(n, u_ref, cnt_ref, wbd_ref, scale_ref, o_ref, ue_ref):
    w = u_ref.shape[2]
    gc = w // len(POOL_WINDOWS)
    ne = n + 2 * POOL_HALO
    u = u_ref[0]
    ue_ref[0:POOL_HALO, :] = jnp.zeros((POOL_HALO, w), F32)
    ue_ref[POOL_HALO:POOL_HALO + n, :] = u
    ue_ref[POOL_HALO + n:ne, :] = jnp.zeros((POOL_HALO, w), F32)
    ue = ue_ref[...]
    down = lambda x, k: pltpu.roll(x, k, axis=0)
    up = lambda x, k: pltpu.roll(x, ne - k, axis=0)
    sums = [down(ue, 1) + ue]
    for win in POOL_WINDOWS[:-1]:
        sums.append(down(sums[-1], win // 2) + up(sums[-1], win // 2))
    lane = lax.broadcasted_iota(I32, (n, w), 1)
    total = sums[-1][POOL_HALO:POOL_HALO + n]
    for gi in range(len(POOL_WINDOWS) - 2, -1, -1):
        total = jnp.where(lane < (gi + 1) * gc, sums[gi][POOL_HALO:POOL_HALO + n], total)
    d = total / cnt_ref[...] - u
    o_ref[0] = (_dot(d.astype(BF16), wbd_ref[...]) * scale_ref[...]).astype(o_ref.dtype)


def _pool_counts(n, width):
    t = np.arange(n)
    gc = width // len(POOL_WINDOWS)
    cols = []
    for win in POOL_WINDOWS:
        left = win // 2
        right = win - 1 - left
        cnt = np.minimum(t + right, n - 1) - np.maximum(t - left, 0) + 1
        cols.append(np.repeat(cnt[:, None], gc, axis=1))
    return jnp.asarray(np.concatenate(cols, axis=1), dtype=F32)


def _pool_call(u, lp):
    B, n, W = u.shape
    return pl.pallas_call(
        functools.partial(_pool_kernel, n),
        grid=(B,),
        in_specs=[pl.BlockSpec((1, n, W), lambda b: (b, 0, 0)),
                  pl.BlockSpec((n, W), lambda b: (0, 0)),
                  pl.BlockSpec((W, W), lambda b: (0, 0)),
                  pl.BlockSpec((1, W), lambda b: (0, 0))],
        out_specs=pl.BlockSpec((1, n, W), lambda b: (b, 0, 0)),
        out_shape=jax.ShapeDtypeStruct((B, n, W), BF16),
        scratch_shapes=[pltpu.VMEM((n + 2 * POOL_HALO, W), F32)],
        compiler_params=_cp(("arbitrary",)),
        name="pool",
    )(u, _pool_counts(n, W), lp["pool_bd"], lp["pool_scale"])


HY_BLOCK = 512
HY_SPEC_DTYPE = F32


def _dot3(a_hi, a_lo, x):
    x_hi = x.astype(BF16)
    x_lo = (x - x_hi.astype(F32)).astype(BF16)
    return _dot(a_hi, x_hi) + _dot(a_hi, x_lo) + _dot(a_lo, x_hi)


def _hyfilt_kernel(n, t, feats_ref, w1_ref, b1_ref, w2_ref, b2_ref, w3_ref, fr_ref, decay_ref,
                   chi_ref, clo_ref, shi_ref, slo_ref, hre_ref, him_ref, hn_ref, tc_ref, ts_ref, edge_ref):
    di = pl.program_id(0)
    nb = n // t
    hc2 = w3_ref.shape[1] // 2

    @pl.when(di == 0)
    def _():
        fr = fr_ref[...]
        hid = jnp.sin(fr * (jnp.dot(feats_ref[...], w1_ref[...], precision=HIGHEST, preferred_element_type=F32)
                            + b1_ref[...]))
        hid = jnp.sin(fr * (jnp.dot(hid, w2_ref[...], precision=HIGHEST, preferred_element_type=F32) + b2_ref[...]))
        h = jnp.dot(hid, w3_ref[...], precision=HIGHEST, preferred_element_type=F32)
        dec = decay_ref[...]
        dec2 = jnp.concatenate([dec, dec], axis=1)
        row = lax.broadcasted_iota(I32, (n, 1), 0)
        fwd = h[:, 0:hc2] * dec2
        bwd = jnp.where(row == 0, 0.0, h[:, hc2:2 * hc2] * dec2)
        scl = lax.rsqrt(jnp.sum(fwd * fwd + bwd * bwd, axis=0, keepdims=True) + EPS)
        alt = jnp.where(lax.broadcasted_iota(I32, (t, 1), 0) % 2 == 0, 1.0, -1.0)
        for fam, filt in enumerate((fwd * scl, bwd * scl)):
            for j in range(nb):
                x = filt[j * t:(j + 1) * t]
                tc_ref[fam, j] = _dot3(chi_ref[...], clo_ref[...], x)
                ts_ref[fam, j] = _dot3(shi_ref[...], slo_ref[...], x)
                r = 2 * (fam * nb + j)
                edge_ref[r:r + 1, :] = x[0:1, :]
                edge_ref[r + 1:r + 2, :] = jnp.sum(x * alt, axis=0, keepdims=True)

    k = lax.broadcasted_iota(I32, (t, 1), 0)
    wk = jnp.where(k == 0, 1.0 / (2 * t), 2.0 / (2 * t))
    sig = jnp.where(k % 2 == 0, 1.0, -1.0)
    first = lambda fam, j: edge_ref[2 * (fam * nb + j):2 * (fam * nb + j) + 1, :]
    altsum = lambda fam, j: edge_ref[2 * (fam * nb + j) + 1:2 * (fam * nb + j) + 2, :]
    for dd in range(2 * nb - 1):
        d = dd - (nb - 1)

        def put(ref, val):
            for o in range(2):
                ref[o, 0] = val[:, o * (hc2 // 2):(o + 1) * (hc2 // 2)].astype(ref.dtype)

        @pl.when(di == dd)
        def _(d=d):
            if d == 0:
                put(hre_ref, wk * (tc_ref[0, 0] + tc_ref[1, 0]))
                put(him_ref, wk * (ts_ref[1, 0] - ts_ref[0, 0]))
                put(hn_ref, (altsum(0, 0) + altsum(1, 0)) * (1.0 / (2 * t)))
            else:
                fam, e, sgn = (0, d, -1.0) if d > 0 else (1, -d, 1.0)
                put(hre_ref, wk * (sig * (tc_ref[fam, e - 1] - first(fam, e - 1)) + tc_ref[fam, e]))
                put(him_ref, (sgn * wk) * (sig * ts_ref[fam, e - 1] + ts_ref[fam, e]))
                put(hn_ref, (altsum(fam, e - 1) - first(fam, e - 1) + altsum(fam, e)) * (1.0 / (2 * t)))


def _hyfilt_call(lp, hc):
    n, t = hc["n"], hc["t"]
    nb = n // t
    nd = 2 * nb - 1
    hc2 = lp["hy_w3"].shape[1] // 2
    ch = hc2 // 2
    full = lambda a: pl.BlockSpec(a.shape, lambda i: (0,) * a.ndim)
    args = [hc["feats"], lp["hy_w1"], lp["hy_b1"], lp["hy_w2"], lp["hy_b2"], lp["hy_w3"], lp["hy_freq"], hc["decay"],
            hc["c_hi"], hc["c_lo"], hc["s_hi"], hc["s_lo"]]
    return pl.pallas_call(
        functools.partial(_hyfilt_kernel, n, t),
        grid=(nd,),
        in_specs=[full(a) for a in args],
        out_specs=[pl.BlockSpec((2, 1, t, ch), lambda i: (0, i, 0, 0)), pl.BlockSpec((2, 1, t, ch), lambda i: (0, i, 0, 0)),
                   pl.BlockSpec((2, 1, 1, ch), lambda i: (0, i, 0, 0))],
        out_shape=[jax.ShapeDtypeStruct((2, nd, t, ch), HY_SPEC_DTYPE), jax.ShapeDtypeStruct((2, nd, t, ch), HY_SPEC_DTYPE),
                   jax.ShapeDtypeStruct((2, nd, 1, ch), F32)],
        scratch_shapes=[pltpu.VMEM((2, nb, t, hc2), F32), pltpu.VMEM((2, nb, t, hc2), F32),
                        pltpu.VMEM((4 * nb, hc2), F32)],
        compiler_params=_cp(("arbitrary",)),
        name="hyena_filter",
    )(*args)


HY_FREQ_ROWS = 32


def _hyena_kernel(n, t, u_ref, cw_ref, cb_ref, c_ref, s_ref, hre_ref, him_ref, hn_ref, bias_ref, o_ref,
                  uc_ref, z_ref, a_ref, b_ref, p_ref, q_ref):
    ch = o_ref.shape[2]
    nb = n // t
    row = lax.broadcasted_iota(I32, (t, 1), 0)
    alt = jnp.where(row % 2 == 0, 1.0, -1.0)
    sub = lax.broadcasted_iota(I32, (SUBLANES, 1), 0)
    for g in range(3):
        cols = slice(g * ch, (g + 1) * ch)
        for j in range(nb):
            r0 = j * t
            u = u_ref[0, r0:r0 + t, cols]
            prev = jnp.zeros((1, ch), F32) if j == 0 else u_ref[0, r0 - 1:r0, cols]
            nxt = jnp.zeros((1, ch), F32) if j == nb - 1 else u_ref[0, r0 + t:r0 + t + 1, cols]
            up = pltpu.roll(u, 1, axis=0)
            un = pltpu.roll(u, t - 1, axis=0)
            up = jnp.concatenate([jnp.where(sub == 0, prev, up[0:SUBLANES]), up[SUBLANES:]], axis=0)
            un = jnp.concatenate([un[:t - SUBLANES], jnp.where(sub == SUBLANES - 1, nxt, un[t - SUBLANES:])], axis=0)
            uc_ref[g, r0:r0 + t, :] = (up * cw_ref[0:1, cols] + u * cw_ref[1:2, cols] + un * cw_ref[2:3, cols]
                                       + cb_ref[:, cols])

    z_ref[...] = uc_ref[0]

    def long_conv(o, carry):
        nyq = []
        for j in range(nb):
            z = z_ref[j * t:(j + 1) * t, :]
            zb = z.astype(BF16)
            a_ref[j] = _dot(c_ref[...], zb).astype(a_ref.dtype)
            b_ref[j] = _dot(s_ref[...], zb).astype(b_ref.dtype)
            nyq.append(jnp.sum(z * alt, axis=0, keepdims=True))
        for r0 in range(0, t, HY_FREQ_ROWS):
            rows = slice(r0, r0 + HY_FREQ_ROWS)
            ab = [(a_ref[j, rows, :], b_ref[j, rows, :]) for j in range(nb)]
            for i in range(nb):
                p = q = None
                for j, (a, b) in enumerate(ab):
                    d = i - j + nb - 1
                    hre, him = hre_ref[o, d, rows, :], him_ref[o, d, rows, :]
                    tp = a * hre + b * him
                    tq = b * hre - a * him
                    p = tp if p is None else p + tp
                    q = tq if q is None else q + tq
                p_ref[i, rows, :] = p.astype(BF16)
                q_ref[i, rows, :] = q.astype(BF16)
        for i in range(nb):
            yn = functools.reduce(jnp.add, [nyq[j] * hn_ref[o, i - j + nb - 1] for j in range(nb)])
            y = _dot(c_ref[...], p_ref[i]) + _dot(s_ref[...], q_ref[i])
            blk = slice(i * t, (i + 1) * t)
            z_ref[blk, :] = uc_ref[o + 1, blk, :] * (y + alt * yn + z_ref[blk, :] * bias_ref[o])
        return carry

    lax.fori_loop(0, 2, long_conv, 0)
    o_ref[0] = z_ref[...].astype(o_ref.dtype)


def _hyena_call(u, lp, hc, filt):
    B, n, W = u.shape
    ch = W // 3
    t = hc["t"]
    hre, him, hn = filt
    once = lambda a: pl.BlockSpec(a.shape, lambda b: (0,) * a.ndim, pipeline_mode=pl.Buffered(1))
    return pl.pallas_call(
        functools.partial(_hyena_kernel, n, t),
        scratch_shapes=[pltpu.VMEM((3, n, ch), F32), pltpu.VMEM((n, ch), F32),
                        pltpu.VMEM((n // t, t, ch), HY_SPEC_DTYPE), pltpu.VMEM((n // t, t, ch), HY_SPEC_DTYPE),
                        pltpu.VMEM((n // t, t, ch), BF16), pltpu.VMEM((n // t, t, ch), BF16)],
        grid=(B,),
        in_specs=[pl.BlockSpec((1, n, W), lambda b: (b, 0, 0)),
                  once(lp["hy_conv_w"]), once(lp["hy_conv_b"]), once(hc["c16"]), once(hc["s16"]),
                  once(hre), once(him), once(hn), once(lp["hy_bias"])],
        out_specs=pl.BlockSpec((1, n, ch), lambda b: (b, 0, 0)),
        out_shape=jax.ShapeDtypeStruct((B, n, ch), BF16),
        compiler_params=_cp(("arbitrary",)),
        name="hyena",
    )(u, lp["hy_conv_w"], lp["hy_conv_b"], hc["c16"], hc["s16"], hre, him, hn, lp["hy_bias"])


def _outproj_kernel(x_ref, g_ref, a_ref, b_ref, c_ref, d_ref, w_ref, o_ref):
    mixed = (_dot(a_ref[0], w_ref[0]) + _dot(b_ref[0], w_ref[1])
             + _dot(c_ref[0], w_ref[2]) + _dot(d_ref[0], w_ref[3]))
    o_ref[0] = x_ref[0] + g_ref[0] * mixed


def _outproj_call(xs, gate, parts, lp):
    B, n, D = xs.shape
    tm = min(n, 512)
    gw = parts[0].shape[2]
    return pl.pallas_call(
        _outproj_kernel,
        grid=(B, n // tm),
        in_specs=[pl.BlockSpec((1, tm, D), lambda b, i: (b, i, 0)), _mod_spec(gate, 0)]
        + [pl.BlockSpec((1, tm, gw), lambda b, i: (b, i, 0))] * 4
        + [pl.BlockSpec(lp["w_out"].shape, lambda b, i: (0, 0, 0))],
        out_specs=pl.BlockSpec((1, tm, D), lambda b, i: (b, i, 0)),
        out_shape=jax.ShapeDtypeStruct((B, n, D), F32),
        compiler_params=_cp(("arbitrary", "arbitrary")),
        name="outproj",
    )(xs, gate[0], *parts, lp["w_out"])


PREFIX_BLOCK = 256


def _prefix_lanes(m, tri):
    e, n = m.shape
    nb = n // PREFIX_BLOCK
    stk = jnp.concatenate([m[:, j * PREFIX_BLOCK:(j + 1) * PREFIX_BLOCK] for j in range(nb)], axis=0)
    pre = _dot(stk.astype(BF16), tri)
    outs = []
    carry = jnp.zeros((e, 1), F32)
    for j in range(nb):
        pj = pre[j * e:(j + 1) * e]
        outs.append(pj + carry)
        carry = carry + pj[:, PREFIX_BLOCK - 1:PREFIX_BLOCK]
    return jnp.concatenate(outs, axis=1)


SLOT_RADIX = 16
RADIX_BITS = 4


def _split3(x):
    a = x.astype(BF16).astype(F32)
    r = x - a
    b = r.astype(BF16).astype(F32)
    c = (r - b).astype(BF16).astype(F32)
    return a, b, c


def _digit_onehots(v, weights=None):
    e_n = v.shape[0]
    hi_d = jnp.floor(v * (1.0 / SLOT_RADIX))
    lo_d = v - SLOT_RADIX * hi_d
    dig = lax.broadcasted_iota(I32, (SLOT_RADIX, v.shape[1]), 0).astype(F32)
    his = [hi_d[e:e + 1, :] == dig for e in range(e_n)]
    lo = jnp.concatenate([jnp.where(lo_d[e:e + 1, :] == dig, 1.0, 0.0) for e in range(e_n)], axis=0)
    if weights is None:
        hi = jnp.concatenate([jnp.where(m, 1.0, 0.0) for m in his], axis=0)
    else:
        hi = jnp.concatenate([jnp.where(m, w[e:e + 1, :], 0.0) for w in weights for e, m in enumerate(his)], axis=0)
    return hi.astype(BF16), lo.astype(BF16)


def _route_kernel(n, cap, x_ref, sh_ref, sc_ref, g_ref, wr_ref, tri_ref, fold_ref, pre_ref, low_ref,
                  hcm_ref, idx_ref, gate_ref):
    x = x_ref[0]
    h = _rms_rows(x) * g_ref[...] * (1.0 + sc_ref[0]) + sh_ref[0]
    for c in range(h.shape[1] // LANES):
        hcm_ref[0, pl.ds(c, n, stride=SUBLANES), :] = h[:, c * LANES:(c + 1) * LANES]
    h_hi = h.astype(BF16)
    h_lo = (h - h_hi.astype(F32)).astype(BF16)
    rt = _dot(jnp.concatenate([h_hi, h_lo], axis=1), wr_ref[...]).T
    lt = rt[0:N_EXPERTS, :] + rt[N_EXPERTS:2 * N_EXPERTS, :]
    ex = jnp.exp(lt - jnp.max(lt, axis=0, keepdims=True))
    aff = ex / jnp.sum(ex, axis=0, keepdims=True)
    bits = pltpu.bitcast(aff, I32)
    thr = jnp.zeros((N_EXPERTS, 1), I32)
    for shift in range(32 - RADIX_BITS, -1, -RADIX_BITS):
        top = (1 << (31 - shift)) if shift + RADIX_BITS > 31 else (1 << RADIX_BITS)
        digit = jnp.zeros((N_EXPERTS, 1), I32)
        for c in range(1, top):
            cnt = jnp.sum(jnp.where(bits >= (thr | (c << shift)), 1.0, 0.0), axis=1, keepdims=True)
            digit = digit + jnp.where(cnt >= cap, 1, 0)
        thr = thr | (digit << shift)
    gt = jnp.where(bits > thr, 1.0, 0.0)
    eq = jnp.where(bits == thr, 1.0, 0.0)
    need = cap - jnp.sum(gt, axis=1, keepdims=True)
    tri = tri_ref[...]
    sel = gt + eq * jnp.where(_prefix_lanes(eq, tri) <= need, 1.0, 0.0)
    incl = _prefix_lanes(sel, tri)
    rows = N_EXPERTS * SLOT_RADIX
    shift = SLOT_RADIX.bit_length() - 1
    same = ((lax.broadcasted_iota(I32, (rows, rows), 0) >> shift)
            == (lax.broadcasted_iota(I32, (rows, rows), 1) >> shift))
    fold = fold_ref[...]
    a_hi, a_lo = _digit_onehots(incl)
    hist = jnp.where(same, _dot_nt(a_hi, a_lo), 0.0)
    hist = _dot_split(hist, fold)
    pre = _dot_split(hist, pre_ref[...])
    tot = jnp.broadcast_to(pre[:, SLOT_RADIX - 1:SLOT_RADIX], pre.shape)
    t_hi = tot.astype(BF16)
    t_lo = (tot - t_hi.astype(F32)).astype(BF16)
    idx_ref[0] = (pre + _dot(low_ref[...], t_hi) + _dot(low_ref[...], t_lo)).astype(I32)
    g_hi, g_lo = _digit_onehots(incl - 1.0, _split3(sel * aff))
    gsum = _dot_nt(g_hi, g_lo)
    gsum = jnp.where(same, gsum[0:rows] + gsum[rows:2 * rows] + gsum[2 * rows:3 * rows], 0.0)
    gate_ref[0] = functools.reduce(jnp.add, [_dot(part.astype(BF16), fold) for part in _split3(gsum)])


def _route_consts():
    rows = N_EXPERTS * SLOT_RADIX
    i = np.arange(rows)
    j = np.arange(LANES)
    fold = (i[:, None] % SLOT_RADIX == j[None, :]).astype(np.float32)
    pre = ((j[:, None] <= j[None, :]) & (j[None, :] < SLOT_RADIX)).astype(np.float32)
    low = ((i[:, None] // SLOT_RADIX == i[None, :] // SLOT_RADIX) & (i[None, :] < i[:, None])).astype(np.float32)
    tri = np.triu(np.ones((PREFIX_BLOCK, PREFIX_BLOCK), np.float32))
    return tuple(jnp.asarray(a, dtype=BF16) for a in (tri, fold, pre, low))


def _route_call(xs, sh, sc, lp, consts):
    B, n, D = xs.shape
    cap = EC_CAPACITY * n // N_EXPERTS
    assert cap <= SLOT_RADIX * SLOT_RADIX
    rows = n * (D // LANES)
    srows = N_EXPERTS * SLOT_RADIX
    full = lambda a: pl.BlockSpec(a.shape, lambda b: (0,) * a.ndim)
    return pl.pallas_call(
        functools.partial(_route_kernel, n, cap),
        grid=(B,),
        in_specs=[pl.BlockSpec((1, n, D), lambda b: (b, 0, 0)), _mod_spec(sh, 0), _mod_spec(sc, 0),
                  full(lp["norm2_g"]), full(lp["router_split"])] + [full(a) for a in consts],
        out_specs=[pl.BlockSpec((1, rows, LANES), lambda b: (b, 0, 0)),
                   pl.BlockSpec((1, srows, LANES), lambda b: (b, 0, 0)),
                   pl.BlockSpec((1, srows, LANES), lambda b: (b, 0, 0))],
        out_shape=[jax.ShapeDtypeStruct((B, rows, LANES), F32),
                   jax.ShapeDtypeStruct((B, srows, LANES), I32),
                   jax.ShapeDtypeStruct((B, srows, LANES), F32)],
        compiler_params=_cp(("arbitrary",)),
        name="route",
    )(xs, sh[0], sc[0], lp["norm2_g"], lp["router_split"], *consts)


SCATTER_GROUP = 4


FFN_ROWS = 256


def _ffn_kernel(cap, bt, idx_ref, h_ref, gate_ref, wg_ref, wu_ref, wd_ref, o_ref, xga_ref, xgb_ref, ya_ref, yb_ref):
    b0 = pl.program_id(0) * bt
    k = pl.program_id(1)
    steps = pl.num_programs(1)
    chunks = wg_ref.shape[1] // LANES
    m = bt * cap
    tile = lambda r: pl.ds(pl.multiple_of(r * SUBLANES, SUBLANES), SUBLANES)
    e0 = 2 * k
    e1 = e0 + 1
    wrap = lambda e: jnp.where(e < 0, e + N_EXPERTS, jnp.where(e >= N_EXPERTS, e - N_EXPERTS, e))

    def gather(e, xg_ref):
        for bi in range(bt):
            base = ((b0 + bi) * N_EXPERTS + e) * cap
            for r in range(cap):
                xg_ref[tile(bi * cap + r), :] = h_ref[bi, tile(idx_ref[base + r]), :]

    def scatter(e, y_ref):
        for bi in range(bt):
            base = ((b0 + bi) * N_EXPERTS + e) * cap
            for r0 in range(0, cap, SCATTER_GROUP):
                upd = []
                for r in range(r0, r0 + SCATTER_GROUP):
                    dst = tile(idx_ref[base + r])
                    upd.append((dst, o_ref[bi, dst, :] + y_ref[tile(bi * cap + r), :]))
                for dst, v in upd:
                    o_ref[bi, dst, :] = v

    def ffn(e, w, xg_ref, y_ref):
        xt = jnp.concatenate([xg_ref[pl.ds(c, m, stride=SUBLANES), :] for c in range(chunks)], axis=1).astype(BF16)
        a = _dot(xt, wg_ref[w])
        u = _dot(xt, wu_ref[w])
        hmid = (a * (1.0 / (1.0 + jnp.exp(-a))) * u).astype(BF16)
        lane = lax.broadcasted_iota(I32, (m, LANES), 1)
        gcol = jnp.sum(jnp.where(lane == e, gate_ref[...].reshape(m, LANES), 0.0), axis=1, keepdims=True)
        y = _dot(hmid, wd_ref[w]) * gcol
        for c in range(chunks):
            y_ref[pl.ds(c, m, stride=SUBLANES), :] = y[:, c * LANES:(c + 1) * LANES]

    @pl.when(k == 0)
    def _():
        o_ref[...] = jnp.zeros_like(o_ref)
        yb_ref[...] = jnp.zeros_like(yb_ref)
        gather(e0, xga_ref)

    ffn(e0, 0, xga_ref, ya_ref)
    gather(e1, xgb_ref)
    scatter(wrap(e0 - 1), yb_ref)
    ffn(e1, 1, xgb_ref, yb_ref)
    gather(wrap(e1 + 1), xga_ref)
    scatter(e0, ya_ref)

    @pl.when(k == steps - 1)
    def _():
        scatter(e1, yb_ref)


def _ffn_call(idx_flat, hcm, gate, lp, cap):
    B, rows, _ = hcm.shape
    D, ff = lp["w_gate"].shape[1:]
    bt = min(B, max(1, FFN_ROWS // cap))
    m = bt * cap
    return pl.pallas_call(
        functools.partial(_ffn_kernel, cap, bt),
        grid=(B // bt, N_EXPERTS // 2),
        in_specs=[pl.BlockSpec(memory_space=pltpu.SMEM),
                  pl.BlockSpec((bt, rows, LANES), lambda b, k: (b, 0, 0)),
                  pl.BlockSpec((bt, cap, LANES), lambda b, k: (b, 0, 0)),
                  pl.BlockSpec((2, D, ff), lambda b, k: (k, 0, 0)),
                  pl.BlockSpec((2, D, ff), lambda b, k: (k, 0, 0)),
                  pl.BlockSpec((2, ff, D), lambda b, k: (k, 0, 0))],
        out_specs=pl.BlockSpec((bt, rows, LANES), lambda b, k: (b, 0, 0)),
        out_shape=jax.ShapeDtypeStruct((B, rows, LANES), F32),
        scratch_shapes=[pltpu.VMEM((m * SUBLANES, LANES), F32)] * 4,
        compiler_params=_cp(("arbitrary", "arbitrary")),
        name="expert_ffn",
    )(idx_flat, hcm, gate, lp["w_gate"], lp["w_up"], lp["w_down"])


def _moe_residual_kernel(x_ref, g_ref, m_ref, o_ref):
    tm = x_ref.shape[1]
    chunks = x_ref.shape[2] // LANES
    moe = jnp.concatenate([m_ref[0, pl.ds(c, tm, stride=SUBLANES), :] for c in range(chunks)], axis=1)
    o_ref[0] = x_ref[0] + g_ref[0] * moe


def _moe_residual_call(xs, gate, moe_cm):
    B, n, D = xs.shape
    tm = min(n, 512)
    return pl.pallas_call(
        _moe_residual_kernel,
        grid=(B, n // tm),
        in_specs=[pl.BlockSpec((1, tm, D), lambda b, i: (b, i, 0)), _mod_spec(gate, 0),
                  pl.BlockSpec((1, tm * (D // LANES), LANES), lambda b, i: (b, i, 0))],
        out_specs=pl.BlockSpec((1, tm, D), lambda b, i: (b, i, 0)),
        out_shape=jax.ShapeDtypeStruct((B, n, D), F32),
        compiler_params=_cp(("arbitrary", "arbitrary")),
        name="moe_residual",
    )(xs, gate[0], moe_cm)


def _axial_tables(n, head_pat, half, reps, roped):
    width = reps * len(head_pat)
    if not roped:
        return jnp.ones((n, width), F32), jnp.zeros((n, width), F32)
    pats = list(head_pat) * reps
    on_col = np.array([p is not None and p[0] == 1 for p in pats])
    freq = np.array([0 if p is None else p[1] for p in pats])
    sign = np.array([0.0 if p is None else (-1.0 if p[2] else 1.0) for p in pats], np.float32)
    t = jnp.arange(n)
    row = jnp.floor_divide(t, GRID_W).astype(F32)[:, None]
    col = jnp.remainder(t, GRID_W).astype(F32)[:, None]
    inv = (ROPE_THETA ** (-jnp.arange(half, dtype=F32) / half))[freq][None, :]
    ang = jnp.where(on_col[None, :], col, row) * inv
    return jnp.where(sign[None, :] == 0.0, 1.0, jnp.cos(ang)), sign[None, :] * jnp.sin(ang)


def _partner_matrix(head_pat, half, reps):
    hw = len(head_pat)
    p = np.zeros((hw * reps, hw * reps), np.float32)
    for r in range(reps):
        for j, pat in enumerate(head_pat):
            if pat is not None:
                p[r * hw + (j + half if pat[2] else j - half), r * hw + j] = 1.0
    return jnp.asarray(p, dtype=BF16)


def _rope_patterns():
    gh = GQA_HD // 4
    gpat = [(j // (2 * gh), j % gh, (j % (2 * gh)) < gh) for j in range(GQA_HD)]
    mh = MLA_ROPE // 4
    mpat = [None] * MLA_NOPE + [(j // (2 * mh), j % mh, (j % (2 * mh)) < mh) for j in range(MLA_ROPE)]
    mpat += [None] * (MLA_PAD - MLA_HD)
    return gpat, gh, mpat, mh


def _rope_tables(n, roped):
    gpat, gh, mpat, mh = _rope_patterns()
    return _axial_tables(n, gpat, gh, GQA_HEADS, roped) + _axial_tables(n, mpat, mh, MLA_HEADS, roped)


def _rope_partners():
    gpat, gh, mpat, mh = _rope_patterns()
    return _partner_matrix(gpat, gh, GQA_HEADS), _partner_matrix(mpat, mh, 2)


def _hyena_consts(n, ch):
    f32 = F32
    tb = min(n, HY_BLOCK)
    k = jnp.arange(tb, dtype=I32)
    ang = jnp.remainder(k[:, None] * k[None, :], 2 * tb).astype(f32) * (math.pi / tb)
    c32, s32 = jnp.cos(ang), jnp.sin(ang)
    c_hi, s_hi = c32.astype(BF16), s32.astype(BF16)
    c_lo, s_lo = (c32 - c_hi.astype(f32)).astype(BF16), (s32 - s_hi.astype(f32)).astype(BF16)
    t = jnp.linspace(0.0, 1.0, n, dtype=f32)[:, None]
    lag = jnp.arange(n, dtype=f32)[:, None]
    bands = jnp.linspace(1e-4, HY_BANDS - 1, HY_BANDS, dtype=f32)[None, :]
    a = (2.0 * math.pi / n) * lag * bands
    feats = jnp.concatenate([t, jnp.cos(a), -jnp.sin(a)], axis=-1)
    feats = jnp.pad(feats, ((0, 0), (0, LANES - feats.shape[1])))
    deltas = jnp.abs(jnp.linspace(math.log(HY_DECAY_TARGET) / HY_SLOW_PCT,
                                  math.log(HY_DECAY_TARGET) / HY_FAST_PCT, ch, dtype=f32))
    decay = jnp.exp(-t * deltas[None, :])
    return {"n": n, "t": tb, "c16": c_hi, "s16": s_hi, "c_hi": c_hi, "c_lo": c_lo, "s_hi": s_hi, "s_lo": s_lo,
            "feats": feats, "decay": decay}


def _seg_matrix(width, seg):
    i = np.arange(width)
    return jnp.asarray((i[:, None] // seg == i[None, :] // seg).astype(np.float32), dtype=BF16)


def _router_split(rw):
    d = rw.shape[0]
    hi = rw.astype(BF16)
    lo = (rw - hi.astype(F32)).astype(BF16)
    top = jnp.concatenate([hi, lo, jnp.zeros((d, LANES - 2 * N_EXPERTS), BF16)], axis=1)
    bot = jnp.concatenate([hi, jnp.zeros((d, LANES - N_EXPERTS), BF16)], axis=1)
    return jnp.concatenate([top, bot], axis=0)


def _layer_params(l, w):
    D = w["w_in"].shape[1]
    wi = w["w_in"][l]
    z = lambda c: jnp.zeros((D, c), F32)
    w_in_r = jnp.concatenate(
        [wi[:, 0:384], z(MLA_NOPE), wi[:, 384:416], z(MLA_PAD - MLA_HD), wi[:, 416:]], axis=1).astype(BF16)
    ukv = w["mla_w_ukv"][l].reshape(-1, MLA_HEADS, MLA_NOPE + MLA_V)
    kpart = jnp.pad(ukv[:, :, :MLA_NOPE], ((0, 0), (0, 0), (0, MLA_PAD - MLA_NOPE)))
    w_ukv_r = jnp.concatenate([kpart.reshape(ukv.shape[0], -1), ukv[:, :, MLA_NOPE:].reshape(ukv.shape[0], -1)],
                              axis=1).astype(BF16)
    uq = w["mla_w_uq"][l].reshape(-1, MLA_HEADS, MLA_HD)
    w_uq_r = jnp.pad(uq, ((0, 0), (0, 0), (0, MLA_PAD - MLA_HD))).reshape(uq.shape[0], -1).astype(BF16)
    padg = lambda g: jnp.tile(jnp.pad(g, (0, MLA_PAD - MLA_HD)), MLA_HEADS)[None, :]
    pw = w["pool_w"][l]
    gc = pw.shape[1]
    pool_bd = jnp.zeros((len(POOL_WINDOWS) * gc,) * 2, F32)
    for gi in range(len(POOL_WINDOWS)):
        pool_bd = pool_bd.at[gi * gc:(gi + 1) * gc, gi * gc:(gi + 1) * gc].set(pw[gi])
    nf = w["hy_f_w1"].shape[1]
    perm_g, perm_m = _rope_partners()
    return {
        "perm_g": perm_g, "perm_m": perm_m,
        "norm1_g": w["norm1_g"][l][None, :], "norm2_g": w["norm2_g"][l][None, :],
        "w_in_r": w_in_r, "w_ukv_r": w_ukv_r, "w_uq_r": w_uq_r,
        "g_k": jnp.tile(w["gqa_knorm_g"][l], GQA_KV_HEADS)[None, :],
        "g_q": jnp.tile(w["gqa_qnorm_g"][l], GQA_HEADS)[None, :],
        "g_ckv": w["mla_ckv_g"][l][None, :], "g_cq": w["mla_cq_g"][l][None, :],
        "g_mk": padg(w["mla_knorm_g"][l]), "g_mq": padg(w["mla_qnorm_g"][l]),
        "seg": _seg_matrix(GQA_HEADS * GQA_HD, GQA_HD),
        "pool_bd": pool_bd.astype(BF16), "pool_scale": w["pool_scale"][l][None, :],
        "hy_conv_w": w["hy_conv_w"][l], "hy_conv_b": w["hy_conv_b"][l][None, :],
        "hy_w1": jnp.pad(w["hy_f_w1"][l], ((0, LANES - nf), (0, 0))), "hy_b1": w["hy_f_b1"][l][None, :],
        "hy_w2": w["hy_f_w2"][l], "hy_b2": w["hy_f_b2"][l][None, :], "hy_w3": w["hy_f_w3"][l],
        "hy_freq": w["hy_freq"][l][None, :], "hy_bias": w["hy_bias"][l][:, None, :],
        "w_out": w["w_out"][l].reshape(4, -1, D).astype(BF16),
        "router_split": _router_split(w["router_w"][l]),
        "w_gate": w["exp_w_gate"][l].astype(BF16), "w_up": w["exp_w_up"][l].astype(BF16),
        "w_down": w["exp_w_down"][l].astype(BF16),
    }


def _moe(xs, sh, sc, lp, consts):
    B, n, _ = xs.shape
    cap = EC_CAPACITY * n // N_EXPERTS
    hcm, idx_t, gate_t = _route_call(xs, sh, sc, lp, consts)
    unfold = lambda a: a[:, :, :SLOT_RADIX].reshape(B, N_EXPERTS, SLOT_RADIX * SLOT_RADIX)[:, :, :cap]
    idx_flat = unfold(idx_t).reshape(-1)
    gts = jnp.pad(jnp.swapaxes(unfold(gate_t), 1, 2), ((0, 0), (0, 0), (0, LANES - N_EXPERTS)))
    return _ffn_call(idx_flat, hcm, gts, lp, cap)


def kernel(x, c, ctx, c_ctx, norm1_g, norm2_g, w_mod, b_mod, w_in, w_out, pool_w, pool_scale, gqa_qnorm_g, gqa_knorm_g, hy_conv_w, hy_conv_b, hy_f_w1, hy_f_b1, hy_f_w2, hy_f_b2, hy_f_w3, hy_freq, hy_bias, mla_cq_g, mla_w_uq, mla_ckv_g, mla_w_ukv, mla_qnorm_g, mla_knorm_g, router_w, exp_w_gate, exp_w_up, exp_w_down):
    w = dict(norm1_g=norm1_g, norm2_g=norm2_g, w_in=w_in, w_out=w_out, pool_w=pool_w, pool_scale=pool_scale,
             gqa_qnorm_g=gqa_qnorm_g, gqa_knorm_g=gqa_knorm_g, hy_conv_w=hy_conv_w, hy_conv_b=hy_conv_b,
             hy_f_w1=hy_f_w1, hy_f_b1=hy_f_b1, hy_f_w2=hy_f_w2, hy_f_b2=hy_f_b2, hy_f_w3=hy_f_w3, hy_freq=hy_freq,
             hy_bias=hy_bias, mla_cq_g=mla_cq_g, mla_w_uq=mla_w_uq, mla_ckv_g=mla_ckv_g, mla_w_ukv=mla_w_ukv,
             mla_qnorm_g=mla_qnorm_g, mla_knorm_g=mla_knorm_g, router_w=router_w, exp_w_gate=exp_w_gate,
             exp_w_up=exp_w_up, exp_w_down=exp_w_down)
    B, n, D = x.shape
    nc = ctx.shape[1]
    depth = w_mod.shape[0]
    ch = hy_bias.shape[2]
    rows = -(-(B + 1) // SUBLANES) * SUBLANES
    cc = jnp.concatenate([c, c_ctx[None, :], jnp.zeros((rows - B - 1, D), F32)], axis=0)
    mod3 = _mod_call(cc, w_mod, b_mod).reshape(depth * rows * MOD_PARTS, 1, D)
    tabs_x = _rope_tables(n, True)
    tabs_c = _rope_tables(nc, False)
    hc_x = _hyena_consts(n, ch)
    hc_c = _hyena_consts(nc, ch)
    tri = _route_consts()
    xc, moe_c = ctx, None
    moe_x = None
    for l in range(depth):
        last = l == depth - 1
        lp = _layer_params(l, w)
        mx = [_mod_vec(mod3, l, rows, i, 0, True) for i in range(MOD_PARTS)]
        mc = [_mod_vec(mod3, l, rows, i, B, False) for i in range(MOD_PARTS)]
        pc = _inproj_call(xc, mc[0], mc[1], lp, tabs_c, last, moe_c)
        kc, vc, mkc, mvc = pc[:4]
        px = _inproj_call(x, mx[0], mx[1], lp, tabs_x, False, moe_x)
        kx, vx, mkx, mvx, qx, mqx, poolx, hyx = px[:8]
        if moe_x is not None:
            x = px[8]
        o_gqa = _attn_call(qx, [kx, kc], [vx, vc], GQA_HEADS, GQA_KV_HEADS, GQA_HD, "attn_gqa")
        o_mla = _attn_call(mqx, [mkx, mkc], [mvx, mvc], MLA_HEADS, MLA_HEADS, MLA_V, "attn_mla")
        o_pool = _pool_call(poolx, lp)
        o_hy = _hyena_call(hyx, lp, hc_x, _hyfilt_call(lp, hc_x))
        x = _outproj_call(x, mx[2], [o_pool, o_gqa, o_hy, o_mla], lp)
        moe_x = (_moe(x, mx[3], mx[4], lp, tri), mx[5])
        if not last:
            qc, mqc, poolc, hyc = pc[4:8]
            if moe_c is not None:
                xc = pc[8]
            oc_gqa = _attn_call(qc, [kc], [vc], GQA_HEADS, GQA_KV_HEADS, GQA_HD, "attn_gqa_ctx")
            oc_mla = _attn_call(mqc, [mkc], [mvc], MLA_HEADS, MLA_HEADS, MLA_V, "attn_mla_ctx")
            oc_pool = _pool_call(poolc, lp)
            oc_hy = _hyena_call(hyc, lp, hc_c, _hyfilt_call(lp, hc_c))
            xc = _outproj_call(xc, mc[2], [oc_pool, oc_gqa, oc_hy, oc_mla], lp)
            moe_c = (_moe(xc, mc[3], mc[4], lp, tri), mc[5])
    return _moe_residual_call(x, moe_x[1], moe_x[0])
```

```python
import functools
import math

import numpy as np
import jax
import jax.numpy as jnp
from jax import lax
from jax.experimental import pallas as pl
from jax.experimental.pallas import tpu as pltpu

F32 = jnp.float32
BF16 = jnp.bfloat16
I32 = jnp.int32

GRID_W = 64
ROPE_THETA = 10000.0
EPS = 1e-6
POOL_WINDOWS = (2, 4, 8, 16)
GQA_HEADS = 4
GQA_KV_HEADS = 2
GQA_HD = 64
HY_BANDS = 8
HY_DECAY_TARGET = 1e-2
HY_FAST_PCT = 0.3
HY_SLOW_PCT = 1.5
MLA_HEADS = 4
MLA_NOPE = 64
MLA_ROPE = 32
MLA_V = 64
MLA_HD = MLA_NOPE + MLA_ROPE
N_EXPERTS = 16
EC_CAPACITY = 2

LANES = 128
SUBLANES = 8
MLA_PAD = LANES
VMEM_LIMIT = 56 * 2 ** 20
HIGHEST = lax.Precision.HIGHEST


def _cp(sem, vmem=VMEM_LIMIT):
    return pltpu.CompilerParams(dimension_semantics=sem, vmem_limit_bytes=vmem)


def _dot(a, b):
    return jnp.dot(a, b, preferred_element_type=F32)


def _dot_nt(a, b):
    return lax.dot_general(a, b, (((1,), (1,)), ((), ())), preferred_element_type=F32)


def _dot_split(x, m):
    hi = x.astype(BF16)
    lo = (x - hi.astype(F32)).astype(BF16)
    return _dot(hi, m) + _dot(lo, m)


def _rms_rows(x):
    return x * lax.rsqrt(jnp.mean(x * x, axis=-1, keepdims=True) + EPS)


def _rope(y, cos, sin_s, perm):
    pw = perm.shape[0]
    yb = y.astype(BF16)
    partner = jnp.concatenate([_dot(yb[:, j:j + pw], perm) for j in range(0, y.shape[1], pw)], axis=1)
    return y * cos + partner * sin_s


def _mod_kernel(c_ref, w_ref, b_ref, o_ref):
    c = c_ref[...]
    s = c * (1.0 / (1.0 + jnp.exp(-c)))
    o_ref[0] = jnp.dot(s, w_ref[0], precision=HIGHEST, preferred_element_type=F32) + b_ref[0]


def _mod_call(cc, w_mod, b_mod):
    L, D, N = w_mod.shape
    R = cc.shape[0]
    tn = 1536
    return pl.pallas_call(
        _mod_kernel,
        grid=(L, N // tn),
        in_specs=[
            pl.BlockSpec((R, D), lambda l, j: (0, 0)),
            pl.BlockSpec((1, D, tn), lambda l, j: (l, 0, j)),
            pl.BlockSpec((1, 1, tn), lambda l, j: (l, 0, j)),
        ],
        out_specs=pl.BlockSpec((1, R, tn), lambda l, j: (l, 0, j)),
        out_shape=jax.ShapeDtypeStruct((L, R, N), F32),
        compiler_params=_cp(("arbitrary", "arbitrary")),
        name="adaln_mod",
    )(cc, w_mod, b_mod.reshape(L, 1, N))


MOD_PARTS = 6
STREAM_ROWS = 1024


def _mod_vec(mod3, layer, rows, part, sample0, per_sample):
    return mod3, (layer * rows + sample0) * MOD_PARTS + part, MOD_PARTS if per_sample else 0


def _mod_spec(mv, batch_axis):
    arr, base, stride = mv
    return pl.BlockSpec((1, 1, arr.shape[2]), lambda *g: (base + stride * g[batch_axis], 0, 0))


KV_COLS = 512
ALL_COLS = 2048
INPROJ_ROWS = 512
INPROJ_SUB_ROWS = 256


def _inproj_kernel(kv_only, with_moe, x_ref, sh_ref, sc_ref, g1_ref, w_ref,
                   gk_ref, gq_ref, gckv_ref, wukv_ref, gmk_ref, gcq_ref, wuq_ref, gmq_ref,
                   seg_ref, pg_ref, pm_ref, cg_ref, sg_ref, cm_ref, sm_ref, *rest):
    if with_moe:
        moe_ref, g2_ref, outs = rest[0], rest[1], rest[2:]
        if not kv_only:
            outs, xo_ref = outs[:-1], outs[-1]
    else:
        outs = rest
    tm = x_ref.shape[1]
    sub = min(tm, INPROJ_SUB_ROWS)
    chunks = x_ref.shape[2] // LANES
    projs = []
    for r0 in range(0, tm, sub):
        x = x_ref[0, r0:r0 + sub, :]
        if with_moe:
            moe = jnp.concatenate([moe_ref[0, pl.ds(r0 * chunks + c, sub, stride=chunks), :] for c in range(chunks)],
                                  axis=1)
            x = x + g2_ref[0] * moe
            if not kv_only:
                xo_ref[0, r0:r0 + sub, :] = x
        h = _rms_rows(x) * g1_ref[...] * (1.0 + sc_ref[0]) + sh_ref[0]
        projs.append((slice(r0, r0 + sub), _dot(h.astype(BF16), w_ref[...])))
    for rows, p in projs:
        _inproj_heads(kv_only, rows, p, gk_ref, gq_ref, gckv_ref, wukv_ref, gmk_ref, gcq_ref, wuq_ref, gmq_ref,
                      seg_ref, pg_ref, pm_ref, cg_ref, sg_ref, cm_ref, sm_ref, *outs)


def _inproj_heads(kv_only, rows, p, gk_ref, gq_ref, gckv_ref, wukv_ref, gmk_ref, gcq_ref, wuq_ref, gmq_ref,
                  seg_ref, pg_ref, pm_ref, cg_ref, sg_ref, cm_ref, sm_ref, *outs):
    if kv_only:
        k_ref, v_ref, mk_ref, mv_ref = outs
    else:
        k_ref, v_ref, mk_ref, mv_ref, q_ref, mq_ref, pool_ref, hy_ref = outs

    def seg_norm(y, seg):
        ss = _dot((y * y).astype(BF16), seg)
        return y * lax.rsqrt(ss * (1.0 / GQA_HD) + EPS)

    def head_norm(y):
        parts = []
        for hh in range(MLA_HEADS):
            s = y[:, hh * MLA_PAD:(hh + 1) * MLA_PAD]
            ms = jnp.sum(s * s, axis=-1, keepdims=True) * (1.0 / MLA_HD)
            parts.append(s * lax.rsqrt(ms + EPS))
        return jnp.concatenate(parts, axis=1)

    def put_heads(ref, y, width):
        for hh in range(ref.shape[1]):
            ref[0, hh, rows, :] = y[:, hh * width:(hh + 1) * width].astype(ref.dtype)

    def put_values_t(ref, v, dv):
        vt = v.T
        dve = dv + ATTN_ONES_ROWS
        for hh in range(v.shape[1] // dv):
            ref[0, hh * dve:hh * dve + dv, rows] = vt[hh * dv:(hh + 1) * dv, :].astype(ref.dtype)
            ref[0, hh * dve + dv:(hh + 1) * dve, rows] = jnp.ones((ATTN_ONES_ROWS, v.shape[0]), ref.dtype)

    cg, sg = cg_ref[rows, :], sg_ref[rows, :]
    cm, sm, pm = cm_ref[rows, :], sm_ref[rows, :], pm_ref[...]
    kk = seg_norm(p[:, 0:128], seg_ref[0:128, 0:128]) * gk_ref[...]
    kk = _rope(kk, cg[:, 0:128], sg[:, 0:128], pg_ref[0:128, 0:128])
    put_heads(k_ref, kk, GQA_HD)
    put_values_t(v_ref, p[:, 128:256], GQA_HD)
    ckv = _rms_rows(p[:, 256:384]) * gckv_ref[...]
    kv = _dot(ckv.astype(BF16), wukv_ref[...])
    kr = p[:, 384:512]
    kraw = kv[:, 0:MLA_HEADS * MLA_PAD] + jnp.concatenate([kr] * MLA_HEADS, axis=1)
    mk = _rope(head_norm(kraw) * gmk_ref[...], cm, sm, pm)
    put_heads(mk_ref, mk, MLA_PAD)
    put_values_t(mv_ref, kv[:, MLA_HEADS * MLA_PAD:], MLA_V)
    if kv_only:
        return
    qq = seg_norm(p[:, 512:768], seg_ref[...]) * gq_ref[...]
    qq = _rope(qq, cg, sg, pg_ref[...])
    put_heads(q_ref, qq * (GQA_HD ** -0.5 * LOG2E), GQA_HD)
    cq = _rms_rows(p[:, 768:1024]) * gcq_ref[...]
    mq = _rope(head_norm(_dot(cq.astype(BF16), wuq_ref[...])) * gmq_ref[...], cm, sm, pm)
    put_heads(mq_ref, mq * (MLA_HD ** -0.5 * LOG2E), MLA_PAD)
    pool_ref[0, rows, :] = p[:, 1024:1280]
    hy_ref[0, rows, :] = p[:, 1280:2048]


def _inproj_call(xs, sh, sc, lp, tabs, kv_only, moe=None):
    B, n, D = xs.shape
    tm = min(n, INPROJ_ROWS)
    cols = KV_COLS if kv_only else ALL_COLS
    w = lp["w_in_r"][:, :cols]
    full = lambda a: pl.BlockSpec(a.shape, lambda i, b: (0,) * a.ndim)
    tab = lambda a: pl.BlockSpec((tm, a.shape[1]), lambda i, b: (i, 0))
    row = lambda width: pl.BlockSpec((1, tm, width), lambda i, b: (b, i, 0))
    params = [lp["g_k"], lp["g_q"], lp["g_ckv"], lp["w_ukv_r"], lp["g_mk"], lp["g_cq"], lp["w_uq_r"], lp["g_mq"],
              lp["seg"], lp["perm_g"], lp["perm_m"]]
    ones = ATTN_ONES_ROWS
    heads = lambda h, d: (jax.ShapeDtypeStruct((B, h, n, d), BF16), pl.BlockSpec((1, h, tm, d), lambda i, b: (b, 0, i, 0)))
    vals_t = lambda h, d: (jax.ShapeDtypeStruct((B, h * (d + ones), n), BF16),
                           pl.BlockSpec((1, h * (d + ones), tm), lambda i, b: (b, 0, i)))
    rows = lambda d: (jax.ShapeDtypeStruct((B, n, d), F32), row(d))
    outs = [heads(GQA_KV_HEADS, GQA_HD), vals_t(GQA_KV_HEADS, GQA_HD), heads(MLA_HEADS, MLA_PAD), vals_t(MLA_HEADS, MLA_V)]
    if not kv_only:
        outs += [heads(GQA_HEADS, GQA_HD), heads(MLA_HEADS, MLA_PAD), rows(256), rows(768)]
    extra_specs, extra_args = [], []
    if moe is not None:
        extra_specs = [pl.BlockSpec((1, tm * (D // LANES), LANES), lambda i, b: (b, i, 0)), _mod_spec(moe[1], 1)]
        extra_args = [moe[0], moe[1][0]]
        if not kv_only:
            outs.append(rows(D))
    return pl.pallas_call(
        functools.partial(_inproj_kernel, kv_only, moe is not None),
        grid=(n // tm, B),
        in_specs=[row(D), _mod_spec(sh, 1), _mod_spec(sc, 1), full(lp["norm1_g"]), full(w)]
        + [full(a) for a in params] + [tab(a) for a in tabs] + extra_specs,
        out_specs=[spec for _, spec in outs],
        out_shape=[shape for shape, _ in outs],
        compiler_params=_cp(("arbitrary", "arbitrary")),
        name="inproj_kv" if kv_only else "inproj",
    )(xs, sh[0], sc[0], lp["norm1_g"], w, *params, *tabs, *extra_args)


ATTN_KEY_BLOCK = 256
ATTN_Q_TILE = 512
ATTN_ONES_ROWS = 16
LOG2E = math.log2(math.e)


def _attn_kernel(heads, kv_heads, dv, nseg, q_ref, *refs):
    k_refs, vt_refs, o_ref = refs[:nseg], refs[nseg:2 * nseg], refs[2 * nseg]
    grp = heads // kv_heads
    kb = ATTN_KEY_BLOCK
    dve = dv + ATTN_ONES_ROWS
    blocks = [(i, r) for i in range(nseg) for r in range(0, k_refs[i].shape[2], kb)]

    qs = [q_ref[0, hh] for hh in range(heads)]
    run_max = [None] * heads
    acc = [None] * heads
    s_prev = [None] * heads
    for j in range(len(blocks) + 1):
        s_cur = [None] * heads
        for hh in range(heads):
            g = hh // grp
            if j < len(blocks):
                i, r = blocks[j]
                s_cur[hh] = _dot_nt(k_refs[i][0, g, r:r + kb, :], qs[hh])
            if j >= 1:
                i, r = blocks[j - 1]
                bm = jnp.max(s_prev[hh], axis=0, keepdims=True)
                m_new = bm if run_max[hh] is None else jnp.maximum(run_max[hh], bm)
                t = _dot(vt_refs[i][0, g * dve:(g + 1) * dve, r:r + kb], jnp.exp2(s_prev[hh] - m_new).astype(BF16))
                acc[hh] = t if acc[hh] is None else acc[hh] * jnp.exp2(run_max[hh] - m_new) + t
                run_max[hh] = m_new
        s_prev = s_cur
    o_ref[0] = jnp.concatenate([a[0:dv] / a[dv:dv + 1] for a in acc], axis=0).T.astype(o_ref.dtype)


def _attn_call(q, ks, vts, heads, kv_heads, dv, name):
    B, _, n, dk = q.shape
    tq = min(n, ATTN_Q_TILE)
    kspec = lambda a: pl.BlockSpec((1,) + a.shape[1:], lambda b, i: (b,) + (0,) * (a.ndim - 1))
    return pl.pallas_call(
        functools.partial(_attn_kernel, heads, kv_heads, dv, len(ks)),
        grid=(B, n // tq),
        in_specs=[pl.BlockSpec((1, heads, tq, dk), lambda b, i: (b, 0, i, 0))]
        + [kspec(a) for a in ks] + [kspec(a) for a in vts],
        out_specs=pl.BlockSpec((1, tq, heads * dv), lambda b, i: (b, i, 0)),
        out_shape=jax.ShapeDtypeStruct((B, n, heads * dv), BF16),
        compiler_params=_cp(("arbitrary", "arbitrary")),
        name=name,
    )(q, *ks, *vts)


POOL_HALO = 8
assert all(b == 2 * a for a, b in zip(POOL_WINDOWS, POOL_WINDOWS[1:])) and POOL_WINDOWS[0] == 2
assert POOL_WINDOWS[-1] // 2 <= POOL_HALO


def _pool_kernel(n, u_ref, cnt_ref, wbd_ref, scale_ref, o_ref, ue_ref):
    w = u_ref.shape[2]
    gc = w // len(POOL_WINDOWS)
    ne = n + 2 * POOL_HALO
    u = u_ref[0]
    ue_ref[0:POOL_HALO, :] = jnp.zeros((POOL_HALO, w), F32)
    ue_ref[POOL_HALO:POOL_HALO + n, :] = u
    ue_ref[POOL_HALO + n:ne, :] = jnp.zeros((POOL_HALO, w), F32)
    ue = ue_ref[...]
    down = lambda x, k: pltpu.roll(x, k, axis=0)
    up = lambda x, k: pltpu.roll(x, ne - k, axis=0)
    sums = [down(ue, 1) + ue]
    for win in POOL_WINDOWS[:-1]:
        sums.append(down(sums[-1], win // 2) + up(sums[-1], win // 2))
    lane = lax.broadcasted_iota(I32, (n, w), 1)
    total = sums[-1][POOL_HALO:POOL_HALO + n]
    for gi in range(len(POOL_WINDOWS) - 2, -1, -1):
        total = jnp.where(lane < (gi + 1) * gc, sums[gi][POOL_HALO:POOL_HALO + n], total)
    d = total / cnt_ref[...] - u
    o_ref[0] = (_dot(d.astype(BF16), wbd_ref[...]) * scale_ref[...]).astype(o_ref.dtype)


def _pool_counts(n, width):
    t = np.arange(n)
    gc = width // len(POOL_WINDOWS)
    cols = []
    for win in POOL_WINDOWS:
        left = win // 2
        right = win - 1 - left
        cnt = np.minimum(t + right, n - 1) - np.maximum(t - left, 0) + 1
        cols.append(np.repeat(cnt[:, None], gc, axis=1))
    return jnp.asarray(np.concatenate(cols, axis=1), dtype=F32)


def _pool_call(u, lp):
    B, n, W = u.shape
    return pl.pallas_call(
        functools.partial(_pool_kernel, n),
        grid=(B,),
        in_specs=[pl.BlockSpec((1, n, W), lambda b: (b, 0, 0)),
                  pl.BlockSpec((n, W), lambda b: (0, 0)),
                  pl.BlockSpec((W, W), lambda b: (0, 0)),
                  pl.BlockSpec((1, W), lambda b: (0, 0))],
        out_specs=pl.BlockSpec((1, n, W), lambda b: (b, 0, 0)),
        out_shape=jax.ShapeDtypeStruct((B, n, W), BF16),
        scratch_shapes=[pltpu.VMEM((n + 2 * POOL_HALO, W), F32)],
        compiler_params=_cp(("arbitrary",)),
        name="pool",
    )(u, _pool_counts(n, W), lp["pool_bd"], lp["pool_scale"])


HY_BLOCK = 512
HY_SPEC_DTYPE = F32


def _dot3(a_hi, a_lo, x):
    x_hi = x.astype(BF16)
    x_lo = (x - x_hi.astype(F32)).astype(BF16)
    return _dot(a_hi, x_hi) + _dot(a_hi, x_lo) + _dot(a_lo, x_hi)


def _hyfilt_kernel(n, t, feats_ref, w1_ref, b1_ref, w2_ref, b2_ref, w3_ref, fr_ref, decay_ref,
                   chi_ref, clo_ref, shi_ref, slo_ref, hre_ref, him_ref, hn_ref, tc_ref, ts_ref, edge_ref):
    di = pl.program_id(0)
    nb = n // t
    hc2 = w3_ref.shape[1] // 2

    @pl.when(di == 0)
    def _():
        fr = fr_ref[...]
        hid = jnp.sin(fr * (jnp.dot(feats_ref[...], w1_ref[...], precision=HIGHEST, preferred_element_type=F32)
                            + b1_ref[...]))
        hid = jnp.sin(fr * (jnp.dot(hid, w2_ref[...], precision=HIGHEST, preferred_element_type=F32) + b2_ref[...]))
        h = jnp.dot(hid, w3_ref[...], precision=HIGHEST, preferred_element_type=F32)
        dec = decay_ref[...]
        dec2 = jnp.concatenate([dec, dec], axis=1)
        row = lax.broadcasted_iota(I32, (n, 1), 0)
        fwd = h[:, 0:hc2] * dec2
        bwd = jnp.where(row == 0, 0.0, h[:, hc2:2 * hc2] * dec2)
        scl = lax.rsqrt(jnp.sum(fwd * fwd + bwd * bwd, axis=0, keepdims=True) + EPS)
        alt = jnp.where(lax.broadcasted_iota(I32, (t, 1), 0) % 2 == 0, 1.0, -1.0)
        for fam, filt in enumerate((fwd * scl, bwd * scl)):
            for j in range(nb):
                x = filt[j * t:(j + 1) * t]
                tc_ref[fam, j] = _dot3(chi_ref[...], clo_ref[...], x)
                ts_ref[fam, j] = _dot3(shi_ref[...], slo_ref[...], x)
                r = 2 * (fam * nb + j)
                edge_ref[r:r + 1, :] = x[0:1, :]
                edge_ref[r + 1:r + 2, :] = jnp.sum(x * alt, axis=0, keepdims=True)

    k = lax.broadcasted_iota(I32, (t, 1), 0)
    wk = jnp.where(k == 0, 1.0 / (2 * t), 2.0 / (2 * t))
    sig = jnp.where(k % 2 == 0, 1.0, -1.0)
    first = lambda fam, j: edge_ref[2 * (fam * nb + j):2 * (fam * nb + j) + 1, :]
    altsum = lambda fam, j: edge_ref[2 * (fam * nb + j) + 1:2 * (fam * nb + j) + 2, :]
    for dd in range(2 * nb - 1):
        d = dd - (nb - 1)

        def put(ref, val):
            for o in range(2):
                ref[o, 0] = val[:, o * (hc2 // 2):(o + 1) * (hc2 // 2)].astype(ref.dtype)

        @pl.when(di == dd)
        def _(d=d):
            if d == 0:
                put(hre_ref, wk * (tc_ref[0, 0] + tc_ref[1, 0]))
                put(him_ref, wk * (ts_ref[1, 0] - ts_ref[0, 0]))
                put(hn_ref, (altsum(0, 0) + altsum(1, 0)) * (1.0 / (2 * t)))
            else:
                fam, e, sgn = (0, d, -1.0) if d > 0 else (1, -d, 1.0)
                put(hre_ref, wk * (sig * (tc_ref[fam, e - 1] - first(fam, e - 1)) + tc_ref[fam, e]))
                put(him_ref, (sgn * wk) * (sig * ts_ref[fam, e - 1] + ts_ref[fam, e]))
                put(hn_ref, (altsum(fam, e - 1) - first(fam, e - 1) + altsum(fam, e)) * (1.0 / (2 * t)))


def _hyfilt_call(lp, hc):
    n, t = hc["n"], hc["t"]
    nb = n // t
    nd = 2 * nb - 1
    hc2 = lp["hy_w3"].shape[1] // 2
    ch = hc2 // 2
    full = lambda a: pl.BlockSpec(a.shape, lambda i: (0,) * a.ndim)
    args = [hc["feats"], lp["hy_w1"], lp["hy_b1"], lp["hy_w2"], lp["hy_b2"], lp["hy_w3"], lp["hy_freq"], hc["decay"],
            hc["c_hi"], hc["c_lo"], hc["s_hi"], hc["s_lo"]]
    return pl.pallas_call(
        functools.partial(_hyfilt_kernel, n, t),
        grid=(nd,),
        in_specs=[full(a) for a in args],
        out_specs=[pl.BlockSpec((2, 1, t, ch), lambda i: (0, i, 0, 0)), pl.BlockSpec((2, 1, t, ch), lambda i: (0, i, 0, 0)),
                   pl.BlockSpec((2, 1, 1, ch), lambda i: (0, i, 0, 0))],
        out_shape=[jax.ShapeDtypeStruct((2, nd, t, ch), HY_SPEC_DTYPE), jax.ShapeDtypeStruct((2, nd, t, ch), HY_SPEC_DTYPE),
                   jax.ShapeDtypeStruct((2, nd, 1, ch), F32)],
        scratch_shapes=[pltpu.VMEM((2, nb, t, hc2), F32), pltpu.VMEM((2, nb, t, hc2), F32),
                        pltpu.VMEM((4 * nb, hc2), F32)],
        compiler_params=_cp(("arbitrary",)),
        name="hyena_filter",
    )(*args)


HY_FREQ_ROWS = 32


def _hyena_kernel(n, t, u_ref, cw_ref, cb_ref, c_ref, s_ref, hre_ref, him_ref, hn_ref, bias_ref, o_ref,
                  uc_ref, z_ref, a_ref, b_ref, p_ref, q_ref):
    ch = o_ref.shape[2]
    nb = n // t
    row = lax.broadcasted_iota(I32, (t, 1), 0)
    alt = jnp.where(row % 2 == 0, 1.0, -1.0)
    sub = lax.broadcasted_iota(I32, (SUBLANES, 1), 0)
    for g in range(3):
        cols = slice(g * ch, (g + 1) * ch)
        for j in range(nb):
            r0 = j * t
            u = u_ref[0, r0:r0 + t, cols]
            prev = jnp.zeros((1, ch), F32) if j == 0 else u_ref[0, r0 - 1:r0, cols]
            nxt = jnp.zeros((1, ch), F32) if j == nb - 1 else u_ref[0, r0 + t:r0 + t + 1, cols]
            up = pltpu.roll(u, 1, axis=0)
            un = pltpu.roll(u, t - 1, axis=0)
            up = jnp.concatenate([jnp.where(sub == 0, prev, up[0:SUBLANES]), up[SUBLANES:]], axis=0)
            un = jnp.concatenate([un[:t - SUBLANES], jnp.where(sub == SUBLANES - 1, nxt, un[t - SUBLANES:])], axis=0)
            uc_ref[g, r0:r0 + t, :] = (up * cw_ref[0:1, cols] + u * cw_ref[1:2, cols] + un * cw_ref[2:3, cols]
                                       + cb_ref[:, cols])

    z_ref[...] = uc_ref[0]

    def long_conv(o, carry):
        nyq = []
        for j in range(nb):
            z = z_ref[j * t:(j + 1) * t, :]
            zb = z.astype(BF16)
            a_ref[j] = _dot(c_ref[...], zb).astype(a_ref.dtype)
            b_ref[j] = _dot(s_ref[...], zb).astype(b_ref.dtype)
            nyq.append(jnp.sum(z * alt, axis=0, keepdims=True))
        for r0 in range(0, t, HY_FREQ_ROWS):
            rows = slice(r0, r0 + HY_FREQ_ROWS)
            ab = [(a_ref[j, rows, :], b_ref[j, rows, :]) for j in range(nb)]
            for i in range(nb):
                p = q = None
                for j, (a, b) in enumerate(ab):
                    d = i - j + nb - 1
                    hre, him = hre_ref[o, d, rows, :], him_ref[o, d, rows, :]
                    tp = a * hre + b * him
                    tq = b * hre - a * him
                    p = tp if p is None else p + tp
                    q = tq if q is None else q + tq
                p_ref[i, rows, :] = p.astype(BF16)
                q_ref[i, rows, :] = q.astype(BF16)
        for i in range(nb):
            yn = functools.reduce(jnp.add, [nyq[j] * hn_ref[o, i - j + nb - 1] for j in range(nb)])
            y = _dot(c_ref[...], p_ref[i]) + _dot(s_ref[...], q_ref[i])
            blk = slice(i * t, (i + 1) * t)
            z_ref[blk, :] = uc_ref[o + 1, blk, :] * (y + alt * yn + z_ref[blk, :] * bias_ref[o])
        return carry

    lax.fori_loop(0, 2, long_conv, 0)
    o_ref[0] = z_ref[...].astype(o_ref.dtype)


def _hyena_call(u, lp, hc, filt):
    B, n, W = u.shape
    ch = W // 3
    t = hc["t"]
    hre, him, hn = filt
    once = lambda a: pl.BlockSpec(a.shape, lambda b: (0,) * a.ndim, pipeline_mode=pl.Buffered(1))
    return pl.pallas_call(
        functools.partial(_hyena_kernel, n, t),
        scratch_shapes=[pltpu.VMEM((3, n, ch), F32), pltpu.VMEM((n, ch), F32),
                        pltpu.VMEM((n // t, t, ch), HY_SPEC_DTYPE), pltpu.VMEM((n // t, t, ch), HY_SPEC_DTYPE),
                        pltpu.VMEM((n // t, t, ch), BF16), pltpu.VMEM((n // t, t, ch), BF16)],
        grid=(B,),
        in_specs=[pl.BlockSpec((1, n, W), lambda b: (b, 0, 0)),
                  once(lp["hy_conv_w"]), once(lp["hy_conv_b"]), once(hc["c16"]), once(hc["s16"]),
                  once(hre), once(him), once(hn), once(lp["hy_bias"])],
        out_specs=pl.BlockSpec((1, n, ch), lambda b: (b, 0, 0)),
        out_shape=jax.ShapeDtypeStruct((B, n, ch), BF16),
        compiler_params=_cp(("arbitrary",)),
        name="hyena",
    )(u, lp["hy_conv_w"], lp["hy_conv_b"], hc["c16"], hc["s16"], hre, him, hn, lp["hy_bias"])


def _outproj_kernel(x_ref, g_ref, a_ref, b_ref, c_ref, d_ref, w_ref, o_ref):
    mixed = (_dot(a_ref[0], w_ref[0]) + _dot(b_ref[0], w_ref[1])
             + _dot(c_ref[0], w_ref[2]) + _dot(d_ref[0], w_ref[3]))
    o_ref[0] = x_ref[0] + g_ref[0] * mixed


def _outproj_call(xs, gate, parts, lp):
    B, n, D = xs.shape
    tm = min(n, STREAM_ROWS)
    gw = parts[0].shape[2]
    return pl.pallas_call(
        _outproj_kernel,
        grid=(B, n // tm),
        in_specs=[pl.BlockSpec((1, tm, D), lambda b, i: (b, i, 0)), _mod_spec(gate, 0)]
        + [pl.BlockSpec((1, tm, gw), lambda b, i: (b, i, 0))] * 4
        + [pl.BlockSpec(lp["w_out"].shape, lambda b, i: (0, 0, 0))],
        out_specs=pl.BlockSpec((1, tm, D), lambda b, i: (b, i, 0)),
        out_shape=jax.ShapeDtypeStruct((B, n, D), F32),
        compiler_params=_cp(("arbitrary", "arbitrary")),
        name="outproj",
    )(xs, gate[0], *parts, lp["w_out"])


PREFIX_BLOCK = 256


def _prefix_lanes(m, tri):
    e, n = m.shape
    nb = n // PREFIX_BLOCK
    stk = jnp.concatenate([m[:, j * PREFIX_BLOCK:(j + 1) * PREFIX_BLOCK] for j in range(nb)], axis=0)
    pre = _dot(stk.astype(BF16), tri)
    outs = []
    carry = jnp.zeros((e, 1), F32)
    for j in range(nb):
        pj = pre[j * e:(j + 1) * e]
        outs.append(pj + carry)
        carry = carry + pj[:, PREFIX_BLOCK - 1:PREFIX_BLOCK]
    return jnp.concatenate(outs, axis=1)


SLOT_RADIX = 16
RADIX_BITS = 4


def _split3(x):
    a = x.astype(BF16).astype(F32)
    r = x - a
    b = r.astype(BF16).astype(F32)
    c = (r - b).astype(BF16).astype(F32)
    return a, b, c


def _digit_onehots(v, weights=None):
    e_n = v.shape[0]
    hi_d = jnp.floor(v * (1.0 / SLOT_RADIX))
    lo_d = v - SLOT_RADIX * hi_d
    dig = lax.broadcasted_iota(I32, (SLOT_RADIX, v.shape[1]), 0).astype(F32)
    his = [hi_d[e:e + 1, :] == dig for e in range(e_n)]
    lo = jnp.concatenate([jnp.where(lo_d[e:e + 1, :] == dig, 1.0, 0.0) for e in range(e_n)], axis=0)
    if weights is None:
        hi = jnp.concatenate([jnp.where(m, 1.0, 0.0) for m in his], axis=0)
    else:
        hi = jnp.concatenate([jnp.where(m, w[e:e + 1, :], 0.0) for w in weights for e, m in enumerate(his)], axis=0)
    return hi.astype(BF16), lo.astype(BF16)


def _route_kernel(n, cap, x_ref, sh_ref, sc_ref, g_ref, wr_ref, tri_ref, fold_ref, pre_ref, low_ref,
                  hcm_ref, idx_ref, gate_ref):
    x = x_ref[0]
    h = _rms_rows(x) * g_ref[...] * (1.0 + sc_ref[0]) + sh_ref[0]
    for c in range(h.shape[1] // LANES):
        hcm_ref[0, pl.ds(c, n, stride=SUBLANES), :] = h[:, c * LANES:(c + 1) * LANES]
    h_hi = h.astype(BF16)
    h_lo = (h - h_hi.astype(F32)).astype(BF16)
    rt = _dot(jnp.concatenate([h_hi, h_lo], axis=1), wr_ref[...]).T
    lt = rt[0:N_EXPERTS, :] + rt[N_EXPERTS:2 * N_EXPERTS, :]
    ex = jnp.exp(lt - jnp.max(lt, axis=0, keepdims=True))
    aff = ex / jnp.sum(ex, axis=0, keepdims=True)
    bits = pltpu.bitcast(aff, I32)
    thr = jnp.zeros((N_EXPERTS, 1), I32)
    for shift in range(32 - RADIX_BITS, -1, -RADIX_BITS):
        top = (1 << (31 - shift)) if shift + RADIX_BITS > 31 else (1 << RADIX_BITS)
        digit = jnp.zeros((N_EXPERTS, 1), I32)
        for c in range(1, top):
            cnt = jnp.sum(jnp.where(bits >= (thr | (c << shift)), 1.0, 0.0), axis=1, keepdims=True)
            digit = digit + jnp.where(cnt >= cap, 1, 0)
        thr = thr | (digit << shift)
    gt = jnp.where(bits > thr, 1.0, 0.0)
    eq = jnp.where(bits == thr, 1.0, 0.0)
    need = cap - jnp.sum(gt, axis=1, keepdims=True)
    tri = tri_ref[...]
    sel = gt + eq * jnp.where(_prefix_lanes(eq, tri) <= need, 1.0, 0.0)
    incl = _prefix_lanes(sel, tri)
    rows = N_EXPERTS * SLOT_RADIX
    shift = SLOT_RADIX.bit_length() - 1
    same = ((lax.broadcasted_iota(I32, (rows, rows), 0) >> shift)
            == (lax.broadcasted_iota(I32, (rows, rows), 1) >> shift))
    fold = fold_ref[...]
    a_hi, a_lo = _digit_onehots(incl)
    hist = jnp.where(same, _dot_nt(a_hi, a_lo), 0.0)
    hist = _dot_split(hist, fold)
    pre = _dot_split(hist, pre_ref[...])
    tot = jnp.broadcast_to(pre[:, SLOT_RADIX - 1:SLOT_RADIX], pre.shape)
    t_hi = tot.astype(BF16)
    t_lo = (tot - t_hi.astype(F32)).astype(BF16)
    idx_ref[0] = (pre + _dot(low_ref[...], t_hi) + _dot(low_ref[...], t_lo)).astype(I32)
    g_hi, g_lo = _digit_onehots(incl - 1.0, _split3(sel * aff))
    gsum = _dot_nt(g_hi, g_lo)
    gsum = jnp.where(same, gsum[0:rows] + gsum[rows:2 * rows] + gsum[2 * rows:3 * rows], 0.0)
    gate_ref[0] = functools.reduce(jnp.add, [_dot(part.astype(BF16), fold) for part in _split3(gsum)])


def _route_consts():
    rows = N_EXPERTS * SLOT_RADIX
    i = np.arange(rows)
    j = np.arange(LANES)
    fold = (i[:, None] % SLOT_RADIX == j[None, :]).astype(np.float32)
    pre = ((j[:, None] <= j[None, :]) & (j[None, :] < SLOT_RADIX)).astype(np.float32)
    low = ((i[:, None] // SLOT_RADIX == i[None, :] // SLOT_RADIX) & (i[None, :] < i[:, None])).astype(np.float32)
    tri = np.triu(np.ones((PREFIX_BLOCK, PREFIX_BLOCK), np.float32))
    return tuple(jnp.asarray(a, dtype=BF16) for a in (tri, fold, pre, low))


def _route_call(xs, sh, sc, lp, consts):
    B, n, D = xs.shape
    cap = EC_CAPACITY * n // N_EXPERTS
    assert cap <= SLOT_RADIX * SLOT_RADIX
    rows = n * (D // LANES)
    srows = N_EXPERTS * SLOT_RADIX
    full = lambda a: pl.BlockSpec(a.shape, lambda b: (0,) * a.ndim)
    return pl.pallas_call(
        functools.partial(_route_kernel, n, cap),
        grid=(B,),
        in_specs=[pl.BlockSpec((1, n, D), lambda b: (b, 0, 0)), _mod_spec(sh, 0), _mod_spec(sc, 0),
                  full(lp["norm2_g"]), full(lp["router_split"])] + [full(a) for a in consts],
        out_specs=[pl.BlockSpec((1, rows, LANES), lambda b: (b, 0, 0)),
                   pl.BlockSpec((1, srows, LANES), lambda b: (b, 0, 0)),
                   pl.BlockSpec((1, srows, LANES), lambda b: (b, 0, 0))],
        out_shape=[jax.ShapeDtypeStruct((B, rows, LANES), F32),
                   jax.ShapeDtypeStruct((B, srows, LANES), I32),
                   jax.ShapeDtypeStruct((B, srows, LANES), F32)],
        compiler_params=_cp(("arbitrary",)),
        name="route",
    )(xs, sh[0], sc[0], lp["norm2_g"], lp["router_split"], *consts)


SCATTER_GROUP = 4


FFN_ROWS = 256


def _ffn_kernel(cap, bt, idx_ref, h_ref, gate_ref, wg_ref, wu_ref, wd_ref, o_ref, xga_ref, xgb_ref, ya_ref, yb_ref):
    b0 = pl.program_id(0) * bt
    k = pl.program_id(1)
    steps = pl.num_programs(1)
    chunks = wg_ref.shape[1] // LANES
    m = bt * cap
    tile = lambda r: pl.ds(pl.multiple_of(r * SUBLANES, SUBLANES), SUBLANES)
    e0 = 2 * k
    e1 = e0 + 1
    wrap = lambda e: jnp.where(e < 0, e + N_EXPERTS, jnp.where(e >= N_EXPERTS, e - N_EXPERTS, e))

    def gather(e, xg_ref):
        for bi in range(bt):
            base = ((b0 + bi) * N_EXPERTS + e) * cap
            for r in range(cap):
                xg_ref[tile(bi * cap + r), :] = h_ref[bi, tile(idx_ref[base + r]), :]

    def scatter(e, y_ref):
        for bi in range(bt):
            base = ((b0 + bi) * N_EXPERTS + e) * cap
            for r0 in range(0, cap, SCATTER_GROUP):
                upd = []
                for r in range(r0, r0 + SCATTER_GROUP):
                    dst = tile(idx_ref[base + r])
                    upd.append((dst, o_ref[bi, dst, :] + y_ref[tile(bi * cap + r), :]))
                for dst, v in upd:
                    o_ref[bi, dst, :] = v

    def ffn(e, w, xg_ref, y_ref):
        xt = jnp.concatenate([xg_ref[pl.ds(c, m, stride=SUBLANES), :] for c in range(chunks)], axis=1).astype(BF16)
        a = _dot(xt, wg_ref[w])
        u = _dot(xt, wu_ref[w])
        hmid = (a * (1.0 / (1.0 + jnp.exp(-a))) * u).astype(BF16)
        lane = lax.broadcasted_iota(I32, (m, LANES), 1)
        gcol = jnp.sum(jnp.where(lane == e, gate_ref[...].reshape(m, LANES), 0.0), axis=1, keepdims=True)
        y = _dot(hmid, wd_ref[w]) * gcol
        for c in range(chunks):
            y_ref[pl.ds(c, m, stride=SUBLANES), :] = y[:, c * LANES:(c + 1) * LANES]

    @pl.when(k == 0)
    def _():
        o_ref[...] = jnp.zeros_like(o_ref)
        yb_ref[...] = jnp.zeros_like(yb_ref)
        gather(e0, xga_ref)

    ffn(e0, 0, xga_ref, ya_ref)
    gather(e1, xgb_ref)
    scatter(wrap(e0 - 1), yb_ref)
    ffn(e1, 1, xgb_ref, yb_ref)
    gather(wrap(e1 + 1), xga_ref)
    scatter(e0, ya_ref)

    @pl.when(k == steps - 1)
    def _():
        scatter(e1, yb_ref)


def _ffn_call(idx_flat, hcm, gate, lp, cap):
    B, rows, _ = hcm.shape
    D, ff = lp["w_gate"].shape[1:]
    bt = min(B, max(1, FFN_ROWS // cap))
    m = bt * cap
    return pl.pallas_call(
        functools.partial(_ffn_kernel, cap, bt),
        grid=(B // bt, N_EXPERTS // 2),
        in_specs=[pl.BlockSpec(memory_space=pltpu.SMEM),
                  pl.BlockSpec((bt, rows, LANES), lambda b, k: (b, 0, 0)),
                  pl.BlockSpec((bt, cap, LANES), lambda b, k: (b, 0, 0)),
                  pl.BlockSpec((2, D, ff), lambda b, k: (k, 0, 0)),
                  pl.BlockSpec((2, D, ff), lambda b, k: (k, 0, 0)),
                  pl.BlockSpec((2, ff, D), lambda b, k: (k, 0, 0))],
        out_specs=pl.BlockSpec((bt, rows, LANES), lambda b, k: (b, 0, 0)),
        out_shape=jax.ShapeDtypeStruct((B, rows, LANES), F32),
        scratch_shapes=[pltpu.VMEM((m * SUBLANES, LANES), F32)] * 4,
        compiler_params=_cp(("arbitrary", "arbitrary")),
        name="expert_ffn",
    )(idx_flat, hcm, gate, lp["w_gate"], lp["w_up"], lp["w_down"])


def _moe_residual_kernel(x_ref, g_ref, m_ref, o_ref):
    tm = x_ref.shape[1]
    chunks = x_ref.shape[2] // LANES
    moe = jnp.concatenate([m_ref[0, pl.ds(c, tm, stride=SUBLANES), :] for c in range(chunks)], axis=1)
    o_ref[0] = x_ref[0] + g_ref[0] * moe


def _moe_residual_call(xs, gate, moe_cm):
    B, n, D = xs.shape
    tm = min(n, STREAM_ROWS)
    return pl.pallas_call(
        _moe_residual_kernel,
        grid=(B, n // tm),
        in_specs=[pl.BlockSpec((1, tm, D), lambda b, i: (b, i, 0)), _mod_spec(gate, 0),
                  pl.BlockSpec((1, tm * (D // LANES), LANES), lambda b, i: (b, i, 0))],
        out_specs=pl.BlockSpec((1, tm, D), lambda b, i: (b, i, 0)),
        out_shape=jax.ShapeDtypeStruct((B, n, D), F32),
        compiler_params=_cp(("arbitrary", "arbitrary")),
        name="moe_residual",
    )(xs, gate[0], moe_cm)


def _axial_tables(n, head_pat, half, reps, roped):
    width = reps * len(head_pat)
    if not roped:
        return jnp.ones((n, width), F32), jnp.zeros((n, width), F32)
    pats = list(head_pat) * reps
    on_col = np.array([p is not None and p[0] == 1 for p in pats])
    freq = np.array([0 if p is None else p[1] for p in pats])
    sign = np.array([0.0 if p is None else (-1.0 if p[2] else 1.0) for p in pats], np.float32)
    t = jnp.arange(n)
    row = jnp.floor_divide(t, GRID_W).astype(F32)[:, None]
    col = jnp.remainder(t, GRID_W).astype(F32)[:, None]
    inv = (ROPE_THETA ** (-jnp.arange(half, dtype=F32) / half))[freq][None, :]
    ang = jnp.where(on_col[None, :], col, row) * inv
    return jnp.where(sign[None, :] == 0.0, 1.0, jnp.cos(ang)), sign[None, :] * jnp.sin(ang)


def _partner_matrix(head_pat, half, reps):
    hw = len(head_pat)
    p = np.zeros((hw * reps, hw * reps), np.float32)
    for r in range(reps):
        for j, pat in enumerate(head_pat):
            if pat is not None:
                p[r * hw + (j + half if pat[2] else j - half), r * hw + j] = 1.0
    return jnp.asarray(p, dtype=BF16)


def _rope_patterns():
    gh = GQA_HD // 4
    gpat = [(j // (2 * gh), j % gh, (j % (2 * gh)) < gh) for j in range(GQA_HD)]
    mh = MLA_ROPE // 4
    mpat = [None] * MLA_NOPE + [(j // (2 * mh), j % mh, (j % (2 * mh)) < mh) for j in range(MLA_ROPE)]
    mpat += [None] * (MLA_PAD - MLA_HD)
    return gpat, gh, mpat, mh


def _rope_tables(n, roped):
    gpat, gh, mpat, mh = _rope_patterns()
    return _axial_tables(n, gpat, gh, GQA_HEADS, roped) + _axial_tables(n, mpat, mh, MLA_HEADS, roped)


def _rope_partners():
    gpat, gh, mpat, mh = _rope_patterns()
    return _partner_matrix(gpat, gh, GQA_HEADS), _partner_matrix(mpat, mh, 2)


def _hyena_consts(n, ch):
    f32 = F32
    tb = min(n, HY_BLOCK)
    k = jnp.arange(tb, dtype=I32)
    ang = jnp.remainder(k[:, None] * k[None, :], 2 * tb).astype(f32) * (math.pi / tb)
    c32, s32 = jnp.cos(ang), jnp.sin(ang)
    c_hi, s_hi = c32.astype(BF16), s32.astype(BF16)
    c_lo, s_lo = (c32 - c_hi.astype(f32)).astype(BF16), (s32 - s_hi.astype(f32)).astype(BF16)
    t = jnp.linspace(0.0, 1.0, n, dtype=f32)[:, None]
    lag = jnp.arange(n, dtype=f32)[:, None]
    bands = jnp.linspace(1e-4, HY_BANDS - 1, HY_BANDS, dtype=f32)[None, :]
    a = (2.0 * math.pi / n) * lag * bands
    feats = jnp.concatenate([t, jnp.cos(a), -jnp.sin(a)], axis=-1)
    feats = jnp.pad(feats, ((0, 0), (0, LANES - feats.shape[1])))
    deltas = jnp.abs(jnp.linspace(math.log(HY_DECAY_TARGET) / HY_SLOW_PCT,
                                  math.log(HY_DECAY_TARGET) / HY_FAST_PCT, ch, dtype=f32))
    decay = jnp.exp(-t * deltas[None, :])
    return {"n": n, "t": tb, "c16": c_hi, "s16": s_hi, "c_hi": c_hi, "c_lo": c_lo, "s_hi": s_hi, "s_lo": s_lo,
            "feats": feats, "decay": decay}


def _seg_matrix(width, seg):
    i = np.arange(width)
    return jnp.asarray((i[:, None] // seg == i[None, :] // seg).astype(np.float32), dtype=BF16)


def _router_split(rw):
    d = rw.shape[0]
    hi = rw.astype(BF16)
    lo = (rw - hi.astype(F32)).astype(BF16)
    top = jnp.concatenate([hi, lo, jnp.zeros((d, LANES - 2 * N_EXPERTS), BF16)], axis=1)
    bot = jnp.concatenate([hi, jnp.zeros((d, LANES - N_EXPERTS), BF16)], axis=1)
    return jnp.concatenate([top, bot], axis=0)


def _layer_params(l, w):
    D = w["w_in"].shape[1]
    wi = w["w_in"][l]
    z = lambda c: jnp.zeros((D, c), F32)
    w_in_r = jnp.concatenate(
        [wi[:, 0:384], z(MLA_NOPE), wi[:, 384:416], z(MLA_PAD - MLA_HD), wi[:, 416:]], axis=1).astype(BF16)
    ukv = w["mla_w_ukv"][l].reshape(-1, MLA_HEADS, MLA_NOPE + MLA_V)
    kpart = jnp.pad(ukv[:, :, :MLA_NOPE], ((0, 0), (0, 0), (0, MLA_PAD - MLA_NOPE)))
    w_ukv_r = jnp.concatenate([kpart.reshape(ukv.shape[0], -1), ukv[:, :, MLA_NOPE:].reshape(ukv.shape[0], -1)],
                              axis=1).astype(BF16)
    uq = w["mla_w_uq"][l].reshape(-1, MLA_HEADS, MLA_HD)
    w_uq_r = jnp.pad(uq, ((0, 0), (0, 0), (0, MLA_PAD - MLA_HD))).reshape(uq.shape[0], -1).astype(BF16)
    padg = lambda g: jnp.tile(jnp.pad(g, (0, MLA_PAD - MLA_HD)), MLA_HEADS)[None, :]
    pw = w["pool_w"][l]
    gc = pw.shape[1]
    pool_bd = jnp.zeros((len(POOL_WINDOWS) * gc,) * 2, F32)
    for gi in range(len(POOL_WINDOWS)):
        pool_bd = pool_bd.at[gi * gc:(gi + 1) * gc, gi * gc:(gi + 1) * gc].set(pw[gi])
    nf = w["hy_f_w1"].shape[1]
    perm_g, perm_m = _rope_partners()
    return {
        "perm_g": perm_g, "perm_m": perm_m,
        "norm1_g": w["norm1_g"][l][None, :], "norm2_g": w["norm2_g"][l][None, :],
        "w_in_r": w_in_r, "w_ukv_r": w_ukv_r, "w_uq_r": w_uq_r,
        "g_k": jnp.tile(w["gqa_knorm_g"][l], GQA_KV_HEADS)[None, :],
        "g_q": jnp.tile(w["gqa_qnorm_g"][l], GQA_HEADS)[None, :],
        "g_ckv": w["mla_ckv_g"][l][None, :], "g_cq": w["mla_cq_g"][l][None, :],
        "g_mk": padg(w["mla_knorm_g"][l]), "g_mq": padg(w["mla_qnorm_g"][l]),
        "seg": _seg_matrix(GQA_HEADS * GQA_HD, GQA_HD),
        "pool_bd": pool_bd.astype(BF16), "pool_scale": w["pool_scale"][l][None, :],
        "hy_conv_w": w["hy_conv_w"][l], "hy_conv_b": w["hy_conv_b"][l][None, :],
        "hy_w1": jnp.pad(w["hy_f_w1"][l], ((0, LANES - nf), (0, 0))), "hy_b1": w["hy_f_b1"][l][None, :],
        "hy_w2": w["hy_f_w2"][l], "hy_b2": w["hy_f_b2"][l][None, :], "hy_w3": w["hy_f_w3"][l],
        "hy_freq": w["hy_freq"][l][None, :], "hy_bias": w["hy_bias"][l][:, None, :],
        "w_out": w["w_out"][l].reshape(4, -1, D).astype(BF16),
        "router_split": _router_split(w["router_w"][l]),
        "w_gate": w["exp_w_gate"][l].astype(BF16), "w_up": w["exp_w_up"][l].astype(BF16),
        "w_down": w["exp_w_down"][l].astype(BF16),
    }


def _moe(xs, sh, sc, lp, consts):
    B, n, _ = xs.shape
    cap = EC_CAPACITY * n // N_EXPERTS
    hcm, idx_t, gate_t = _route_call(xs, sh, sc, lp, consts)
    unfold = lambda a: a[:, :, :SLOT_RADIX].reshape(B, N_EXPERTS, SLOT_RADIX * SLOT_RADIX)[:, :, :cap]
    idx_flat = unfold(idx_t).reshape(-1)
    gts = jnp.pad(jnp.swapaxes(unfold(gate_t), 1, 2), ((0, 0), (0, 0), (0, LANES - N_EXPERTS)))
    return _ffn_call(idx_flat, hcm, gts, lp, cap)


def kernel(x, c, ctx, c_ctx, norm1_g, norm2_g, w_mod, b_mod, w_in, w_out, pool_w, pool_scale, gqa_qnorm_g, gqa_knorm_g, hy_conv_w, hy_conv_b, hy_f_w1, hy_f_b1, hy_f_w2, hy_f_b2, hy_f_w3, hy_freq, hy_bias, mla_cq_g, mla_w_uq, mla_ckv_g, mla_w_ukv, mla_qnorm_g, mla_knorm_g, router_w, exp_w_gate, exp_w_up, exp_w_down):
    w = dict(norm1_g=norm1_g, norm2_g=norm2_g, w_in=w_in, w_out=w_out, pool_w=pool_w, pool_scale=pool_scale,
             gqa_qnorm_g=gqa_qnorm_g, gqa_knorm_g=gqa_knorm_g, hy_conv_w=hy_conv_w, hy_conv_b=hy_conv_b,
             hy_f_w1=hy_f_w1, hy_f_b1=hy_f_b1, hy_f_w2=hy_f_w2, hy_f_b2=hy_f_b2, hy_f_w3=hy_f_w3, hy_freq=hy_freq,
             hy_bias=hy_bias, mla_cq_g=mla_cq_g, mla_w_uq=mla_w_uq, mla_ckv_g=mla_ckv_g, mla_w_ukv=mla_w_ukv,
             mla_qnorm_g=mla_qnorm_g, mla_knorm_g=mla_knorm_g, router_w=router_w, exp_w_gate=exp_w_gate,
             exp_w_up=exp_w_up, exp_w_down=exp_w_down)
    B, n, D = x.shape
    nc = ctx.shape[1]
    depth = w_mod.shape[0]
    ch = hy_bias.shape[2]
    rows = -(-(B + 1) // SUBLANES) * SUBLANES
    cc = jnp.concatenate([c, c_ctx[None, :], jnp.zeros((rows - B - 1, D), F32)], axis=0)
    mod3 = _mod_call(cc, w_mod, b_mod).reshape(depth * rows * MOD_PARTS, 1, D)
    tabs_x = _rope_tables(n, True)
    tabs_c = _rope_tables(nc, False)
    hc_x = _hyena_consts(n, ch)
    hc_c = _hyena_consts(nc, ch)
    tri = _route_consts()
    xc, moe_c = ctx, None
    moe_x = None
    for l in range(depth):
        last = l == depth - 1
        lp = _layer_params(l, w)
        mx = [_mod_vec(mod3, l, rows, i, 0, True) for i in range(MOD_PARTS)]
        mc = [_mod_vec(mod3, l, rows, i, B, False) for i in range(MOD_PARTS)]
        pc = _inproj_call(xc, mc[0], mc[1], lp, tabs_c, last, moe_c)
        kc, vc, mkc, mvc = pc[:4]
        px = _inproj_call(x, mx[0], mx[1], lp, tabs_x, False, moe_x)
        kx, vx, mkx, mvx, qx, mqx, poolx, hyx = px[:8]
        if moe_x is not None:
            x = px[8]
        o_gqa = _attn_call(qx, [kx, kc], [vx, vc], GQA_HEADS, GQA_KV_HEADS, GQA_HD, "attn_gqa")
        o_mla = _attn_call(mqx, [mkx, mkc], [mvx, mvc], MLA_HEADS, MLA_HEADS, MLA_V, "attn_mla")
        o_pool = _pool_call(poolx, lp)
        o_hy = _hyena_call(hyx, lp, hc_x, _hyfilt_call(lp, hc_x))
        x = _outproj_call(x, mx[2], [o_pool, o_gqa, o_hy, o_mla], lp)
        moe_x = (_moe(x, mx[3], mx[4], lp, tri), mx[5])
        if not last:
            qc, mqc, poolc, hyc = pc[4:8]
            if moe_c is not None:
                xc = pc[8]
            oc_gqa = _attn_call(qc, [kc], [vc], GQA_HEADS, GQA_KV_HEADS, GQA_HD, "attn_gqa_ctx")
            oc_mla = _attn_call(mqc, [mkc], [mvc], MLA_HEADS, MLA_HEADS, MLA_V, "attn_mla_ctx")
            oc_pool = _pool_call(poolc, lp)
            oc_hy = _hyena_call(hyc, lp, hc_c, _hyfilt_call(lp, hc_c))
            xc = _outproj_call(xc, mc[2], [oc_pool, oc_gqa, oc_hy, oc_mla], lp)
            moe_c = (_moe(xc, mc[3], mc[4], lp, tri), mc[5])
    return _moe_residual_call(x, moe_x[1], moe_x[0])
```

```python
import functools
import math

import numpy as np
import jax
import jax.numpy as jnp
from jax import lax
from jax.experimental import pallas as pl
from jax.experimental.pallas import tpu as pltpu

F32 = jnp.float32
BF16 = jnp.bfloat16
I32 = jnp.int32

GRID_W = 64
ROPE_THETA = 10000.0
EPS = 1e-6
POOL_WINDOWS = (2, 4, 8, 16)
GQA_HEADS = 4
GQA_KV_HEADS = 2
GQA_HD = 64
HY_BANDS = 8
HY_DECAY_TARGET = 1e-2
HY_FAST_PCT = 0.3
HY_SLOW_PCT = 1.5
MLA_HEADS = 4
MLA_NOPE = 64
MLA_ROPE = 32
MLA_V = 64
MLA_HD = MLA_NOPE + MLA_ROPE
N_EXPERTS = 16
EC_CAPACITY = 2

LANES = 128
SUBLANES = 8
MLA_PAD = LANES
VMEM_LIMIT = 56 * 2 ** 20
HIGHEST = lax.Precision.HIGHEST


def _cp(sem, vmem=VMEM_LIMIT):
    return pltpu.CompilerParams(dimension_semantics=sem, vmem_limit_bytes=vmem)


def _dot(a, b):
    return jnp.dot(a, b, preferred_element_type=F32)


def _dot_nt(a, b):
    return lax.dot_general(a, b, (((1,), (1,)), ((), ())), preferred_element_type=F32)


def _dot_split(x, m):
    hi = x.astype(BF16)
    lo = (x - hi.astype(F32)).astype(BF16)
    return _dot(hi, m) + _dot(lo, m)


def _rms_rows(x):
    return x * lax.rsqrt(jnp.mean(x * x, axis=-1, keepdims=True) + EPS)


def _rope(y, cos, sin_s, perm):
    pw = perm.shape[0]
    yb = y.astype(BF16)
    partner = jnp.concatenate([_dot(yb[:, j:j + pw], perm) for j in range(0, y.shape[1], pw)], axis=1)
    return y * cos + partner * sin_s


def _mod_kernel(c_ref, w_ref, b_ref, o_ref):
    c = c_ref[...]
    s = c * (1.0 / (1.0 + jnp.exp(-c)))
    o_ref[0] = jnp.dot(s, w_ref[0], precision=HIGHEST, preferred_element_type=F32) + b_ref[0]


def _mod_call(cc, w_mod, b_mod):
    L, D, N = w_mod.shape
    R = cc.shape[0]
    tn = 1536
    return pl.pallas_call(
        _mod_kernel,
        grid=(L, N // tn),
        in_specs=[
            pl.BlockSpec((R, D), lambda l, j: (0, 0)),
            pl.BlockSpec((1, D, tn), lambda l, j: (l, 0, j)),
            pl.BlockSpec((1, 1, tn), lambda l, j: (l, 0, j)),
        ],
        out_specs=pl.BlockSpec((1, R, tn), lambda l, j: (l, 0, j)),
        out_shape=jax.ShapeDtypeStruct((L, R, N), F32),
        compiler_params=_cp(("arbitrary", "arbitrary")),
        name="adaln_mod",
    )(cc, w_mod, b_mod.reshape(L, 1, N))


MOD_PARTS = 6
STREAM_ROWS = 2048


def _mod_vec(mod3, layer, rows, part, sample0, per_sample):
    return mod3, (layer * rows + sample0) * MOD_PARTS + part, MOD_PARTS if per_sample else 0


def _mod_spec(mv, batch_axis):
    arr, base, stride = mv
    return pl.BlockSpec((1, 1, arr.shape[2]), lambda *g: (base + stride * g[batch_axis], 0, 0))


KV_COLS = 512
ALL_COLS = 2048
INPROJ_ROWS = 512
INPROJ_SUB_ROWS = 256


def _inproj_kernel(kv_only, with_moe, x_ref, sh_ref, sc_ref, g1_ref, w_ref,
                   gk_ref, gq_ref, gckv_ref, wukv_ref, gmk_ref, gcq_ref, wuq_ref, gmq_ref,
                   seg_ref, pg_ref, pm_ref, cg_ref, sg_ref, cm_ref, sm_ref, *rest):
    if with_moe:
        moe_ref, g2_ref, outs = rest[0], rest[1], rest[2:]
        if not kv_only:
            outs, xo_ref = outs[:-1], outs[-1]
    else:
        outs = rest
    tm = x_ref.shape[1]
    sub = min(tm, INPROJ_SUB_ROWS)
    chunks = x_ref.shape[2] // LANES
    projs = []
    for r0 in range(0, tm, sub):
        x = x_ref[0, r0:r0 + sub, :]
        if with_moe:
            moe = jnp.concatenate([moe_ref[0, pl.ds(r0 * chunks + c, sub, stride=chunks), :] for c in range(chunks)],
                                  axis=1)
            x = x + g2_ref[0] * moe
            if not kv_only:
                xo_ref[0, r0:r0 + sub, :] = x
        h = _rms_rows(x) * g1_ref[...] * (1.0 + sc_ref[0]) + sh_ref[0]
        projs.append((slice(r0, r0 + sub), _dot(h.astype(BF16), w_ref[...])))
    for rows, p in projs:
        _inproj_heads(kv_only, rows, p, gk_ref, gq_ref, gckv_ref, wukv_ref, gmk_ref, gcq_ref, wuq_ref, gmq_ref,
                      seg_ref, pg_ref, pm_ref, cg_ref, sg_ref, cm_ref, sm_ref, *outs)


def _inproj_heads(kv_only, rows, p, gk_ref, gq_ref, gckv_ref, wukv_ref, gmk_ref, gcq_ref, wuq_ref, gmq_ref,
                  seg_ref, pg_ref, pm_ref, cg_ref, sg_ref, cm_ref, sm_ref, *outs):
    if kv_only:
        k_ref, v_ref, mk_ref, mv_ref = outs
    else:
        k_ref, v_ref, mk_ref, mv_ref, q_ref, mq_ref, pool_ref, hy_ref = outs

    def seg_norm(y, seg):
        ss = _dot((y * y).astype(BF16), seg)
        return y * lax.rsqrt(ss * (1.0 / GQA_HD) + EPS)

    def head_norm(y):
        parts = []
        for hh in range(MLA_HEADS):
            s = y[:, hh * MLA_PAD:(hh + 1) * MLA_PAD]
            ms = jnp.sum(s * s, axis=-1, keepdims=True) * (1.0 / MLA_HD)
            parts.append(s * lax.rsqrt(ms + EPS))
        return jnp.concatenate(parts, axis=1)

    def put_heads(ref, y, width):
        for hh in range(ref.shape[1]):
            ref[0, hh, rows, :] = y[:, hh * width:(hh + 1) * width].astype(ref.dtype)

    def put_values_t(ref, v, dv):
        vt = v.T
        dve = dv + ATTN_ONES_ROWS
        for hh in range(v.shape[1] // dv):
            ref[0, hh * dve:hh * dve + dv, rows] = vt[hh * dv:(hh + 1) * dv, :].astype(ref.dtype)
            ref[0, hh * dve + dv:(hh + 1) * dve, rows] = jnp.ones((ATTN_ONES_ROWS, v.shape[0]), ref.dtype)

    cg, sg = cg_ref[rows, :], sg_ref[rows, :]
    cm, sm, pm = cm_ref[rows, :], sm_ref[rows, :], pm_ref[...]
    kk = seg_norm(p[:, 0:128], seg_ref[0:128, 0:128]) * gk_ref[...]
    kk = _rope(kk, cg[:, 0:128], sg[:, 0:128], pg_ref[0:128, 0:128])
    put_heads(k_ref, kk, GQA_HD)
    put_values_t(v_ref, p[:, 128:256], GQA_HD)
    ckv = _rms_rows(p[:, 256:384]) * gckv_ref[...]
    kv = _dot(ckv.astype(BF16), wukv_ref[...])
    kr = p[:, 384:512]
    kraw = kv[:, 0:MLA_HEADS * MLA_PAD] + jnp.concatenate([kr] * MLA_HEADS, axis=1)
    mk = _rope(head_norm(kraw) * gmk_ref[...], cm, sm, pm)
    put_heads(mk_ref, mk, MLA_PAD)
    put_values_t(mv_ref, kv[:, MLA_HEADS * MLA_PAD:], MLA_V)
    if kv_only:
        return
    qq = seg_norm(p[:, 512:768], seg_ref[...]) * gq_ref[...]
    qq = _rope(qq, cg, sg, pg_ref[...])
    put_heads(q_ref, qq * (GQA_HD ** -0.5 * LOG2E), GQA_HD)
    cq = _rms_rows(p[:, 768:1024]) * gcq_ref[...]
    mq = _rope(head_norm(_dot(cq.astype(BF16), wuq_ref[...])) * gmq_ref[...], cm, sm, pm)
    put_heads(mq_ref, mq * (MLA_HD ** -0.5 * LOG2E), MLA_PAD)
    pool_ref[0, rows, :] = p[:, 1024:1280]
    hy_ref[0, rows, :] = p[:, 1280:2048]


def _inproj_call(xs, sh, sc, lp, tabs, kv_only, moe=None):
    B, n, D = xs.shape
    tm = min(n, INPROJ_ROWS)
    cols = KV_COLS if kv_only else ALL_COLS
    w = lp["w_in_r"][:, :cols]
    full = lambda a: pl.BlockSpec(a.shape, lambda i, b: (0,) * a.ndim)
    tab = lambda a: pl.BlockSpec((tm, a.shape[1]), lambda i, b: (i, 0))
    row = lambda width: pl.BlockSpec((1, tm, width), lambda i, b: (b, i, 0))
    params = [lp["g_k"], lp["g_q"], lp["g_ckv"], lp["w_ukv_r"], lp["g_mk"], lp["g_cq"], lp["w_uq_r"], lp["g_mq"],
              lp["seg"], lp["perm_g"], lp["perm_m"]]
    ones = ATTN_ONES_ROWS
    heads = lambda h, d: (jax.ShapeDtypeStruct((B, h, n, d), BF16), pl.BlockSpec((1, h, tm, d), lambda i, b: (b, 0, i, 0)))
    vals_t = lambda h, d: (jax.ShapeDtypeStruct((B, h * (d + ones), n), BF16),
                           pl.BlockSpec((1, h * (d + ones), tm), lambda i, b: (b, 0, i)))
    rows = lambda d: (jax.ShapeDtypeStruct((B, n, d), F32), row(d))
    outs = [heads(GQA_KV_HEADS, GQA_HD), vals_t(GQA_KV_HEADS, GQA_HD), heads(MLA_HEADS, MLA_PAD), vals_t(MLA_HEADS, MLA_V)]
    if not kv_only:
        outs += [heads(GQA_HEADS, GQA_HD), heads(MLA_HEADS, MLA_PAD), rows(256), rows(768)]
    extra_specs, extra_args = [], []
    if moe is not None:
        extra_specs = [pl.BlockSpec((1, tm * (D // LANES), LANES), lambda i, b: (b, i, 0)), _mod_spec(moe[1], 1)]
        extra_args = [moe[0], moe[1][0]]
        if not kv_only:
            outs.append(rows(D))
    return pl.pallas_call(
        functools.partial(_inproj_kernel, kv_only, moe is not None),
        grid=(n // tm, B),
        in_specs=[row(D), _mod_spec(sh, 1), _mod_spec(sc, 1), full(lp["norm1_g"]), full(w)]
        + [full(a) for a in params] + [tab(a) for a in tabs] + extra_specs,
        out_specs=[spec for _, spec in outs],
        out_shape=[shape for shape, _ in outs],
        compiler_params=_cp(("arbitrary", "arbitrary")),
        name="inproj_kv" if kv_only else "inproj",
    )(xs, sh[0], sc[0], lp["norm1_g"], w, *params, *tabs, *extra_args)


ATTN_KEY_BLOCK = 256
ATTN_Q_TILE = 1024
ATTN_ONES_ROWS = 16
LOG2E = math.log2(math.e)


def _attn_kernel(heads, kv_heads, dv, nseg, q_ref, *refs):
    k_refs, vt_refs, o_ref = refs[:nseg], refs[nseg:2 * nseg], refs[2 * nseg]
    grp = heads // kv_heads
    kb = ATTN_KEY_BLOCK
    dve = dv + ATTN_ONES_ROWS
    blocks = [(i, r) for i in range(nseg) for r in range(0, k_refs[i].shape[2], kb)]

    qs = [q_ref[0, hh] for hh in range(heads)]
    run_max = [None] * heads
    acc = [None] * heads
    s_prev = [None] * heads
    for j in range(len(blocks) + 1):
        s_cur = [None] * heads
        for hh in range(heads):
            g = hh // grp
            if j < len(blocks):
                i, r = blocks[j]
                s_cur[hh] = _dot_nt(k_refs[i][0, g, r:r + kb, :], qs[hh])
            if j >= 1:
                i, r = blocks[j - 1]
                bm = jnp.max(s_prev[hh], axis=0, keepdims=True)
                m_new = bm if run_max[hh] is None else jnp.maximum(run_max[hh], bm)
                t = _dot(vt_refs[i][0, g * dve:(g + 1) * dve, r:r + kb], jnp.exp2(s_prev[hh] - m_new).astype(BF16))
                acc[hh] = t if acc[hh] is None else acc[hh] * jnp.exp2(run_max[hh] - m_new) + t
                run_max[hh] = m_new
        s_prev = s_cur
    o_ref[0] = jnp.concatenate([a[0:dv] / a[dv:dv + 1] for a in acc], axis=0).T.astype(o_ref.dtype)


def _attn_call(q, ks, vts, heads, kv_heads, dv, name):
    B, _, n, dk = q.shape
    tq = min(n, ATTN_Q_TILE)
    kspec = lambda a: pl.BlockSpec((1,) + a.shape[1:], lambda b, i: (b,) + (0,) * (a.ndim - 1))
    return pl.pallas_call(
        functools.partial(_attn_kernel, heads, kv_heads, dv, len(ks)),
        grid=(B, n // tq),
        in_specs=[pl.BlockSpec((1, heads, tq, dk), lambda b, i: (b, 0, i, 0))]
        + [kspec(a) for a in ks] + [kspec(a) for a in vts],
        out_specs=pl.BlockSpec((1, tq, heads * dv), lambda b, i: (b, i, 0)),
        out_shape=jax.ShapeDtypeStruct((B, n, heads * dv), BF16),
        compiler_params=_cp(("arbitrary", "arbitrary")),
        name=name,
    )(q, *ks, *vts)


POOL_HALO = 8
assert all(b == 2 * a for a, b in zip(POOL_WINDOWS, POOL_WINDOWS[1:])) and POOL_WINDOWS[0] == 2
assert POOL_WINDOWS[-1] // 2 <= POOL_HALO


def _pool_kernel(n, u_ref, cnt_ref, wbd_ref, scale_ref, o_ref, ue_ref):
    w = u_ref.shape[2]
    gc = w // len(POOL_WINDOWS)
    ne = n + 2 * POOL_HALO
    u = u_ref[0]
    ue_ref[0:POOL_HALO, :] = jnp.zeros((POOL_HALO, w), F32)
    ue_ref[POOL_HALO:POOL_HALO + n, :] = u
    ue_ref[POOL_HALO + n:ne, :] = jnp.zeros((POOL_HALO, w), F32)
    ue = ue_ref[...]
    down = lambda x, k: pltpu.roll(x, k, axis=0)
    up = lambda x, k: pltpu.roll(x, ne - k, axis=0)
    sums = [down(ue, 1) + ue]
    for win in POOL_WINDOWS[:-1]:
        sums.append(down(sums[-1], win // 2) + up(sums[-1], win // 2))
    lane = lax.broadcasted_iota(I32, (n, w), 1)
    total = sums[-1][POOL_HALO:POOL_HALO + n]
    for gi in range(len(POOL_WINDOWS) - 2, -1, -1):
        total = jnp.where(lane < (gi + 1) * gc, sums[gi][POOL_HALO:POOL_HALO + n], total)
    d = total / cnt_ref[...] - u
    o_ref[0] = (_dot(d.astype(BF16), wbd_ref[...]) * scale_ref[...]).astype(o_ref.dtype)


def _pool_counts(n, width):
    t = np.arange(n)
    gc = width // len(POOL_WINDOWS)
    cols = []
    for win in POOL_WINDOWS:
        left = win // 2
        right = win - 1 - left
        cnt = np.minimum(t + right, n - 1) - np.maximum(t - left, 0) + 1
        cols.append(np.repeat(cnt[:, None], gc, axis=1))
    return jnp.asarray(np.concatenate(cols, axis=1), dtype=F32)


def _pool_call(u, lp):
    B, n, W = u.shape
    return pl.pallas_call(
        functools.partial(_pool_kernel, n),
        grid=(B,),
        in_specs=[pl.BlockSpec((1, n, W), lambda b: (b, 0, 0)),
                  pl.BlockSpec((n, W), lambda b: (0, 0)),
                  pl.BlockSpec((W, W), lambda b: (0, 0)),
                  pl.BlockSpec((1, W), lambda b: (0, 0))],
        out_specs=pl.BlockSpec((1, n, W), lambda b: (b, 0, 0)),
        out_shape=jax.ShapeDtypeStruct((B, n, W), BF16),
        scratch_shapes=[pltpu.VMEM((n + 2 * POOL_HALO, W), F32)],
        compiler_params=_cp(("arbitrary",)),
        name="pool",
    )(u, _pool_counts(n, W), lp["pool_bd"], lp["pool_scale"])


HY_BLOCK = 512
HY_SPEC_DTYPE = F32


def _dot3(a_hi, a_lo, x):
    x_hi = x.astype(BF16)
    x_lo = (x - x_hi.astype(F32)).astype(BF16)
    return _dot(a_hi, x_hi) + _dot(a_hi, x_lo) + _dot(a_lo, x_hi)


def _hyfilt_kernel(n, t, feats_ref, w1_ref, b1_ref, w2_ref, b2_ref, w3_ref, fr_ref, decay_ref,
                   chi_ref, clo_ref, shi_ref, slo_ref, hre_ref, him_ref, hn_ref, tc_ref, ts_ref, edge_ref):
    di = pl.program_id(0)
    nb = n // t
    hc2 = w3_ref.shape[1] // 2

    @pl.when(di == 0)
    def _():
        fr = fr_ref[...]
        hid = jnp.sin(fr * (jnp.dot(feats_ref[...], w1_ref[...], precision=HIGHEST, preferred_element_type=F32)
                            + b1_ref[...]))
        hid = jnp.sin(fr * (jnp.dot(hid, w2_ref[...], precision=HIGHEST, preferred_element_type=F32) + b2_ref[...]))
        h = jnp.dot(hid, w3_ref[...], precision=HIGHEST, preferred_element_type=F32)
        dec = decay_ref[...]
        dec2 = jnp.concatenate([dec, dec], axis=1)
        row = lax.broadcasted_iota(I32, (n, 1), 0)
        fwd = h[:, 0:hc2] * dec2
        bwd = jnp.where(row == 0, 0.0, h[:, hc2:2 * hc2] * dec2)
        scl = lax.rsqrt(jnp.sum(fwd * fwd + bwd * bwd, axis=0, keepdims=True) + EPS)
        alt = jnp.where(lax.broadcasted_iota(I32, (t, 1), 0) % 2 == 0, 1.0, -1.0)
        for fam, filt in enumerate((fwd * scl, bwd * scl)):
            for j in range(nb):
                x = filt[j * t:(j + 1) * t]
                tc_ref[fam, j] = _dot3(chi_ref[...], clo_ref[...], x)
                ts_ref[fam, j] = _dot3(shi_ref[...], slo_ref[...], x)
                r = 2 * (fam * nb + j)
                edge_ref[r:r + 1, :] = x[0:1, :]
                edge_ref[r + 1:r + 2, :] = jnp.sum(x * alt, axis=0, keepdims=True)

    k = lax.broadcasted_iota(I32, (t, 1), 0)
    wk = jnp.where(k == 0, 1.0 / (2 * t), 2.0 / (2 * t))
    sig = jnp.where(k % 2 == 0, 1.0, -1.0)
    first = lambda fam, j: edge_ref[2 * (fam * nb + j):2 * (fam * nb + j) + 1, :]
    altsum = lambda fam, j: edge_ref[2 * (fam * nb + j) + 1:2 * (fam * nb + j) + 2, :]
    for dd in range(2 * nb - 1):
        d = dd - (nb - 1)

        def put(ref, val):
            for o in range(2):
                ref[o, 0] = val[:, o * (hc2 // 2):(o + 1) * (hc2 // 2)].astype(ref.dtype)

        @pl.when(di == dd)
        def _(d=d):
            if d == 0:
                put(hre_ref, wk * (tc_ref[0, 0] + tc_ref[1, 0]))
                put(him_ref, wk * (ts_ref[1, 0] - ts_ref[0, 0]))
                put(hn_ref, (altsum(0, 0) + altsum(1, 0)) * (1.0 / (2 * t)))
            else:
                fam, e, sgn = (0, d, -1.0) if d > 0 else (1, -d, 1.0)
                put(hre_ref, wk * (sig * (tc_ref[fam, e - 1] - first(fam, e - 1)) + tc_ref[fam, e]))
                put(him_ref, (sgn * wk) * (sig * ts_ref[fam, e - 1] + ts_ref[fam, e]))
                put(hn_ref, (altsum(fam, e - 1) - first(fam, e - 1) + altsum(fam, e)) * (1.0 / (2 * t)))


def _hyfilt_call(lp, hc):
    n, t = hc["n"], hc["t"]
    nb = n // t
    nd = 2 * nb - 1
    hc2 = lp["hy_w3"].shape[1] // 2
    ch = hc2 // 2
    full = lambda a: pl.BlockSpec(a.shape, lambda i: (0,) * a.ndim)
    args = [hc["feats"], lp["hy_w1"], lp["hy_b1"], lp["hy_w2"], lp["hy_b2"], lp["hy_w3"], lp["hy_freq"], hc["decay"],
            hc["c_hi"], hc["c_lo"], hc["s_hi"], hc["s_lo"]]
    return pl.pallas_call(
        functools.partial(_hyfilt_kernel, n, t),
        grid=(nd,),
        in_specs=[full(a) for a in args],
        out_specs=[pl.BlockSpec((2, 1, t, ch), lambda i: (0, i, 0, 0)), pl.BlockSpec((2, 1, t, ch), lambda i: (0, i, 0, 0)),
                   pl.BlockSpec((2, 1, 1, ch), lambda i: (0, i, 0, 0))],
        out_shape=[jax.ShapeDtypeStruct((2, nd, t, ch), HY_SPEC_DTYPE), jax.ShapeDtypeStruct((2, nd, t, ch), HY_SPEC_DTYPE),
                   jax.ShapeDtypeStruct((2, nd, 1, ch), F32)],
        scratch_shapes=[pltpu.VMEM((2, nb, t, hc2), F32), pltpu.VMEM((2, nb, t, hc2), F32),
                        pltpu.VMEM((4 * nb, hc2), F32)],
        compiler_params=_cp(("arbitrary",)),
        name="hyena_filter",
    )(*args)


HY_FREQ_ROWS = 32


def _hyena_kernel(n, t, u_ref, cw_ref, cb_ref, c_ref, s_ref, hre_ref, him_ref, hn_ref, bias_ref, o_ref,
                  uc_ref, z_ref, a_ref, b_ref, p_ref, q_ref):
    ch = o_ref.shape[2]
    nb = n // t
    row = lax.broadcasted_iota(I32, (t, 1), 0)
    alt = jnp.where(row % 2 == 0, 1.0, -1.0)
    sub = lax.broadcasted_iota(I32, (SUBLANES, 1), 0)
    for g in range(3):
        cols = slice(g * ch, (g + 1) * ch)
        for j in range(nb):
            r0 = j * t
            u = u_ref[0, r0:r0 + t, cols]
            prev = jnp.zeros((1, ch), F32) if j == 0 else u_ref[0, r0 - 1:r0, cols]
            nxt = jnp.zeros((1, ch), F32) if j == nb - 1 else u_ref[0, r0 + t:r0 + t + 1, cols]
            up = pltpu.roll(u, 1, axis=0)
            un = pltpu.roll(u, t - 1, axis=0)
            up = jnp.concatenate([jnp.where(sub == 0, prev, up[0:SUBLANES]), up[SUBLANES:]], axis=0)
            un = jnp.concatenate([un[:t - SUBLANES], jnp.where(sub == SUBLANES - 1, nxt, un[t - SUBLANES:])], axis=0)
            uc_ref[g, r0:r0 + t, :] = (up * cw_ref[0:1, cols] + u * cw_ref[1:2, cols] + un * cw_ref[2:3, cols]
                                       + cb_ref[:, cols])

    z_ref[...] = uc_ref[0]

    def long_conv(o, carry):
        nyq = []
        for j in range(nb):
            z = z_ref[j * t:(j + 1) * t, :]
            zb = z.astype(BF16)
            a_ref[j] = _dot(c_ref[...], zb).astype(a_ref.dtype)
            b_ref[j] = _dot(s_ref[...], zb).astype(b_ref.dtype)
            nyq.append(jnp.sum(z * alt, axis=0, keepdims=True))
        for r0 in range(0, t, HY_FREQ_ROWS):
            rows = slice(r0, r0 + HY_FREQ_ROWS)
            ab = [(a_ref[j, rows, :], b_ref[j, rows, :]) for j in range(nb)]
            for i in range(nb):
                p = q = None
                for j, (a, b) in enumerate(ab):
                    d = i - j + nb - 1
                    hre, him = hre_ref[o, d, rows, :], him_ref[o, d, rows, :]
                    tp = a * hre + b * him
                    tq = b * hre - a * him
                    p = tp if p is None else p + tp
                    q = tq if q is None else q + tq
                p_ref[i, rows, :] = p.astype(BF16)
                q_ref[i, rows, :] = q.astype(BF16)
        for i in range(nb):
            yn = functools.reduce(jnp.add, [nyq[j] * hn_ref[o, i - j + nb - 1] for j in range(nb)])
            y = _dot(c_ref[...], p_ref[i]) + _dot(s_ref[...], q_ref[i])
            blk = slice(i * t, (i + 1) * t)
            z_ref[blk, :] = uc_ref[o + 1, blk, :] * (y + alt * yn + z_ref[blk, :] * bias_ref[o])
        return carry

    lax.fori_loop(0, 2, long_conv, 0)
    o_ref[0] = z_ref[...].astype(o_ref.dtype)


def _hyena_call(u, lp, hc, filt):
    B, n, W = u.shape
    ch = W // 3
    t = hc["t"]
    hre, him, hn = filt
    once = lambda a: pl.BlockSpec(a.shape, lambda b: (0,) * a.ndim, pipeline_mode=pl.Buffered(1))
    return pl.pallas_call(
        functools.partial(_hyena_kernel, n, t),
        scratch_shapes=[pltpu.VMEM((3, n, ch), F32), pltpu.VMEM((n, ch), F32),
                        pltpu.VMEM((n // t, t, ch), HY_SPEC_DTYPE), pltpu.VMEM((n // t, t, ch), HY_SPEC_DTYPE),
                        pltpu.VMEM((n // t, t, ch), BF16), pltpu.VMEM((n // t, t, ch), BF16)],
        grid=(B,),
        in_specs=[pl.BlockSpec((1, n, W), lambda b: (b, 0, 0)),
                  once(lp["hy_conv_w"]), once(lp["hy_conv_b"]), once(hc["c16"]), once(hc["s16"]),
                  once(hre), once(him), once(hn), once(lp["hy_bias"])],
        out_specs=pl.BlockSpec((1, n, ch), lambda b: (b, 0, 0)),
        out_shape=jax.ShapeDtypeStruct((B, n, ch), BF16),
        compiler_params=_cp(("arbitrary",)),
        name="hyena",
    )(u, lp["hy_conv_w"], lp["hy_conv_b"], hc["c16"], hc["s16"], hre, him, hn, lp["hy_bias"])


def _outproj_kernel(x_ref, g_ref, a_ref, b_ref, c_ref, d_ref, w_ref, o_ref):
    mixed = (_dot(a_ref[0], w_ref[0]) + _dot(b_ref[0], w_ref[1])
             + _dot(c_ref[0], w_ref[2]) + _dot(d_ref[0], w_ref[3]))
    o_ref[0] = x_ref[0] + g_ref[0] * mixed


def _outproj_call(xs, gate, parts, lp):
    B, n, D = xs.shape
    tm = min(n, STREAM_ROWS)
    gw = parts[0].shape[2]
    return pl.pallas_call(
        _outproj_kernel,
        grid=(B, n // tm),
        in_specs=[pl.BlockSpec((1, tm, D), lambda b, i: (b, i, 0)), _mod_spec(gate, 0)]
        + [pl.BlockSpec((1, tm, gw), lambda b, i: (b, i, 0))] * 4
        + [pl.BlockSpec(lp["w_out"].shape, lambda b, i: (0, 0, 0))],
        out_specs=pl.BlockSpec((1, tm, D), lambda b, i: (b, i, 0)),
        out_shape=jax.ShapeDtypeStruct((B, n, D), F32),
        compiler_params=_cp(("arbitrary", "arbitrary")),
        name="outproj",
    )(xs, gate[0], *parts, lp["w_out"])


PREFIX_BLOCK = 256


def _prefix_lanes(m, tri):
    e, n = m.shape
    nb = n // PREFIX_BLOCK
    stk = jnp.concatenate([m[:, j * PREFIX_BLOCK:(j + 1) * PREFIX_BLOCK] for j in range(nb)], axis=0)
    pre = _dot(stk.astype(BF16), tri)
    outs = []
    carry = jnp.zeros((e, 1), F32)
    for j in range(nb):
        pj = pre[j * e:(j + 1) * e]
        outs.append(pj + carry)
        carry = carry + pj[:, PREFIX_BLOCK - 1:PREFIX_BLOCK]
    return jnp.concatenate(outs, axis=1)


SLOT_RADIX = 16
RADIX_BITS = 4


def _split3(x):
    a = x.astype(BF16).astype(F32)
    r = x - a
    b = r.astype(BF16).astype(F32)
    c = (r - b).astype(BF16).astype(F32)
    return a, b, c


def _digit_onehots(v, weights=None):
    e_n = v.shape[0]
    hi_d = jnp.floor(v * (1.0 / SLOT_RADIX))
    lo_d = v - SLOT_RADIX * hi_d
    dig = lax.broadcasted_iota(I32, (SLOT_RADIX, v.shape[1]), 0).astype(F32)
    his = [hi_d[e:e + 1, :] == dig for e in range(e_n)]
    lo = jnp.concatenate([jnp.where(lo_d[e:e + 1, :] == dig, 1.0, 0.0) for e in range(e_n)], axis=0)
    if weights is None:
        hi = jnp.concatenate([jnp.where(m, 1.0, 0.0) for m in his], axis=0)
    else:
        hi = jnp.concatenate([jnp.where(m, w[e:e + 1, :], 0.0) for w in weights for e, m in enumerate(his)], axis=0)
    return hi.astype(BF16), lo.astype(BF16)


def _route_kernel(n, cap, x_ref, sh_ref, sc_ref, g_ref, wr_ref, tri_ref, fold_ref, pre_ref, low_ref,
                  hcm_ref, idx_ref, gate_ref):
    x = x_ref[0]
    h = _rms_rows(x) * g_ref[...] * (1.0 + sc_ref[0]) + sh_ref[0]
    for c in range(h.shape[1] // LANES):
        hcm_ref[0, pl.ds(c, n, stride=SUBLANES), :] = h[:, c * LANES:(c + 1) * LANES]
    h_hi = h.astype(BF16)
    h_lo = (h - h_hi.astype(F32)).astype(BF16)
    rt = _dot(jnp.concatenate([h_hi, h_lo], axis=1), wr_ref[...]).T
    lt = rt[0:N_EXPERTS, :] + rt[N_EXPERTS:2 * N_EXPERTS, :]
    ex = jnp.exp(lt - jnp.max(lt, axis=0, keepdims=True))
    aff = ex / jnp.sum(ex, axis=0, keepdims=True)
    bits = pltpu.bitcast(aff, I32)
    thr = jnp.zeros((N_EXPERTS, 1), I32)
    for shift in range(32 - RADIX_BITS, -1, -RADIX_BITS):
        top = (1 << (31 - shift)) if shift + RADIX_BITS > 31 else (1 << RADIX_BITS)
        digit = jnp.zeros((N_EXPERTS, 1), I32)
        for c in range(1, top):
            cnt = jnp.sum(jnp.where(bits >= (thr | (c << shift)), 1.0, 0.0), axis=1, keepdims=True)
            digit = digit + jnp.where(cnt >= cap, 1, 0)
        thr = thr | (digit << shift)
    gt = jnp.where(bits > thr, 1.0, 0.0)
    eq = jnp.where(bits == thr, 1.0, 0.0)
    need = cap - jnp.sum(gt, axis=1, keepdims=True)
    tri = tri_ref[...]
    sel = gt + eq * jnp.where(_prefix_lanes(eq, tri) <= need, 1.0, 0.0)
    incl = _prefix_lanes(sel, tri)
    rows = N_EXPERTS * SLOT_RADIX
    shift = SLOT_RADIX.bit_length() - 1
    same = ((lax.broadcasted_iota(I32, (rows, rows), 0) >> shift)
            == (lax.broadcasted_iota(I32, (rows, rows), 1) >> shift))
    fold = fold_ref[...]
    a_hi, a_lo = _digit_onehots(incl)
    hist = jnp.where(same, _dot_nt(a_hi, a_lo), 0.0)
    hist = _dot_split(hist, fold)
    pre = _dot_split(hist, pre_ref[...])
    tot = jnp.broadcast_to(pre[:, SLOT_RADIX - 1:SLOT_RADIX], pre.shape)
    t_hi = tot.astype(BF16)
    t_lo = (tot - t_hi.astype(F32)).astype(BF16)
    idx_ref[0] = (pre + _dot(low_ref[...], t_hi) + _dot(low_ref[...], t_lo)).astype(I32)
    g_hi, g_lo = _digit_onehots(incl - 1.0, _split3(sel * aff))
    gsum = _dot_nt(g_hi, g_lo)
    gsum = jnp.where(same, gsum[0:rows] + gsum[rows:2 * rows] + gsum[2 * rows:3 * rows], 0.0)
    gate_ref[0] = functools.reduce(jnp.add, [_dot(part.astype(BF16), fold) for part in _split3(gsum)])


def _route_consts():
    rows = N_EXPERTS * SLOT_RADIX
    i = np.arange(rows)
    j = np.arange(LANES)
    fold = (i[:, None] % SLOT_RADIX == j[None, :]).astype(np.float32)
    pre = ((j[:, None] <= j[None, :]) & (j[None, :] < SLOT_RADIX)).astype(np.float32)
    low = ((i[:, None] // SLOT_RADIX == i[None, :] // SLOT_RADIX) & (i[None, :] < i[:, None])).astype(np.float32)
    tri = np.triu(np.ones((PREFIX_BLOCK, PREFIX_BLOCK), np.float32))
    return tuple(jnp.asarray(a, dtype=BF16) for a in (tri, fold, pre, low))


def _route_call(xs, sh, sc, lp, consts):
    B, n, D = xs.shape
    cap = EC_CAPACITY * n // N_EXPERTS
    assert cap <= SLOT_RADIX * SLOT_RADIX
    rows = n * (D // LANES)
    srows = N_EXPERTS * SLOT_RADIX
    full = lambda a: pl.BlockSpec(a.shape, lambda b: (0,) * a.ndim)
    return pl.pallas_call(
        functools.partial(_route_kernel, n, cap),
        grid=(B,),
        in_specs=[pl.BlockSpec((1, n, D), lambda b: (b, 0, 0)), _mod_spec(sh, 0), _mod_spec(sc, 0),
                  full(lp["norm2_g"]), full(lp["router_split"])] + [full(a) for a in consts],
        out_specs=[pl.BlockSpec((1, rows, LANES), lambda b: (b, 0, 0)),
                   pl.BlockSpec((1, srows, LANES), lambda b: (b, 0, 0)),
                   pl.BlockSpec((1, srows, LANES), lambda b: (b, 0, 0))],
        out_shape=[jax.ShapeDtypeStruct((B, rows, LANES), F32),
                   jax.ShapeDtypeStruct((B, srows, LANES), I32),
                   jax.ShapeDtypeStruct((B, srows, LANES), F32)],
        compiler_params=_cp(("arbitrary",)),
        name="route",
    )(xs, sh[0], sc[0], lp["norm2_g"], lp["router_split"], *consts)


SCATTER_GROUP = 4


FFN_ROWS = 256


def _ffn_kernel(cap, bt, idx_ref, h_ref, gate_ref, wg_ref, wu_ref, wd_ref, o_ref, xga_ref, xgb_ref, ya_ref, yb_ref):
    b0 = pl.program_id(0) * bt
    k = pl.program_id(1)
    steps = pl.num_programs(1)
    chunks = wg_ref.shape[1] // LANES
    m = bt * cap
    tile = lambda r: pl.ds(pl.multiple_of(r * SUBLANES, SUBLANES), SUBLANES)
    e0 = 2 * k
    e1 = e0 + 1
    wrap = lambda e: jnp.where(e < 0, e + N_EXPERTS, jnp.where(e >= N_EXPERTS, e - N_EXPERTS, e))

    def gather(e, xg_ref):
        for bi in range(bt):
            base = ((b0 + bi) * N_EXPERTS + e) * cap
            for r in range(cap):
                xg_ref[tile(bi * cap + r), :] = h_ref[bi, tile(idx_ref[base + r]), :]

    def scatter(e, y_ref):
        for bi in range(bt):
            base = ((b0 + bi) * N_EXPERTS + e) * cap
            for r0 in range(0, cap, SCATTER_GROUP):
                upd = []
                for r in range(r0, r0 + SCATTER_GROUP):
                    dst = tile(idx_ref[base + r])
                    upd.append((dst, o_ref[bi, dst, :] + y_ref[tile(bi * cap + r), :]))
                for dst, v in upd:
                    o_ref[bi, dst, :] = v

    def ffn(e, w, xg_ref, y_ref):
        xt = jnp.concatenate([xg_ref[pl.ds(c, m, stride=SUBLANES), :] for c in range(chunks)], axis=1).astype(BF16)
        a = _dot(xt, wg_ref[w])
        u = _dot(xt, wu_ref[w])
        hmid = (a * (1.0 / (1.0 + jnp.exp(-a))) * u).astype(BF16)
        lane = lax.broadcasted_iota(I32, (m, LANES), 1)
        gcol = jnp.sum(jnp.where(lane == e, gate_ref[...].reshape(m, LANES), 0.0), axis=1, keepdims=True)
        y = _dot(hmid, wd_ref[w]) * gcol
        for c in range(chunks):
            y_ref[pl.ds(c, m, stride=SUBLANES), :] = y[:, c * LANES:(c + 1) * LANES]

    @pl.when(k == 0)
    def _():
        o_ref[...] = jnp.zeros_like(o_ref)
        yb_ref[...] = jnp.zeros_like(yb_ref)
        gather(e0, xga_ref)

    ffn(e0, 0, xga_ref, ya_ref)
    gather(e1, xgb_ref)
    scatter(wrap(e0 - 1), yb_ref)
    ffn(e1, 1, xgb_ref, yb_ref)
    gather(wrap(e1 + 1), xga_ref)
    scatter(e0, ya_ref)

    @pl.when(k == steps - 1)
    def _():
        scatter(e1, yb_ref)


def _ffn_call(idx_flat, hcm, gate, lp, cap):
    B, rows, _ = hcm.shape
    D, ff = lp["w_gate"].shape[1:]
    bt = min(B, max(1, FFN_ROWS // cap))
    m = bt * cap
    return pl.pallas_call(
        functools.partial(_ffn_kernel, cap, bt),
        grid=(B // bt, N_EXPERTS // 2),
        in_specs=[pl.BlockSpec(memory_space=pltpu.SMEM),
                  pl.BlockSpec((bt, rows, LANES), lambda b, k: (b, 0, 0)),
                  pl.BlockSpec((bt, cap, LANES), lambda b, k: (b, 0, 0)),
                  pl.BlockSpec((2, D, ff), lambda b, k: (k, 0, 0)),
                  pl.BlockSpec((2, D, ff), lambda b, k: (k, 0, 0)),
                  pl.BlockSpec((2, ff, D), lambda b, k: (k, 0, 0))],
        out_specs=pl.BlockSpec((bt, rows, LANES), lambda b, k: (b, 0, 0)),
        out_shape=jax.ShapeDtypeStruct((B, rows, LANES), F32),
        scratch_shapes=[pltpu.VMEM((m * SUBLANES, LANES), F32)] * 4,
        compiler_params=_cp(("arbitrary", "arbitrary")),
        name="expert_ffn",
    )(idx_flat, hcm, gate, lp["w_gate"], lp["w_up"], lp["w_down"])


def _moe_residual_kernel(x_ref, g_ref, m_ref, o_ref):
    tm = x_ref.shape[1]
    chunks = x_ref.shape[2] // LANES
    moe = jnp.concatenate([m_ref[0, pl.ds(c, tm, stride=SUBLANES), :] for c in range(chunks)], axis=1)
    o_ref[0] = x_ref[0] + g_ref[0] * moe


def _moe_residual_call(xs, gate, moe_cm):
    B, n, D = xs.shape
    tm = min(n, STREAM_ROWS)
    return pl.pallas_call(
        _moe_residual_kernel,
        grid=(B, n // tm),
        in_specs=[pl.BlockSpec((1, tm, D), lambda b, i: (b, i, 0)), _mod_spec(gate, 0),
                  pl.BlockSpec((1, tm * (D // LANES), LANES), lambda b, i: (b, i, 0))],
        out_specs=pl.BlockSpec((1, tm, D), lambda b, i: (b, i, 0)),
        out_shape=jax.ShapeDtypeStruct((B, n, D), F32),
        compiler_params=_cp(("arbitrary", "arbitrary")),
        name="moe_residual",
    )(xs, gate[0], moe_cm)


def _axial_tables(n, head_pat, half, reps, roped):
    width = reps * len(head_pat)
    if not roped:
        return jnp.ones((n, width), F32), jnp.zeros((n, width), F32)
    pats = list(head_pat) * reps
    on_col = np.array([p is not None and p[0] == 1 for p in pats])
    freq = np.array([0 if p is None else p[1] for p in pats])
    sign = np.array([0.0 if p is None else (-1.0 if p[2] else 1.0) for p in pats], np.float32)
    t = jnp.arange(n)
    row = jnp.floor_divide(t, GRID_W).astype(F32)[:, None]
    col = jnp.remainder(t, GRID_W).astype(F32)[:, None]
    inv = (ROPE_THETA ** (-jnp.arange(half, dtype=F32) / half))[freq][None, :]
    ang = jnp.where(on_col[None, :], col, row) * inv
    return jnp.where(sign[None, :] == 0.0, 1.0, jnp.cos(ang)), sign[None, :] * jnp.sin(ang)


def _partner_matrix(head_pat, half, reps):
    hw = len(head_pat)
    p = np.zeros((hw * reps, hw * reps), np.float32)
    for r in range(reps):
        for j, pat in enumerate(head_pat):
            if pat is not None:
                p[r * hw + (j + half if pat[2] else j - half), r * hw + j] = 1.0
    return jnp.asarray(p, dtype=BF16)


def _rope_patterns():
    gh = GQA_HD // 4
    gpat = [(j // (2 * gh), j % gh, (j % (2 * gh)) < gh) for j in range(GQA_HD)]
    mh = MLA_ROPE // 4
    mpat = [None] * MLA_NOPE + [(j // (2 * mh), j % mh, (j % (2 * mh)) < mh) for j in range(MLA_ROPE)]
    mpat += [None] * (MLA_PAD - MLA_HD)
    return gpat, gh, mpat, mh


def _rope_tables(n, roped):
    gpat, gh, mpat, mh = _rope_patterns()
    return _axial_tables(n, gpat, gh, GQA_HEADS, roped) + _axial_tables(n, mpat, mh, MLA_HEADS, roped)


def _rope_partners():
    gpat, gh, mpat, mh = _rope_patterns()
    return _partner_matrix(gpat, gh, GQA_HEADS), _partner_matrix(mpat, mh, 2)


def _hyena_consts(n, ch):
    f32 = F32
    tb = min(n, HY_BLOCK)
    k = jnp.arange(tb, dtype=I32)
    ang = jnp.remainder(k[:, None] * k[None, :], 2 * tb).astype(f32) * (math.pi / tb)
    c32, s32 = jnp.cos(ang), jnp.sin(ang)
    c_hi, s_hi = c32.astype(BF16), s32.astype(BF16)
    c_lo, s_lo = (c32 - c_hi.astype(f32)).astype(BF16), (s32 - s_hi.astype(f32)).astype(BF16)
    t = jnp.linspace(0.0, 1.0, n, dtype=f32)[:, None]
    lag = jnp.arange(n, dtype=f32)[:, None]
    bands = jnp.linspace(1e-4, HY_BANDS - 1, HY_BANDS, dtype=f32)[None, :]
    a = (2.0 * math.pi / n) * lag * bands
    feats = jnp.concatenate([t, jnp.cos(a), -jnp.sin(a)], axis=-1)
    feats = jnp.pad(feats, ((0, 0), (0, LANES - feats.shape[1])))
    deltas = jnp.abs(jnp.linspace(math.log(HY_DECAY_TARGET) / HY_SLOW_PCT,
                                  math.log(HY_DECAY_TARGET) / HY_FAST_PCT, ch, dtype=f32))
    decay = jnp.exp(-t * deltas[None, :])
    return {"n": n, "t": tb, "c16": c_hi, "s16": s_hi, "c_hi": c_hi, "c_lo": c_lo, "s_hi": s_hi, "s_lo": s_lo,
            "feats": feats, "decay": decay}


def _seg_matrix(width, seg):
    i = np.arange(width)
    return jnp.asarray((i[:, None] // seg == i[None, :] // seg).astype(np.float32), dtype=BF16)


def _router_split(rw):
    d = rw.shape[0]
    hi = rw.astype(BF16)
    lo = (rw - hi.astype(F32)).astype(BF16)
    top = jnp.concatenate([hi, lo, jnp.zeros((d, LANES - 2 * N_EXPERTS), BF16)], axis=1)
    bot = jnp.concatenate([hi, jnp.zeros((d, LANES - N_EXPERTS), BF16)], axis=1)
    return jnp.concatenate([top, bot], axis=0)


def _layer_params(l, w):
    D = w["w_in"].shape[1]
    wi = w["w_in"][l]
    z = lambda c: jnp.zeros((D, c), F32)
    w_in_r = jnp.concatenate(
        [wi[:, 0:384], z(MLA_NOPE), wi[:, 384:416], z(MLA_PAD - MLA_HD), wi[:, 416:]], axis=1).astype(BF16)
    ukv = w["mla_w_ukv"][l].reshape(-1, MLA_HEADS, MLA_NOPE + MLA_V)
    kpart = jnp.pad(ukv[:, :, :MLA_NOPE], ((0, 0), (0, 0), (0, MLA_PAD - MLA_NOPE)))
    w_ukv_r = jnp.concatenate([kpart.reshape(ukv.shape[0], -1), ukv[:, :, MLA_NOPE:].reshape(ukv.shape[0], -1)],
                              axis=1).astype(BF16)
    uq = w["mla_w_uq"][l].reshape(-1, MLA_HEADS, MLA_HD)
    w_uq_r = jnp.pad(uq, ((0, 0), (0, 0), (0, MLA_PAD - MLA_HD))).reshape(uq.shape[0], -1).astype(BF16)
    padg = lambda g: jnp.tile(jnp.pad(g, (0, MLA_PAD - MLA_HD)), MLA_HEADS)[None, :]
    pw = w["pool_w"][l]
    gc = pw.shape[1]
    pool_bd = jnp.zeros((len(POOL_WINDOWS) * gc,) * 2, F32)
    for gi in range(len(POOL_WINDOWS)):
        pool_bd = pool_bd.at[gi * gc:(gi + 1) * gc, gi * gc:(gi + 1) * gc].set(pw[gi])
    nf = w["hy_f_w1"].shape[1]
    perm_g, perm_m = _rope_partners()
    return {
        "perm_g": perm_g, "perm_m": perm_m,
        "norm1_g": w["norm1_g"][l][None, :], "norm2_g": w["norm2_g"][l][None, :],
        "w_in_r": w_in_r, "w_ukv_r": w_ukv_r, "w_uq_r": w_uq_r,
        "g_k": jnp.tile(w["gqa_knorm_g"][l], GQA_KV_HEADS)[None, :],
        "g_q": jnp.tile(w["gqa_qnorm_g"][l], GQA_HEADS)[None, :],
        "g_ckv": w["mla_ckv_g"][l][None, :], "g_cq": w["mla_cq_g"][l][None, :],
        "g_mk": padg(w["mla_knorm_g"][l]), "g_mq": padg(w["mla_qnorm_g"][l]),
        "seg": _seg_matrix(GQA_HEADS * GQA_HD, GQA_HD),
        "pool_bd": pool_bd.astype(BF16), "pool_scale": w["pool_scale"][l][None, :],
        "hy_conv_w": w["hy_conv_w"][l], "hy_conv_b": w["hy_conv_b"][l][None, :],
        "hy_w1": jnp.pad(w["hy_f_w1"][l], ((0, LANES - nf), (0, 0))), "hy_b1": w["hy_f_b1"][l][None, :],
        "hy_w2": w["hy_f_w2"][l], "hy_b2": w["hy_f_b2"][l][None, :], "hy_w3": w["hy_f_w3"][l],
        "hy_freq": w["hy_freq"][l][None, :], "hy_bias": w["hy_bias"][l][:, None, :],
        "w_out": w["w_out"][l].reshape(4, -1, D).astype(BF16),
        "router_split": _router_split(w["router_w"][l]),
        "w_gate": w["exp_w_gate"][l].astype(BF16), "w_up": w["exp_w_up"][l].astype(BF16),
        "w_down": w["exp_w_down"][l].astype(BF16),
    }


def _moe(xs, sh, sc, lp, consts):
    B, n, _ = xs.shape
    cap = EC_CAPACITY * n // N_EXPERTS
    hcm, idx_t, gate_t = _route_call(xs, sh, sc, lp, consts)
    unfold = lambda a: a[:, :, :SLOT_RADIX].reshape(B, N_EXPERTS, SLOT_RADIX * SLOT_RADIX)[:, :, :cap]
    idx_flat = unfold(idx_t).reshape(-1)
    gts = jnp.pad(jnp.swapaxes(unfold(gate_t), 1, 2), ((0, 0), (0, 0), (0, LANES - N_EXPERTS)))
    return _ffn_call(idx_flat, hcm, gts, lp, cap)


def kernel(x, c, ctx, c_ctx, norm1_g, norm2_g, w_mod, b_mod, w_in, w_out, pool_w, pool_scale, gqa_qnorm_g, gqa_knorm_g, hy_conv_w, hy_conv_b, hy_f_w1, hy_f_b1, hy_f_w2, hy_f_b2, hy_f_w3, hy_freq, hy_bias, mla_cq_g, mla_w_uq, mla_ckv_g, mla_w_ukv, mla_qnorm_g, mla_knorm_g, router_w, exp_w_gate, exp_w_up, exp_w_down):
    w = dict(norm1_g=norm1_g, norm2_g=norm2_g, w_in=w_in, w_out=w_out, pool_w=pool_w, pool_scale=pool_scale,
             gqa_qnorm_g=gqa_qnorm_g, gqa_knorm_g=gqa_knorm_g, hy_conv_w=hy_conv_w, hy_conv_b=hy_conv_b,
             hy_f_w1=hy_f_w1, hy_f_b1=hy_f_b1, hy_f_w2=hy_f_w2, hy_f_b2=hy_f_b2, hy_f_w3=hy_f_w3, hy_freq=hy_freq,
             hy_bias=hy_bias, mla_cq_g=mla_cq_g, mla_w_uq=mla_w_uq, mla_ckv_g=mla_ckv_g, mla_w_ukv=mla_w_ukv,
             mla_qnorm_g=mla_qnorm_g, mla_knorm_g=mla_knorm_g, router_w=router_w, exp_w_gate=exp_w_gate,
             exp_w_up=exp_w_up, exp_w_down=exp_w_down)
    B, n, D = x.shape
    nc = ctx.shape[1]
    depth = w_mod.shape[0]
    ch = hy_bias.shape[2]
    rows = -(-(B + 1) // SUBLANES) * SUBLANES
    cc = jnp.concatenate([c, c_ctx[None, :], jnp.zeros((rows - B - 1, D), F32)], axis=0)
    mod3 = _mod_call(cc, w_mod, b_mod).reshape(depth * rows * MOD_PARTS, 1, D)
    tabs_x = _rope_tables(n, True)
    tabs_c = _rope_tables(nc, False)
    hc_x = _hyena_consts(n, ch)
    hc_c = _hyena_consts(nc, ch)
    tri = _route_consts()
    xc, moe_c = ctx, None
    moe_x = None
    for l in range(depth):
        last = l == depth - 1
        lp = _layer_params(l, w)
        mx = [_mod_vec(mod3, l, rows, i, 0, True) for i in range(MOD_PARTS)]
        mc = [_mod_vec(mod3, l, rows, i, B, False) for i in range(MOD_PARTS)]
        pc = _inproj_call(xc, mc[0], mc[1], lp, tabs_c, last, moe_c)
        kc, vc, mkc, mvc = pc[:4]
        px = _inproj_call(x, mx[0], mx[1], lp, tabs_x, False, moe_x)
        kx, vx, mkx, mvx, qx, mqx, poolx, hyx = px[:8]
        if moe_x is not None:
            x = px[8]
        o_gqa = _attn_call(qx, [kx, kc], [vx, vc], GQA_HEADS, GQA_KV_HEADS, GQA_HD, "attn_gqa")
        o_mla = _attn_call(mqx, [mkx, mkc], [mvx, mvc], MLA_HEADS, MLA_HEADS, MLA_V, "attn_mla")
        o_pool = _pool_call(poolx, lp)
        o_hy = _hyena_call(hyx, lp, hc_x, _hyfilt_call(lp, hc_x))
        x = _outproj_call(x, mx[2], [o_pool, o_gqa, o_hy, o_mla], lp)
        moe_x = (_moe(x, mx[3], mx[4], lp, tri), mx[5])
        if not last:
            qc, mqc, poolc, hyc = pc[4:8]
            if moe_c is not None:
                xc = pc[8]
            oc_gqa = _attn_call(qc, [kc], [vc], GQA_HEADS, GQA_KV_HEADS, GQA_HD, "attn_gqa_ctx")
            oc_mla = _attn_call(mqc, [mkc], [mvc], MLA_HEADS, MLA_HEADS, MLA_V, "attn_mla_ctx")
            oc_pool = _pool_call(poolc, lp)
            oc_hy = _hyena_call(hyc, lp, hc_c, _hyfilt_call(lp, hc_c))
            xc = _outproj_call(xc, mc[2], [oc_pool, oc_gqa, oc_hy, oc_mla], lp)
            moe_c = (_moe(xc, mc[3], mc[4], lp, tri), mc[5])
    return _moe_residual_call(x, moe_x[1], moe_x[0])
```

```python
import functools
import math

import numpy as np
import jax
import jax.numpy as jnp
from jax import lax
from jax.experimental import pallas as pl
from jax.experimental.pallas import tpu as pltpu

F32 = jnp.float32
BF16 = jnp.bfloat16
I32 = jnp.int32

GRID_W = 64
ROPE_THETA = 10000.0
EPS = 1e-6
POOL_WINDOWS = (2, 4, 8, 16)
GQA_HEADS = 4
GQA_KV_HEADS = 2
GQA_HD = 64
HY_BANDS = 8
HY_DECAY_TARGET = 1e-2
HY_FAST_PCT = 0.3
HY_SLOW_PCT = 1.5
MLA_HEADS = 4
MLA_NOPE = 64
MLA_ROPE = 32
MLA_V = 64
MLA_HD = MLA_NOPE + MLA_ROPE
N_EXPERTS = 16
EC_CAPACITY = 2

LANES = 128
SUBLANES = 8
MLA_PAD = LANES
VMEM_LIMIT = 56 * 2 ** 20
HIGHEST = lax.Precision.HIGHEST


def _cp(sem, vmem=VMEM_LIMIT):
    return pltpu.CompilerParams(dimension_semantics=sem, vmem_limit_bytes=vmem)


def _dot(a, b):
    return jnp.dot(a, b, preferred_element_type=F32)


def _dot_nt(a, b):
    return lax.dot_general(a, b, (((1,), (1,)), ((), ())), preferred_element_type=F32)


def _dot_split(x, m):
    hi = x.astype(BF16)
    lo = (x - hi.astype(F32)).astype(BF16)
    return _dot(hi, m) + _dot(lo, m)


def _rms_rows(x):
    return x * lax.rsqrt(jnp.mean(x * x, axis=-1, keepdims=True) + EPS)


def _rope(y, cos, sin_s, perm):
    pw = perm.shape[0]
    yb = y.astype(BF16)
    partner = jnp.concatenate([_dot(yb[:, j:j + pw], perm) for j in range(0, y.shape[1], pw)], axis=1)
    return y * cos + partner * sin_s


def _mod_kernel(c_ref, w_ref, b_ref, o_ref):
    c = c_ref[...]
    s = c * (1.0 / (1.0 + jnp.exp(-c)))
    o_ref[0] = jnp.dot(s, w_ref[0], precision=HIGHEST, preferred_element_type=F32) + b_ref[0]


def _mod_call(cc, w_mod, b_mod):
    L, D, N = w_mod.shape
    R = cc.shape[0]
    tn = 1536
    return pl.pallas_call(
        _mod_kernel,
        grid=(L, N // tn),
        in_specs=[
            pl.BlockSpec((R, D), lambda l, j: (0, 0)),
            pl.BlockSpec((1, D, tn), lambda l, j: (l, 0, j)),
            pl.BlockSpec((1, 1, tn), lambda l, j: (l, 0, j)),
        ],
        out_specs=pl.BlockSpec((1, R, tn), lambda l, j: (l, 0, j)),
        out_shape=jax.ShapeDtypeStruct((L, R, N), F32),
        compiler_params=_cp(("arbitrary", "arbitrary")),
        name="adaln_mod",
    )(cc, w_mod, b_mod.reshape(L, 1, N))


MOD_PARTS = 6
STREAM_ROWS = 2048


def _mod_vec(mod3, layer, rows, part, sample0, per_sample):
    return mod3, (layer * rows + sample0) * MOD_PARTS + part, MOD_PARTS if per_sample else 0


def _mod_spec(mv, batch_axis):
    arr, base, stride = mv
    return pl.BlockSpec((1, 1, arr.shape[2]), lambda *g: (base + stride * g[batch_axis], 0, 0))


KV_COLS = 512
ALL_COLS = 2048
INPROJ_ROWS = 1024
INPROJ_ROWS_FUSED = 512
INPROJ_SUB_ROWS = 256


def _inproj_kernel(kv_only, with_moe, x_ref, sh_ref, sc_ref, g1_ref, w_ref,
                   gk_ref, gq_ref, gckv_ref, wukv_ref, gmk_ref, gcq_ref, wuq_ref, gmq_ref,
                   seg_ref, pg_ref, pm_ref, cg_ref, sg_ref, cm_ref, sm_ref, *rest):
    if with_moe:
        moe_ref, g2_ref, outs = rest[0], rest[1], rest[2:]
        if not kv_only:
            outs, xo_ref = outs[:-1], outs[-1]
    else:
        outs = rest
    tm = x_ref.shape[1]
    sub = min(tm, INPROJ_SUB_ROWS)
    chunks = x_ref.shape[2] // LANES
    projs = []
    for r0 in range(0, tm, sub):
        x = x_ref[0, r0:r0 + sub, :]
        if with_moe:
            moe = jnp.concatenate([moe_ref[0, pl.ds(r0 * chunks + c, sub, stride=chunks), :] for c in range(chunks)],
                                  axis=1)
            x = x + g2_ref[0] * moe
            if not kv_only:
                xo_ref[0, r0:r0 + sub, :] = x
        h = _rms_rows(x) * g1_ref[...] * (1.0 + sc_ref[0]) + sh_ref[0]
        projs.append((slice(r0, r0 + sub), _dot(h.astype(BF16), w_ref[...])))
    for rows, p in projs:
        _inproj_heads(kv_only, rows, p, gk_ref, gq_ref, gckv_ref, wukv_ref, gmk_ref, gcq_ref, wuq_ref, gmq_ref,
                      seg_ref, pg_ref, pm_ref, cg_ref, sg_ref, cm_ref, sm_ref, *outs)


def _inproj_heads(kv_only, rows, p, gk_ref, gq_ref, gckv_ref, wukv_ref, gmk_ref, gcq_ref, wuq_ref, gmq_ref,
                  seg_ref, pg_ref, pm_ref, cg_ref, sg_ref, cm_ref, sm_ref, *outs):
    if kv_only:
        k_ref, v_ref, mk_ref, mv_ref = outs
    else:
        k_ref, v_ref, mk_ref, mv_ref, q_ref, mq_ref, pool_ref, hy_ref = outs

    def seg_norm(y, seg):
        ss = _dot((y * y).astype(BF16), seg)
        return y * lax.rsqrt(ss * (1.0 / GQA_HD) + EPS)

    def head_norm(y):
        parts = []
        for hh in range(MLA_HEADS):
            s = y[:, hh * MLA_PAD:(hh + 1) * MLA_PAD]
            ms = jnp.sum(s * s, axis=-1, keepdims=True) * (1.0 / MLA_HD)
            parts.append(s * lax.rsqrt(ms + EPS))
        return jnp.concatenate(parts, axis=1)

    def put_heads(ref, y, width):
        for hh in range(ref.shape[1]):
            ref[0, hh, rows, :] = y[:, hh * width:(hh + 1) * width].astype(ref.dtype)

    def put_values_t(ref, v, dv):
        vt = v.T
        dve = dv + ATTN_ONES_ROWS
        for hh in range(v.shape[1] // dv):
            ref[0, hh * dve:hh * dve + dv, rows] = vt[hh * dv:(hh + 1) * dv, :].astype(ref.dtype)
            ref[0, hh * dve + dv:(hh + 1) * dve, rows] = jnp.ones((ATTN_ONES_ROWS, v.shape[0]), ref.dtype)

    cg, sg = cg_ref[rows, :], sg_ref[rows, :]
    cm, sm, pm = cm_ref[rows, :], sm_ref[rows, :], pm_ref[...]
    kk = seg_norm(p[:, 0:128], seg_ref[0:128, 0:128]) * gk_ref[...]
    kk = _rope(kk, cg[:, 0:128], sg[:, 0:128], pg_ref[0:128, 0:128])
    put_heads(k_ref, kk, GQA_HD)
    put_values_t(v_ref, p[:, 128:256], GQA_HD)
    ckv = _rms_rows(p[:, 256:384]) * gckv_ref[...]
    kv = _dot(ckv.astype(BF16), wukv_ref[...])
    kr = p[:, 384:512]
    kraw = kv[:, 0:MLA_HEADS * MLA_PAD] + jnp.concatenate([kr] * MLA_HEADS, axis=1)
    mk = _rope(head_norm(kraw) * gmk_ref[...], cm, sm, pm)
    put_heads(mk_ref, mk, MLA_PAD)
    put_values_t(mv_ref, kv[:, MLA_HEADS * MLA_PAD:], MLA_V)
    if kv_only:
        return
    qq = seg_norm(p[:, 512:768], seg_ref[...]) * gq_ref[...]
    qq = _rope(qq, cg, sg, pg_ref[...])
    put_heads(q_ref, qq * (GQA_HD ** -0.5 * LOG2E), GQA_HD)
    cq = _rms_rows(p[:, 768:1024]) * gcq_ref[...]
    mq = _rope(head_norm(_dot(cq.astype(BF16), wuq_ref[...])) * gmq_ref[...], cm, sm, pm)
    put_heads(mq_ref, mq * (MLA_HD ** -0.5 * LOG2E), MLA_PAD)
    pool_ref[0, rows, :] = p[:, 1024:1280]
    hy_ref[0, rows, :] = p[:, 1280:2048]


def _inproj_call(xs, sh, sc, lp, tabs, kv_only, moe=None):
    B, n, D = xs.shape
    tm = min(n, INPROJ_ROWS if moe is None else INPROJ_ROWS_FUSED)
    cols = KV_COLS if kv_only else ALL_COLS
    w = lp["w_in_r"][:, :cols]
    full = lambda a: pl.BlockSpec(a.shape, lambda i, b: (0,) * a.ndim)
    tab = lambda a: pl.BlockSpec((tm, a.shape[1]), lambda i, b: (i, 0))
    row = lambda width: pl.BlockSpec((1, tm, width), lambda i, b: (b, i, 0))
    params = [lp["g_k"], lp["g_q"], lp["g_ckv"], lp["w_ukv_r"], lp["g_mk"], lp["g_cq"], lp["w_uq_r"], lp["g_mq"],
              lp["seg"], lp["perm_g"], lp["perm_m"]]
    ones = ATTN_ONES_ROWS
    heads = lambda h, d: (jax.ShapeDtypeStruct((B, h, n, d), BF16), pl.BlockSpec((1, h, tm, d), lambda i, b: (b, 0, i, 0)))
    vals_t = lambda h, d: (jax.ShapeDtypeStruct((B, h * (d + ones), n), BF16),
                           pl.BlockSpec((1, h * (d + ones), tm), lambda i, b: (b, 0, i)))
    rows = lambda d: (jax.ShapeDtypeStruct((B, n, d), F32), row(d))
    outs = [heads(GQA_KV_HEADS, GQA_HD), vals_t(GQA_KV_HEADS, GQA_HD), heads(MLA_HEADS, MLA_PAD), vals_t(MLA_HEADS, MLA_V)]
    if not kv_only:
        outs += [heads(GQA_HEADS, GQA_HD), heads(MLA_HEADS, MLA_PAD), rows(256), rows(768)]
    extra_specs, extra_args = [], []
    if moe is not None:
        extra_specs = [pl.BlockSpec((1, tm * (D // LANES), LANES), lambda i, b: (b, i, 0)), _mod_spec(moe[1], 1)]
        extra_args = [moe[0], moe[1][0]]
        if not kv_only:
            outs.append(rows(D))
    return pl.pallas_call(
        functools.partial(_inproj_kernel, kv_only, moe is not None),
        grid=(n // tm, B),
        in_specs=[row(D), _mod_spec(sh, 1), _mod_spec(sc, 1), full(lp["norm1_g"]), full(w)]
        + [full(a) for a in params] + [tab(a) for a in tabs] + extra_specs,
        out_specs=[spec for _, spec in outs],
        out_shape=[shape for shape, _ in outs],
        compiler_params=_cp(("arbitrary", "arbitrary")),
        name="inproj_kv" if kv_only else "inproj",
    )(xs, sh[0], sc[0], lp["norm1_g"], w, *params, *tabs, *extra_args)


ATTN_KEY_BLOCK = 256
ATTN_Q_TILE = 1024
ATTN_ONES_ROWS = 16
LOG2E = math.log2(math.e)


def _attn_kernel(heads, kv_heads, dv, nseg, q_ref, *refs):
    k_refs, vt_refs, o_ref = refs[:nseg], refs[nseg:2 * nseg], refs[2 * nseg]
    grp = heads // kv_heads
    kb = ATTN_KEY_BLOCK
    dve = dv + ATTN_ONES_ROWS
    blocks = [(i, r) for i in range(nseg) for r in range(0, k_refs[i].shape[2], kb)]

    qs = [q_ref[0, hh] for hh in range(heads)]
    run_max = [None] * heads
    acc = [None] * heads
    s_prev = [None] * heads
    for j in range(len(blocks) + 1):
        s_cur = [None] * heads
        for hh in range(heads):
            g = hh // grp
            if j < len(blocks):
                i, r = blocks[j]
                s_cur[hh] = _dot_nt(k_refs[i][0, g, r:r + kb, :], qs[hh])
            if j >= 1:
                i, r = blocks[j - 1]
                bm = jnp.max(s_prev[hh], axis=0, keepdims=True)
                m_new = bm if run_max[hh] is None else jnp.maximum(run_max[hh], bm)
                t = _dot(vt_refs[i][0, g * dve:(g + 1) * dve, r:r + kb], jnp.exp2(s_prev[hh] - m_new).astype(BF16))
                acc[hh] = t if acc[hh] is None else acc[hh] * jnp.exp2(run_max[hh] - m_new) + t
                run_max[hh] = m_new
        s_prev = s_cur
    o_ref[0] = jnp.concatenate([a[0:dv] / a[dv:dv + 1] for a in acc], axis=0).T.astype(o_ref.dtype)


def _attn_call(q, ks, vts, heads, kv_heads, dv, name):
    B, _, n, dk = q.shape
    tq = min(n, ATTN_Q_TILE)
    kspec = lambda a: pl.BlockSpec((1,) + a.shape[1:], lambda b, i: (b,) + (0,) * (a.ndim - 1))
    return pl.pallas_call(
        functools.partial(_attn_kernel, heads, kv_heads, dv, len(ks)),
        grid=(B, n // tq),
        in_specs=[pl.BlockSpec((1, heads, tq, dk), lambda b, i: (b, 0, i, 0))]
        + [kspec(a) for a in ks] + [kspec(a) for a in vts],
        out_specs=pl.BlockSpec((1, tq, heads * dv), lambda b, i: (b, i, 0)),
        out_shape=jax.ShapeDtypeStruct((B, n, heads * dv), BF16),
        compiler_params=_cp(("arbitrary", "arbitrary")),
        name=name,
    )(q, *ks, *vts)


POOL_HALO = 8
assert all(b == 2 * a for a, b in zip(POOL_WINDOWS, POOL_WINDOWS[1:])) and POOL_WINDOWS[0] == 2
assert POOL_WINDOWS[-1] // 2 <= POOL_HALO


def _pool_kernel(n, u_ref, cnt_ref, wbd_ref, scale_ref, o_ref, ue_ref):
    w = u_ref.shape[2]
    gc = w // len(POOL_WINDOWS)
    ne = n + 2 * POOL_HALO
    u = u_ref[0]
    ue_ref[0:POOL_HALO, :] = jnp.zeros((POOL_HALO, w), F32)
    ue_ref[POOL_HALO:POOL_HALO + n, :] = u
    ue_ref[POOL_HALO + n:ne, :] = jnp.zeros((POOL_HALO, w), F32)
    ue = ue_ref[...]
    down = lambda x, k: pltpu.roll(x, k, axis=0)
    up = lambda x, k: pltpu.roll(x, ne - k, axis=0)
    sums = [down(ue, 1) + ue]
    for win in POOL_WINDOWS[:-1]:
        sums.append(down(sums[-1], win // 2) + up(sums[-1], win // 2))
    lane = lax.broadcasted_iota(I32, (n, w), 1)
    total = sums[-1][POOL_HALO:POOL_HALO + n]
    for gi in range(len(POOL_WINDOWS) - 2, -1, -1):
        total = jnp.where(lane < (gi + 1) * gc, sums[gi][POOL_HALO:POOL_HALO + n], total)
    d = total / cnt_ref[...] - u
    o_ref[0] = (_dot(d.astype(BF16), wbd_ref[...]) * scale_ref[...]).astype(o_ref.dtype)


def _pool_counts(n, width):
    t = np.arange(n)
    gc = width // len(POOL_WINDOWS)
    cols = []
    for win in POOL_WINDOWS:
        left = win // 2
        right = win - 1 - left
        cnt = np.minimum(t + right, n - 1) - np.maximum(t - left, 0) + 1
        cols.append(np.repeat(cnt[:, None], gc, axis=1))
    return jnp.asarray(np.concatenate(cols, axis=1), dtype=F32)


def _pool_call(u, lp):
    B, n, W = u.shape
    return pl.pallas_call(
        functools.partial(_pool_kernel, n),
        grid=(B,),
        in_specs=[pl.BlockSpec((1, n, W), lambda b: (b, 0, 0)),
                  pl.BlockSpec((n, W), lambda b: (0, 0)),
                  pl.BlockSpec((W, W), lambda b: (0, 0)),
                  pl.BlockSpec((1, W), lambda b: (0, 0))],
        out_specs=pl.BlockSpec((1, n, W), lambda b: (b, 0, 0)),
        out_shape=jax.ShapeDtypeStruct((B, n, W), BF16),
        scratch_shapes=[pltpu.VMEM((n + 2 * POOL_HALO, W), F32)],
        compiler_params=_cp(("arbitrary",)),
        name="pool",
    )(u, _pool_counts(n, W), lp["pool_bd"], lp["pool_scale"])


HY_BLOCK = 512
HY_SPEC_DTYPE = F32


def _dot3(a_hi, a_lo, x):
    x_hi = x.astype(BF16)
    x_lo = (x - x_hi.astype(F32)).astype(BF16)
    return _dot(a_hi, x_hi) + _dot(a_hi, x_lo) + _dot(a_lo, x_hi)


def _hyfilt_kernel(n, t, feats_ref, w1_ref, b1_ref, w2_ref, b2_ref, w3_ref, fr_ref, decay_ref,
                   chi_ref, clo_ref, shi_ref, slo_ref, hre_ref, him_ref, hn_ref, tc_ref, ts_ref, edge_ref):
    di = pl.program_id(0)
    nb = n // t
    hc2 = w3_ref.shape[1] // 2

    @pl.when(di == 0)
    def _():
        fr = fr_ref[...]
        hid = jnp.sin(fr * (jnp.dot(feats_ref[...], w1_ref[...], precision=HIGHEST, preferred_element_type=F32)
                            + b1_ref[...]))
        hid = jnp.sin(fr * (jnp.dot(hid, w2_ref[...], precision=HIGHEST, preferred_element_type=F32) + b2_ref[...]))
        h = jnp.dot(hid, w3_ref[...], precision=HIGHEST, preferred_element_type=F32)
        dec = decay_ref[...]
        dec2 = jnp.concatenate([dec, dec], axis=1)
        row = lax.broadcasted_iota(I32, (n, 1), 0)
        fwd = h[:, 0:hc2] * dec2
        bwd = jnp.where(row == 0, 0.0, h[:, hc2:2 * hc2] * dec2)
        scl = lax.rsqrt(jnp.sum(fwd * fwd + bwd * bwd, axis=0, keepdims=True) + EPS)
        alt = jnp.where(lax.broadcasted_iota(I32, (t, 1), 0) % 2 == 0, 1.0, -1.0)
        for fam, filt in enumerate((fwd * scl, bwd * scl)):
            for j in range(nb):
                x = filt[j * t:(j + 1) * t]
                tc_ref[fam, j] = _dot3(chi_ref[...], clo_ref[...], x)
                ts_ref[fam, j] = _dot3(shi_ref[...], slo_ref[...], x)
                r = 2 * (fam * nb + j)
                edge_ref[r:r + 1, :] = x[0:1, :]
                edge_ref[r + 1:r + 2, :] = jnp.sum(x * alt, axis=0, keepdims=True)

    k = lax.broadcasted_iota(I32, (t, 1), 0)
    wk = jnp.where(k == 0, 1.0 / (2 * t), 2.0 / (2 * t))
    sig = jnp.where(k % 2 == 0, 1.0, -1.0)
    first = lambda fam, j: edge_ref[2 * (fam * nb + j):2 * (fam * nb + j) + 1, :]
    altsum = lambda fam, j: edge_ref[2 * (fam * nb + j) + 1:2 * (fam * nb + j) + 2, :]
    for dd in range(2 * nb - 1):
        d = dd - (nb - 1)

        def put(ref, val):
            for o in range(2):
                ref[o, 0] = val[:, o * (hc2 // 2):(o + 1) * (hc2 // 2)].astype(ref.dtype)

        @pl.when(di == dd)
        def _(d=d):
            if d == 0:
                put(hre_ref, wk * (tc_ref[0, 0] + tc_ref[1, 0]))
                put(him_ref, wk * (ts_ref[1, 0] - ts_ref[0, 0]))
                put(hn_ref, (altsum(0, 0) + altsum(1, 0)) * (1.0 / (2 * t)))
            else:
                fam, e, sgn = (0, d, -1.0) if d > 0 else (1, -d, 1.0)
                put(hre_ref, wk * (sig * (tc_ref[fam, e - 1] - first(fam, e - 1)) + tc_ref[fam, e]))
                put(him_ref, (sgn * wk) * (sig * ts_ref[fam, e - 1] + ts_ref[fam, e]))
                put(hn_ref, (altsum(fam, e - 1) - first(fam, e - 1) + altsum(fam, e)) * (1.0 / (2 * t)))


def _hyfilt_call(lp, hc):
    n, t = hc["n"], hc["t"]
    nb = n // t
    nd = 2 * nb - 1
    hc2 = lp["hy_w3"].shape[1] // 2
    ch = hc2 // 2
    full = lambda a: pl.BlockSpec(a.shape, lambda i: (0,) * a.ndim)
    args = [hc["feats"], lp["hy_w1"], lp["hy_b1"], lp["hy_w2"], lp["hy_b2"], lp["hy_w3"], lp["hy_freq"], hc["decay"],
            hc["c_hi"], hc["c_lo"], hc["s_hi"], hc["s_lo"]]
    return pl.pallas_call(
        functools.partial(_hyfilt_kernel, n, t),
        grid=(nd,),
        in_specs=[full(a) for a in args],
        out_specs=[pl.BlockSpec((2, 1, t, ch), lambda i: (0, i, 0, 0)), pl.BlockSpec((2, 1, t, ch), lambda i: (0, i, 0, 0)),
                   pl.BlockSpec((2, 1, 1, ch), lambda i: (0, i, 0, 0))],
        out_shape=[jax.ShapeDtypeStruct((2, nd, t, ch), HY_SPEC_DTYPE), jax.ShapeDtypeStruct((2, nd, t, ch), HY_SPEC_DTYPE),
                   jax.ShapeDtypeStruct((2, nd, 1, ch), F32)],
        scratch_shapes=[pltpu.VMEM((2, nb, t, hc2), F32), pltpu.VMEM((2, nb, t, hc2), F32),
                        pltpu.VMEM((4 * nb, hc2), F32)],
        compiler_params=_cp(("arbitrary",)),
        name="hyena_filter",
    )(*args)


HY_FREQ_ROWS = 32


def _hyena_kernel(n, t, u_ref, cw_ref, cb_ref, c_ref, s_ref, hre_ref, him_ref, hn_ref, bias_ref, o_ref,
                  uc_ref, z_ref, a_ref, b_ref, p_ref, q_ref):
    ch = o_ref.shape[2]
    nb = n // t
    row = lax.broadcasted_iota(I32, (t, 1), 0)
    alt = jnp.where(row % 2 == 0, 1.0, -1.0)
    sub = lax.broadcasted_iota(I32, (SUBLANES, 1), 0)
    for g in range(3):
        cols = slice(g * ch, (g + 1) * ch)
        for j in range(nb):
            r0 = j * t
            u = u_ref[0, r0:r0 + t, cols]
            prev = jnp.zeros((1, ch), F32) if j == 0 else u_ref[0, r0 - 1:r0, cols]
            nxt = jnp.zeros((1, ch), F32) if j == nb - 1 else u_ref[0, r0 + t:r0 + t + 1, cols]
            up = pltpu.roll(u, 1, axis=0)
            un = pltpu.roll(u, t - 1, axis=0)
            up = jnp.concatenate([jnp.where(sub == 0, prev, up[0:SUBLANES]), up[SUBLANES:]], axis=0)
            un = jnp.concatenate([un[:t - SUBLANES], jnp.where(sub == SUBLANES - 1, nxt, un[t - SUBLANES:])], axis=0)
            uc_ref[g, r0:r0 + t, :] = (up * cw_ref[0:1, cols] + u * cw_ref[1:2, cols] + un * cw_ref[2:3, cols]
                                       + cb_ref[:, cols])

    z_ref[...] = uc_ref[0]

    def long_conv(o, carry):
        nyq = []
        for j in range(nb):
            z = z_ref[j * t:(j + 1) * t, :]
            zb = z.astype(BF16)
            a_ref[j] = _dot(c_ref[...], zb).astype(a_ref.dtype)
            b_ref[j] = _dot(s_ref[...], zb).astype(b_ref.dtype)
            nyq.append(jnp.sum(z * alt, axis=0, keepdims=True))
        for r0 in range(0, t, HY_FREQ_ROWS):
            rows = slice(r0, r0 + HY_FREQ_ROWS)
            ab = [(a_ref[j, rows, :], b_ref[j, rows, :]) for j in range(nb)]
            for i in range(nb):
                p = q = None
                for j, (a, b) in enumerate(ab):
                    d = i - j + nb - 1
                    hre, him = hre_ref[o, d, rows, :], him_ref[o, d, rows, :]
                    tp = a * hre + b * him
                    tq = b * hre - a * him
                    p = tp if p is None else p + tp
                    q = tq if q is None else q + tq
                p_ref[i, rows, :] = p.astype(BF16)
                q_ref[i, rows, :] = q.astype(BF16)
        for i in range(nb):
            yn = functools.reduce(jnp.add, [nyq[j] * hn_ref[o, i - j + nb - 1] for j in range(nb)])
            y = _dot(c_ref[...], p_ref[i]) + _dot(s_ref[...], q_ref[i])
            blk = slice(i * t, (i + 1) * t)
            z_ref[blk, :] = uc_ref[o + 1, blk, :] * (y + alt * yn + z_ref[blk, :] * bias_ref[o])
        return carry

    lax.fori_loop(0, 2, long_conv, 0)
    o_ref[0] = z_ref[...].astype(o_ref.dtype)


def _hyena_call(u, lp, hc, filt):
    B, n, W = u.shape
    ch = W // 3
    t = hc["t"]
    hre, him, hn = filt
    once = lambda a: pl.BlockSpec(a.shape, lambda b: (0,) * a.ndim, pipeline_mode=pl.Buffered(1))
    return pl.pallas_call(
        functools.partial(_hyena_kernel, n, t),
        scratch_shapes=[pltpu.VMEM((3, n, ch), F32), pltpu.VMEM((n, ch), F32),
                        pltpu.VMEM((n // t, t, ch), HY_SPEC_DTYPE), pltpu.VMEM((n // t, t, ch), HY_SPEC_DTYPE),
                        pltpu.VMEM((n // t, t, ch), BF16), pltpu.VMEM((n // t, t, ch), BF16)],
        grid=(B,),
        in_specs=[pl.BlockSpec((1, n, W), lambda b: (b, 0, 0)),
                  once(lp["hy_conv_w"]), once(lp["hy_conv_b"]), once(hc["c16"]), once(hc["s16"]),
                  once(hre), once(him), once(hn), once(lp["hy_bias"])],
        out_specs=pl.BlockSpec((1, n, ch), lambda b: (b, 0, 0)),
        out_shape=jax.ShapeDtypeStruct((B, n, ch), BF16),
        compiler_params=_cp(("arbitrary",)),
        name="hyena",
    )(u, lp["hy_conv_w"], lp["hy_conv_b"], hc["c16"], hc["s16"], hre, him, hn, lp["hy_bias"])


def _outproj_kernel(x_ref, g_ref, a_ref, b_ref, c_ref, d_ref, w_ref, o_ref):
    mixed = (_dot(a_ref[0], w_ref[0]) + _dot(b_ref[0], w_ref[1])
             + _dot(c_ref[0], w_ref[2]) + _dot(d_ref[0], w_ref[3]))
    o_ref[0] = x_ref[0] + g_ref[0] * mixed


def _outproj_call(xs, gate, parts, lp):
    B, n, D = xs.shape
    tm = min(n, STREAM_ROWS)
    gw = parts[0].shape[2]
    return pl.pallas_call(
        _outproj_kernel,
        grid=(B, n // tm),
        in_specs=[pl.BlockSpec((1, tm, D), lambda b, i: (b, i, 0)), _mod_spec(gate, 0)]
        + [pl.BlockSpec((1, tm, gw), lambda b, i: (b, i, 0))] * 4
        + [pl.BlockSpec(lp["w_out"].shape, lambda b, i: (0, 0, 0))],
        out_specs=pl.BlockSpec((1, tm, D), lambda b, i: (b, i, 0)),
        out_shape=jax.ShapeDtypeStruct((B, n, D), F32),
        compiler_params=_cp(("arbitrary", "arbitrary")),
        name="outproj",
    )(xs, gate[0], *parts, lp["w_out"])


PREFIX_BLOCK = 256


def _prefix_lanes(m, tri):
    e, n = m.shape
    nb = n // PREFIX_BLOCK
    stk = jnp.concatenate([m[:, j * PREFIX_BLOCK:(j + 1) * PREFIX_BLOCK] for j in range(nb)], axis=0)
    pre = _dot(stk.astype(BF16), tri)
    outs = []
    carry = jnp.zeros((e, 1), F32)
    for j in range(nb):
        pj = pre[j * e:(j + 1) * e]
        outs.append(pj + carry)
        carry = carry + pj[:, PREFIX_BLOCK - 1:PREFIX_BLOCK]
    return jnp.concatenate(outs, axis=1)


SLOT_RADIX = 16
RADIX_BITS = 4


def _split3(x):
    a = x.astype(BF16).astype(F32)
    r = x - a
    b = r.astype(BF16).astype(F32)
    c = (r - b).astype(BF16).astype(F32)
    return a, b, c


def _digit_onehots(v, weights=None):
    e_n = v.shape[0]
    hi_d = jnp.floor(v * (1.0 / SLOT_RADIX))
    lo_d = v - SLOT_RADIX * hi_d
    dig = lax.broadcasted_iota(I32, (SLOT_RADIX, v.shape[1]), 0).astype(F32)
    his = [hi_d[e:e + 1, :] == dig for e in range(e_n)]
    lo = jnp.concatenate([jnp.where(lo_d[e:e + 1, :] == dig, 1.0, 0.0) for e in range(e_n)], axis=0)
    if weights is None:
        hi = jnp.concatenate([jnp.where(m, 1.0, 0.0) for m in his], axis=0)
    else:
        hi = jnp.concatenate([jnp.where(m, w[e:e + 1, :], 0.0) for w in weights for e, m in enumerate(his)], axis=0)
    return hi.astype(BF16), lo.astype(BF16)


def _route_kernel(n, cap, x_ref, sh_ref, sc_ref, g_ref, wr_ref, tri_ref, fold_ref, pre_ref, low_ref,
                  hcm_ref, idx_ref, gate_ref):
    x = x_ref[0]
    h = _rms_rows(x) * g_ref[...] * (1.0 + sc_ref[0]) + sh_ref[0]
    for c in range(h.shape[1] // LANES):
        hcm_ref[0, pl.ds(c, n, stride=SUBLANES), :] = h[:, c * LANES:(c + 1) * LANES]
    h_hi = h.astype(BF16)
    h_lo = (h - h_hi.astype(F32)).astype(BF16)
    rt = _dot(jnp.concatenate([h_hi, h_lo], axis=1), wr_ref[...]).T
    lt = rt[0:N_EXPERTS, :] + rt[N_EXPERTS:2 * N_EXPERTS, :]
    ex = jnp.exp(lt - jnp.max(lt, axis=0, keepdims=True))
    aff = ex / jnp.sum(ex, axis=0, keepdims=True)
    bits = pltpu.bitcast(aff, I32)
    thr = jnp.zeros((N_EXPERTS, 1), I32)
    for shift in range(32 - RADIX_BITS, -1, -RADIX_BITS):
        top = (1 << (31 - shift)) if shift + RADIX_BITS > 31 else (1 << RADIX_BITS)
        digit = jnp.zeros((N_EXPERTS, 1), I32)
        for c in range(1, top):
            cnt = jnp.sum(jnp.where(bits >= (thr | (c << shift)), 1.0, 0.0), axis=1, keepdims=True)
            digit = digit + jnp.where(cnt >= cap, 1, 0)
        thr = thr | (digit << shift)
    gt = jnp.where(bits > thr, 1.0, 0.0)
    eq = jnp.where(bits == thr, 1.0, 0.0)
    need = cap - jnp.sum(gt, axis=1, keepdims=True)
    tri = tri_ref[...]
    sel = gt + eq * jnp.where(_prefix_lanes(eq, tri) <= need, 1.0, 0.0)
    incl = _prefix_lanes(sel, tri)
    rows = N_EXPERTS * SLOT_RADIX
    shift = SLOT_RADIX.bit_length() - 1
    same = ((lax.broadcasted_iota(I32, (rows, rows), 0) >> shift)
            == (lax.broadcasted_iota(I32, (rows, rows), 1) >> shift))
    fold = fold_ref[...]
    a_hi, a_lo = _digit_onehots(incl)
    hist = jnp.where(same, _dot_nt(a_hi, a_lo), 0.0)
    hist = _dot_split(hist, fold)
    pre = _dot_split(hist, pre_ref[...])
    tot = jnp.broadcast_to(pre[:, SLOT_RADIX - 1:SLOT_RADIX], pre.shape)
    t_hi = tot.astype(BF16)
    t_lo = (tot - t_hi.astype(F32)).astype(BF16)
    idx_ref[0] = (pre + _dot(low_ref[...], t_hi) + _dot(low_ref[...], t_lo)).astype(I32)
    g_hi, g_lo = _digit_onehots(incl - 1.0, _split3(sel * aff))
    gsum = _dot_nt(g_hi, g_lo)
    gsum = jnp.where(same, gsum[0:rows] + gsum[rows:2 * rows] + gsum[2 * rows:3 * rows], 0.0)
    gate_ref[0] = functools.reduce(jnp.add, [_dot(part.astype(BF16), fold) for part in _split3(gsum)])


def _route_consts():
    rows = N_EXPERTS * SLOT_RADIX
    i = np.arange(rows)
    j = np.arange(LANES)
    fold = (i[:, None] % SLOT_RADIX == j[None, :]).astype(np.float32)
    pre = ((j[:, None] <= j[None, :]) & (j[None, :] < SLOT_RADIX)).astype(np.float32)
    low = ((i[:, None] // SLOT_RADIX == i[None, :] // SLOT_RADIX) & (i[None, :] < i[:, None])).astype(np.float32)
    tri = np.triu(np.ones((PREFIX_BLOCK, PREFIX_BLOCK), np.float32))
    return tuple(jnp.asarray(a, dtype=BF16) for a in (tri, fold, pre, low))


def _route_call(xs, sh, sc, lp, consts):
    B, n, D = xs.shape
    cap = EC_CAPACITY * n // N_EXPERTS
    assert cap <= SLOT_RADIX * SLOT_RADIX
    rows = n * (D // LANES)
    srows = N_EXPERTS * SLOT_RADIX
    full = lambda a: pl.BlockSpec(a.shape, lambda b: (0,) * a.ndim)
    return pl.pallas_call(
        functools.partial(_route_kernel, n, cap),
        grid=(B,),
        in_specs=[pl.BlockSpec((1, n, D), lambda b: (b, 0, 0)), _mod_spec(sh, 0), _mod_spec(sc, 0),
                  full(lp["norm2_g"]), full(lp["router_split"])] + [full(a) for a in consts],
        out_specs=[pl.BlockSpec((1, rows, LANES), lambda b: (b, 0, 0)),
                   pl.BlockSpec((1, srows, LANES), lambda b: (b, 0, 0)),
                   pl.BlockSpec((1, srows, LANES), lambda b: (b, 0, 0))],
        out_shape=[jax.ShapeDtypeStruct((B, rows, LANES), F32),
                   jax.ShapeDtypeStruct((B, srows, LANES), I32),
                   jax.ShapeDtypeStruct((B, srows, LANES), F32)],
        compiler_params=_cp(("arbitrary",)),
        name="route",
    )(xs, sh[0], sc[0], lp["norm2_g"], lp["router_split"], *consts)


SCATTER_GROUP = 4


FFN_ROWS = 256


def _ffn_kernel(cap, bt, idx_ref, h_ref, gate_ref, wg_ref, wu_ref, wd_ref, o_ref, xga_ref, xgb_ref, ya_ref, yb_ref):
    b0 = pl.program_id(0) * bt
    k = pl.program_id(1)
    steps = pl.num_programs(1)
    chunks = wg_ref.shape[1] // LANES
    m = bt * cap
    tile = lambda r: pl.ds(pl.multiple_of(r * SUBLANES, SUBLANES), SUBLANES)
    e0 = 2 * k
    e1 = e0 + 1
    wrap = lambda e: jnp.where(e < 0, e + N_EXPERTS, jnp.where(e >= N_EXPERTS, e - N_EXPERTS, e))

    def gather(e, xg_ref):
        for bi in range(bt):
            base = ((b0 + bi) * N_EXPERTS + e) * cap
            for r in range(cap):
                xg_ref[tile(bi * cap + r), :] = h_ref[bi, tile(idx_ref[base + r]), :]

    def scatter(e, y_ref):
        for bi in range(bt):
            base = ((b0 + bi) * N_EXPERTS + e) * cap
            for r0 in range(0, cap, SCATTER_GROUP):
                upd = []
                for r in range(r0, r0 + SCATTER_GROUP):
                    dst = tile(idx_ref[base + r])
                    upd.append((dst, o_ref[bi, dst, :] + y_ref[tile(bi * cap + r), :]))
                for dst, v in upd:
                    o_ref[bi, dst, :] = v

    def ffn(e, w, xg_ref, y_ref):
        xt = jnp.concatenate([xg_ref[pl.ds(c, m, stride=SUBLANES), :] for c in range(chunks)], axis=1).astype(BF16)
        a = _dot(xt, wg_ref[w])
        u = _dot(xt, wu_ref[w])
        hmid = (a * (1.0 / (1.0 + jnp.exp(-a))) * u).astype(BF16)
        lane = lax.broadcasted_iota(I32, (m, LANES), 1)
        gcol = jnp.sum(jnp.where(lane == e, gate_ref[...].reshape(m, LANES), 0.0), axis=1, keepdims=True)
        y = _dot(hmid, wd_ref[w]) * gcol
        for c in range(chunks):
            y_ref[pl.ds(c, m, stride=SUBLANES), :] = y[:, c * LANES:(c + 1) * LANES]

    @pl.when(k == 0)
    def _():
        o_ref[...] = jnp.zeros_like(o_ref)
        yb_ref[...] = jnp.zeros_like(yb_ref)
        gather(e0, xga_ref)

    ffn(e0, 0, xga_ref, ya_ref)
    gather(e1, xgb_ref)
    scatter(wrap(e0 - 1), yb_ref)
    ffn(e1, 1, xgb_ref, yb_ref)
    gather(wrap(e1 + 1), xga_ref)
    scatter(e0, ya_ref)

    @pl.when(k == steps - 1)
    def _():
        scatter(e1, yb_ref)


def _ffn_call(idx_flat, hcm, gate, lp, cap):
    B, rows, _ = hcm.shape
    D, ff = lp["w_gate"].shape[1:]
    bt = min(B, max(1, FFN_ROWS // cap))
    m = bt * cap
    return pl.pallas_call(
        functools.partial(_ffn_kernel, cap, bt),
        grid=(B // bt, N_EXPERTS // 2),
        in_specs=[pl.BlockSpec(memory_space=pltpu.SMEM),
                  pl.BlockSpec((bt, rows, LANES), lambda b, k: (b, 0, 0)),
                  pl.BlockSpec((bt, cap, LANES), lambda b, k: (b, 0, 0)),
                  pl.BlockSpec((2, D, ff), lambda b, k: (k, 0, 0)),
                  pl.BlockSpec((2, D, ff), lambda b, k: (k, 0, 0)),
                  pl.BlockSpec((2, ff, D), lambda b, k: (k, 0, 0))],
        out_specs=pl.BlockSpec((bt, rows, LANES), lambda b, k: (b, 0, 0)),
        out_shape=jax.ShapeDtypeStruct((B, rows, LANES), F32),
        scratch_shapes=[pltpu.VMEM((m * SUBLANES, LANES), F32)] * 4,
        compiler_params=_cp(("arbitrary", "arbitrary")),
        name="expert_ffn",
    )(idx_flat, hcm, gate, lp["w_gate"], lp["w_up"], lp["w_down"])


def _moe_residual_kernel(x_ref, g_ref, m_ref, o_ref):
    tm = x_ref.shape[1]
    chunks = x_ref.shape[2] // LANES
    moe = jnp.concatenate([m_ref[0, pl.ds(c, tm, stride=SUBLANES), :] for c in range(chunks)], axis=1)
    o_ref[0] = x_ref[0] + g_ref[0] * moe


def _moe_residual_call(xs, gate, moe_cm):
    B, n, D = xs.shape
    tm = min(n, STREAM_ROWS)
    return pl.pallas_call(
        _moe_residual_kernel,
        grid=(B, n // tm),
        in_specs=[pl.BlockSpec((1, tm, D), lambda b, i: (b, i, 0)), _mod_spec(gate, 0),
                  pl.BlockSpec((1, tm * (D // LANES), LANES), lambda b, i: (b, i, 0))],
        out_specs=pl.BlockSpec((1, tm, D), lambda b, i: (b, i, 0)),
        out_shape=jax.ShapeDtypeStruct((B, n, D), F32),
        compiler_params=_cp(("arbitrary", "arbitrary")),
        name="moe_residual",
    )(xs, gate[0], moe_cm)


def _axial_tables(n, head_pat, half, reps, roped):
    width = reps * len(head_pat)
    if not roped:
        return jnp.ones((n, width), F32), jnp.zeros((n, width), F32)
    pats = list(head_pat) * reps
    on_col = np.array([p is not None and p[0] == 1 for p in pats])
    freq = np.array([0 if p is None else p[1] for p in pats])
    sign = np.array([0.0 if p is None else (-1.0 if p[2] else 1.0) for p in pats], np.float32)
    t = jnp.arange(n)
    row = jnp.floor_divide(t, GRID_W).astype(F32)[:, None]
    col = jnp.remainder(t, GRID_W).astype(F32)[:, None]
    inv = (ROPE_THETA ** (-jnp.arange(half, dtype=F32) / half))[freq][None, :]
    ang = jnp.where(on_col[None, :], col, row) * inv
    return jnp.where(sign[None, :] == 0.0, 1.0, jnp.cos(ang)), sign[None, :] * jnp.sin(ang)


def _partner_matrix(head_pat, half, reps):
    hw = len(head_pat)
    p = np.zeros((hw * reps, hw * reps), np.float32)
    for r in range(reps):
        for j, pat in enumerate(head_pat):
            if pat is not None:
                p[r * hw + (j + half if pat[2] else j - half), r * hw + j] = 1.0
    return jnp.asarray(p, dtype=BF16)


def _rope_patterns():
    gh = GQA_HD // 4
    gpat = [(j // (2 * gh), j % gh, (j % (2 * gh)) < gh) for j in range(GQA_HD)]
    mh = MLA_ROPE // 4
    mpat = [None] * MLA_NOPE + [(j // (2 * mh), j % mh, (j % (2 * mh)) < mh) for j in range(MLA_ROPE)]
    mpat += [None] * (MLA_PAD - MLA_HD)
    return gpat, gh, mpat, mh


def _rope_tables(n, roped):
    gpat, gh, mpat, mh = _rope_patterns()
    return _axial_tables(n, gpat, gh, GQA_HEADS, roped) + _axial_tables(n, mpat, mh, MLA_HEADS, roped)


def _rope_partners():
    gpat, gh, mpat, mh = _rope_patterns()
    return _partner_matrix(gpat, gh, GQA_HEADS), _partner_matrix(mpat, mh, 2)


def _hyena_consts(n, ch):
    f32 = F32
    tb = min(n, HY_BLOCK)
    k = jnp.arange(tb, dtype=I32)
    ang = jnp.remainder(k[:, None] * k[None, :], 2 * tb).astype(f32) * (math.pi / tb)
    c32, s32 = jnp.cos(ang), jnp.sin(ang)
    c_hi, s_hi = c32.astype(BF16), s32.astype(BF16)
    c_lo, s_lo = (c32 - c_hi.astype(f32)).astype(BF16), (s32 - s_hi.astype(f32)).astype(BF16)
    t = jnp.linspace(0.0, 1.0, n, dtype=f32)[:, None]
    lag = jnp.arange(n, dtype=f32)[:, None]
    bands = jnp.linspace(1e-4, HY_BANDS - 1, HY_BANDS, dtype=f32)[None, :]
    a = (2.0 * math.pi / n) * lag * bands
    feats = jnp.concatenate([t, jnp.cos(a), -jnp.sin(a)], axis=-1)
    feats = jnp.pad(feats, ((0, 0), (0, LANES - feats.shape[1])))
    deltas = jnp.abs(jnp.linspace(math.log(HY_DECAY_TARGET) / HY_SLOW_PCT,
                                  math.log(HY_DECAY_TARGET) / HY_FAST_PCT, ch, dtype=f32))
    decay = jnp.exp(-t * deltas[None, :])
    return {"n": n, "t": tb, "c16": c_hi, "s16": s_hi, "c_hi": c_hi, "c_lo": c_lo, "s_hi": s_hi, "s_lo": s_lo,
            "feats": feats, "decay": decay}


def _seg_matrix(width, seg):
    i = np.arange(width)
    return jnp.asarray((i[:, None] // seg == i[None, :] // seg).astype(np.float32), dtype=BF16)


def _router_split(rw):
    d = rw.shape[0]
    hi = rw.astype(BF16)
    lo = (rw - hi.astype(F32)).astype(BF16)
    top = jnp.concatenate([hi, lo, jnp.zeros((d, LANES - 2 * N_EXPERTS), BF16)], axis=1)
    bot = jnp.concatenate([hi, jnp.zeros((d, LANES - N_EXPERTS), BF16)], axis=1)
    return jnp.concatenate([top, bot], axis=0)


def _layer_params(l, w):
    D = w["w_in"].shape[1]
    wi = w["w_in"][l]
    z = lambda c: jnp.zeros((D, c), F32)
    w_in_r = jnp.concatenate(
        [wi[:, 0:384], z(MLA_NOPE), wi[:, 384:416], z(MLA_PAD - MLA_HD), wi[:, 416:]], axis=1).astype(BF16)
    ukv = w["mla_w_ukv"][l].reshape(-1, MLA_HEADS, MLA_NOPE + MLA_V)
    kpart = jnp.pad(ukv[:, :, :MLA_NOPE], ((0, 0), (0, 0), (0, MLA_PAD - MLA_NOPE)))
    w_ukv_r = jnp.concatenate([kpart.reshape(ukv.shape[0], -1), ukv[:, :, MLA_NOPE:].reshape(ukv.shape[0], -1)],
                              axis=1).astype(BF16)
    uq = w["mla_w_uq"][l].reshape(-1, MLA_HEADS, MLA_HD)
    w_uq_r = jnp.pad(uq, ((0, 0), (0, 0), (0, MLA_PAD - MLA_HD))).reshape(uq.shape[0], -1).astype(BF16)
    padg = lambda g: jnp.tile(jnp.pad(g, (0, MLA_PAD - MLA_HD)), MLA_HEADS)[None, :]
    pw = w["pool_w"][l]
    gc = pw.shape[1]
    pool_bd = jnp.zeros((len(POOL_WINDOWS) * gc,) * 2, F32)
    for gi in range(len(POOL_WINDOWS)):
        pool_bd = pool_bd.at[gi * gc:(gi + 1) * gc, gi * gc:(gi + 1) * gc].set(pw[gi])
    nf = w["hy_f_w1"].shape[1]
    perm_g, perm_m = _rope_partners()
    return {
        "perm_g": perm_g, "perm_m": perm_m,
        "norm1_g": w["norm1_g"][l][None, :], "norm2_g": w["norm2_g"][l][None, :],
        "w_in_r": w_in_r, "w_ukv_r": w_ukv_r, "w_uq_r": w_uq_r,
        "g_k": jnp.tile(w["gqa_knorm_g"][l], GQA_KV_HEADS)[None, :],
        "g_q": jnp.tile(w["gqa_qnorm_g"][l], GQA_HEADS)[None, :],
        "g_ckv": w["mla_ckv_g"][l][None, :], "g_cq": w["mla_cq_g"][l][None, :],
        "g_mk": padg(w["mla_knorm_g"][l]), "g_mq": padg(w["mla_qnorm_g"][l]),
        "seg": _seg_matrix(GQA_HEADS * GQA_HD, GQA_HD),
        "pool_bd": pool_bd.astype(BF16), "pool_scale": w["pool_scale"][l][None, :],
        "hy_conv_w": w["hy_conv_w"][l], "hy_conv_b": w["hy_conv_b"][l][None, :],
        "hy_w1": jnp.pad(w["hy_f_w1"][l], ((0, LANES - nf), (0, 0))), "hy_b1": w["hy_f_b1"][l][None, :],
        "hy_w2": w["hy_f_w2"][l], "hy_b2": w["hy_f_b2"][l][None, :], "hy_w3": w["hy_f_w3"][l],
        "hy_freq": w["hy_freq"][l][None, :], "hy_bias": w["hy_bias"][l][:, None, :],
        "w_out": w["w_out"][l].reshape(4, -1, D).astype(BF16),
        "router_split": _router_split(w["router_w"][l]),
        "w_gate": w["exp_w_gate"][l].astype(BF16), "w_up": w["exp_w_up"][l].astype(BF16),
        "w_down": w["exp_w_down"][l].astype(BF16),
    }


def _moe(xs, sh, sc, lp, consts):
    B, n, _ = xs.shape
    cap = EC_CAPACITY * n // N_EXPERTS
    hcm, idx_t, gate_t = _route_call(xs, sh, sc, lp, consts)
    unfold = lambda a: a[:, :, :SLOT_RADIX].reshape(B, N_EXPERTS, SLOT_RADIX * SLOT_RADIX)[:, :, :cap]
    idx_flat = unfold(idx_t).reshape(-1)
    gts = jnp.pad(jnp.swapaxes(unfold(gate_t), 1, 2), ((0, 0), (0, 0), (0, LANES - N_EXPERTS)))
    return _ffn_call(idx_flat, hcm, gts, lp, cap)


def kernel(x, c, ctx, c_ctx, norm1_g, norm2_g, w_mod, b_mod, w_in, w_out, pool_w, pool_scale, gqa_qnorm_g, gqa_knorm_g, hy_conv_w, hy_conv_b, hy_f_w1, hy_f_b1, hy_f_w2, hy_f_b2, hy_f_w3, hy_freq, hy_bias, mla_cq_g, mla_w_uq, mla_ckv_g, mla_w_ukv, mla_qnorm_g, mla_knorm_g, router_w, exp_w_gate, exp_w_up, exp_w_down):
    w = dict(norm1_g=norm1_g, norm2_g=norm2_g, w_in=w_in, w_out=w_out, pool_w=pool_w, pool_scale=pool_scale,
             gqa_qnorm_g=gqa_qnorm_g, gqa_knorm_g=gqa_knorm_g, hy_conv_w=hy_conv_w, hy_conv_b=hy_conv_b,
             hy_f_w1=hy_f_w1, hy_f_b1=hy_f_b1, hy_f_w2=hy_f_w2, hy_f_b2=hy_f_b2, hy_f_w3=hy_f_w3, hy_freq=hy_freq,
             hy_bias=hy_bias, mla_cq_g=mla_cq_g, mla_w_uq=mla_w_uq, mla_ckv_g=mla_ckv_g, mla_w_ukv=mla_w_ukv,
             mla_qnorm_g=mla_qnorm_g, mla_knorm_g=mla_knorm_g, router_w=router_w, exp_w_gate=exp_w_gate,
             exp_w_up=exp_w_up, exp_w_down=exp_w_down)
    B, n, D = x.shape
    nc = ctx.shape[1]
    depth = w_mod.shape[0]
    ch = hy_bias.shape[2]
    rows = -(-(B + 1) // SUBLANES) * SUBLANES
    cc = jnp.concatenate([c, c_ctx[None, :], jnp.zeros((rows - B - 1, D), F32)], axis=0)
    mod3 = _mod_call(cc, w_mod, b_mod).reshape(depth * rows * MOD_PARTS, 1, D)
    tabs_x = _rope_tables(n, True)
    tabs_c = _rope_tables(nc, False)
    hc_x = _hyena_consts(n, ch)
    hc_c = _hyena_consts(nc, ch)
    tri = _route_consts()
    xc, moe_c = ctx, None
    moe_x = None
    for l in range(depth):
        last = l == depth - 1
        lp = _layer_params(l, w)
        mx = [_mod_vec(mod3, l, rows, i, 0, True) for i in range(MOD_PARTS)]
        mc = [_mod_vec(mod3, l, rows, i, B, False) for i in range(MOD_PARTS)]
        pc = _inproj_call(xc, mc[0], mc[1], lp, tabs_c, last, moe_c)
        kc, vc, mkc, mvc = pc[:4]
        px = _inproj_call(x, mx[0], mx[1], lp, tabs_x, False, moe_x)
        kx, vx, mkx, mvx, qx, mqx, poolx, hyx = px[:8]
        if moe_x is not None:
            x = px[8]
        o_gqa = _attn_call(qx, [kx, kc], [vx, vc], GQA_HEADS, GQA_KV_HEADS, GQA_HD, "attn_gqa")
        o_mla = _attn_call(mqx, [mkx, mkc], [mvx, mvc], MLA_HEADS, MLA_HEADS, MLA_V, "attn_mla")
        o_pool = _pool_call(poolx, lp)
        o_hy = _hyena_call(hyx, lp, hc_x, _hyfilt_call(lp, hc_x))
        x = _outproj_call(x, mx[2], [o_pool, o_gqa, o_hy, o_mla], lp)
        moe_x = (_moe(x, mx[3], mx[4], lp, tri), mx[5])
        if not last:
            qc, mqc, poolc, hyc = pc[4:8]
            if moe_c is not None:
                xc = pc[8]
            oc_gqa = _attn_call(qc, [kc], [vc], GQA_HEADS, GQA_KV_HEADS, GQA_HD, "attn_gqa_ctx")
            oc_mla = _attn_call(mqc, [mkc], [mvc], MLA_HEADS, MLA_HEADS, MLA_V, "attn_mla_ctx")
            oc_pool = _pool_call(poolc, lp)
            oc_hy = _hyena_call(hyc, lp, hc_c, _hyfilt_call(lp, hc_c))
            xc = _outproj_call(xc, mc[2], [oc_pool, oc_gqa, oc_hy, oc_mla], lp)
            moe_c = (_moe(xc, mc[3], mc[4], lp, tri), mc[5])
    return _moe_residual_call(x, moe_x[1], moe_x[0])
```

```python
import functools
import math

import numpy as np
import jax
import jax.numpy as jnp
from jax import lax
from jax.experimental import pallas as pl
from jax.experimental.pallas import tpu as pltpu

F32 = jnp.float32
BF16 = jnp.bfloat16
I32 = jnp.int32

GRID_W = 64
ROPE_THETA = 10000.0
EPS = 1e-6
POOL_WINDOWS = (2, 4, 8, 16)
GQA_HEADS = 4
GQA_KV_HEADS = 2
GQA_HD = 64
HY_BANDS = 8
HY_DECAY_TARGET = 1e-2
HY_FAST_PCT = 0.3
HY_SLOW_PCT = 1.5
MLA_HEADS = 4
MLA_NOPE = 64
MLA_ROPE = 32
MLA_V = 64
MLA_HD = MLA_NOPE + MLA_ROPE
N_EXPERTS = 16
EC_CAPACITY = 2

LANES = 128
SUBLANES = 8
MLA_PAD = LANES
VMEM_LIMIT = 56 * 2 ** 20
HIGHEST = lax.Precision.HIGHEST


def _cp(sem, vmem=VMEM_LIMIT):
    return pltpu.CompilerParams(dimension_semantics=sem, vmem_limit_bytes=vmem)


def _dot(a, b):
    return jnp.dot(a, b, preferred_element_type=F32)


def _dot_nt(a, b):
    return lax.dot_general(a, b, (((1,), (1,)), ((), ())), preferred_element_type=F32)


def _dot_split(x, m):
    hi = x.astype(BF16)
    lo = (x - hi.astype(F32)).astype(BF16)
    return _dot(hi, m) + _dot(lo, m)


def _rms_rows(x):
    return x * lax.rsqrt(jnp.mean(x * x, axis=-1, keepdims=True) + EPS)


def _rope(y, cos, sin_s, perm):
    pw = perm.shape[0]
    yb = y.astype(BF16)
    partner = jnp.concatenate([_dot(yb[:, j:j + pw], perm) for j in range(0, y.shape[1], pw)], axis=1)
    return y * cos + partner * sin_s


def _mod_kernel(c_ref, w_ref, b_ref, o_ref):
    c = c_ref[...]
    s = c * (1.0 / (1.0 + jnp.exp(-c)))
    o_ref[0] = jnp.dot(s, w_ref[0], precision=HIGHEST, preferred_element_type=F32) + b_ref[0]


def _mod_call(cc, w_mod, b_mod):
    L, D, N = w_mod.shape
    R = cc.shape[0]
    tn = 1536
    return pl.pallas_call(
        _mod_kernel,
        grid=(L, N // tn),
        in_specs=[
            pl.BlockSpec((R, D), lambda l, j: (0, 0)),
            pl.BlockSpec((1, D, tn), lambda l, j: (l, 0, j)),
            pl.BlockSpec((1, 1, tn), lambda l, j: (l, 0, j)),
        ],
        out_specs=pl.BlockSpec((1, R, tn), lambda l, j: (l, 0, j)),
        out_shape=jax.ShapeDtypeStruct((L, R, N), F32),
        compiler_params=_cp(("arbitrary", "arbitrary")),
        name="adaln_mod",
    )(cc, w_mod, b_mod.reshape(L, 1, N))


MOD_PARTS = 6
STREAM_ROWS = 2048


def _mod_vec(mod3, layer, rows, part, sample0, per_sample):
    return mod3, (layer * rows + sample0) * MOD_PARTS + part, MOD_PARTS if per_sample else 0


def _mod_spec(mv, batch_axis):
    arr, base, stride = mv
    return pl.BlockSpec((1, 1, arr.shape[2]), lambda *g: (base + stride * g[batch_axis], 0, 0))


KV_COLS = 512
ALL_COLS = 2048
INPROJ_ROWS = 512
INPROJ_SUB_ROWS = 256


def _inproj_kernel(kv_only, with_moe, x_ref, sh_ref, sc_ref, g1_ref, w_ref,
                   gk_ref, gq_ref, gckv_ref, wukv_ref, gmk_ref, gcq_ref, wuq_ref, gmq_ref,
                   seg_ref, pg_ref, pm_ref, cg_ref, sg_ref, cm_ref, sm_ref, *rest):
    if with_moe:
        moe_ref, g2_ref, outs = rest[0], rest[1], rest[2:]
        if not kv_only:
            outs, xo_ref = outs[:-1], outs[-1]
    else:
        outs = rest
    tm = x_ref.shape[1]
    sub = min(tm, INPROJ_SUB_ROWS)
    chunks = x_ref.shape[2] // LANES
    projs = []
    for r0 in range(0, tm, sub):
        x = x_ref[0, r0:r0 + sub, :]
        if with_moe:
            moe = jnp.concatenate([moe_ref[0, pl.ds(r0 * chunks + c, sub, stride=chunks), :] for c in range(chunks)],
                                  axis=1)
            x = x + g2_ref[0] * moe
            if not kv_only:
                xo_ref[0, r0:r0 + sub, :] = x
        h = _rms_rows(x) * g1_ref[...] * (1.0 + sc_ref[0]) + sh_ref[0]
        projs.append((slice(r0, r0 + sub), _dot(h.astype(BF16), w_ref[...])))
    for rows, p in projs:
        _inproj_heads(kv_only, rows, p, gk_ref, gq_ref, gckv_ref, wukv_ref, gmk_ref, gcq_ref, wuq_ref, gmq_ref,
                      seg_ref, pg_ref, pm_ref, cg_ref, sg_ref, cm_ref, sm_ref, *outs)


def _inproj_heads(kv_only, rows, p, gk_ref, gq_ref, gckv_ref, wukv_ref, gmk_ref, gcq_ref, wuq_ref, gmq_ref,
                  seg_ref, pg_ref, pm_ref, cg_ref, sg_ref, cm_ref, sm_ref, *outs):
    if kv_only:
        k_ref, v_ref, mk_ref, mv_ref = outs
    else:
        k_ref, v_ref, mk_ref, mv_ref, q_ref, mq_ref, pool_ref, hy_ref = outs

    def seg_norm(y, seg):
        ss = _dot((y * y).astype(BF16), seg)
        return y * lax.rsqrt(ss * (1.0 / GQA_HD) + EPS)

    def head_norm(y):
        parts = []
        for hh in range(MLA_HEADS):
            s = y[:, hh * MLA_PAD:(hh + 1) * MLA_PAD]
            ms = jnp.sum(s * s, axis=-1, keepdims=True) * (1.0 / MLA_HD)
            parts.append(s * lax.rsqrt(ms + EPS))
        return jnp.concatenate(parts, axis=1)

    def put_heads(ref, y, width):
        for hh in range(ref.shape[1]):
            ref[0, hh, rows, :] = y[:, hh * width:(hh + 1) * width].astype(ref.dtype)

    def put_values_t(ref, v, dv):
        vt = v.T
        dve = dv + ATTN_ONES_ROWS
        for hh in range(v.shape[1] // dv):
            ref[0, hh * dve:hh * dve + dv, rows] = vt[hh * dv:(hh + 1) * dv, :].astype(ref.dtype)
            ref[0, hh * dve + dv:(hh + 1) * dve, rows] = jnp.ones((ATTN_ONES_ROWS, v.shape[0]), ref.dtype)

    cg, sg = cg_ref[rows, :], sg_ref[rows, :]
    cm, sm, pm = cm_ref[rows, :], sm_ref[rows, :], pm_ref[...]
    kk = seg_norm(p[:, 0:128], seg_ref[0:128, 0:128]) * gk_ref[...]
    kk = _rope(kk, cg[:, 0:128], sg[:, 0:128], pg_ref[0:128, 0:128])
    put_heads(k_ref, kk, GQA_HD)
    put_values_t(v_ref, p[:, 128:256], GQA_HD)
    ckv = _rms_rows(p[:, 256:384]) * gckv_ref[...]
    kv = _dot(ckv.astype(BF16), wukv_ref[...])
    kr = p[:, 384:512]
    kraw = kv[:, 0:MLA_HEADS * MLA_PAD] + jnp.concatenate([kr] * MLA_HEADS, axis=1)
    mk = _rope(head_norm(kraw) * gmk_ref[...], cm, sm, pm)
    put_heads(mk_ref, mk, MLA_PAD)
    put_values_t(mv_ref, kv[:, MLA_HEADS * MLA_PAD:], MLA_V)
    if kv_only:
        return
    qq = seg_norm(p[:, 512:768], seg_ref[...]) * gq_ref[...]
    qq = _rope(qq, cg, sg, pg_ref[...])
    put_heads(q_ref, qq * (GQA_HD ** -0.5 * LOG2E), GQA_HD)
    cq = _rms_rows(p[:, 768:1024]) * gcq_ref[...]
    mq = _rope(head_norm(_dot(cq.astype(BF16), wuq_ref[...])) * gmq_ref[...], cm, sm, pm)
    put_heads(mq_ref, mq * (MLA_HD ** -0.5 * LOG2E), MLA_PAD)
    pool_ref[0, rows, :] = p[:, 1024:1280]
    hy_ref[0, rows, :] = p[:, 1280:2048]


def _inproj_call(xs, sh, sc, lp, tabs, kv_only, moe=None):
    B, n, D = xs.shape
    tm = min(n, INPROJ_ROWS)
    cols = KV_COLS if kv_only else ALL_COLS
    w = lp["w_in_r"][:, :cols]
    full = lambda a: pl.BlockSpec(a.shape, lambda i, b: (0,) * a.ndim)
    tab = lambda a: pl.BlockSpec((tm, a.shape[1]), lambda i, b: (i, 0))
    row = lambda width: pl.BlockSpec((1, tm, width), lambda i, b: (b, i, 0))
    params = [lp["g_k"], lp["g_q"], lp["g_ckv"], lp["w_ukv_r"], lp["g_mk"], lp["g_cq"], lp["w_uq_r"], lp["g_mq"],
              lp["seg"], lp["perm_g"], lp["perm_m"]]
    ones = ATTN_ONES_ROWS
    heads = lambda h, d: (jax.ShapeDtypeStruct((B, h, n, d), BF16), pl.BlockSpec((1, h, tm, d), lambda i, b: (b, 0, i, 0)))
    vals_t = lambda h, d: (jax.ShapeDtypeStruct((B, h * (d + ones), n), BF16),
                           pl.BlockSpec((1, h * (d + ones), tm), lambda i, b: (b, 0, i)))
    rows = lambda d: (jax.ShapeDtypeStruct((B, n, d), F32), row(d))
    outs = [heads(GQA_KV_HEADS, GQA_HD), vals_t(GQA_KV_HEADS, GQA_HD), heads(MLA_HEADS, MLA_PAD), vals_t(MLA_HEADS, MLA_V)]
    if not kv_only:
        outs += [heads(GQA_HEADS, GQA_HD), heads(MLA_HEADS, MLA_PAD), rows(256), rows(768)]
    extra_specs, extra_args = [], []
    if moe is not None:
        extra_specs = [pl.BlockSpec((1, tm * (D // LANES), LANES), lambda i, b: (b, i, 0)), _mod_spec(moe[1], 1)]
        extra_args = [moe[0], moe[1][0]]
        if not kv_only:
            outs.append(rows(D))
    return pl.pallas_call(
        functools.partial(_inproj_kernel, kv_only, moe is not None),
        grid=(n // tm, B),
        in_specs=[row(D), _mod_spec(sh, 1), _mod_spec(sc, 1), full(lp["norm1_g"]), full(w)]
        + [full(a) for a in params] + [tab(a) for a in tabs] + extra_specs,
        out_specs=[spec for _, spec in outs],
        out_shape=[shape for shape, _ in outs],
        compiler_params=_cp(("arbitrary", "arbitrary")),
        name="inproj_kv" if kv_only else "inproj",
    )(xs, sh[0], sc[0], lp["norm1_g"], w, *params, *tabs, *extra_args)


ATTN_KEY_BLOCK = 256
ATTN_Q_TILE = 1024
ATTN_ONES_ROWS = 16
LOG2E = math.log2(math.e)


def _attn_kernel(heads, kv_heads, dv, nseg, q_ref, *refs):
    k_refs, vt_refs, o_ref = refs[:nseg], refs[nseg:2 * nseg], refs[2 * nseg]
    grp = heads // kv_heads
    kb = ATTN_KEY_BLOCK
    dve = dv + ATTN_ONES_ROWS
    blocks = [(i, r) for i in range(nseg) for r in range(0, k_refs[i].shape[2], kb)]

    streams = [(bi, hh) for bi in range(q_ref.shape[0]) for hh in range(heads)]
    qs = [q_ref[bi, hh] for bi, hh in streams]
    run_max = [None] * len(streams)
    acc = [None] * len(streams)
    s_prev = [None] * len(streams)
    for j in range(len(blocks) + 1):
        s_cur = [None] * len(streams)
        for st, (bi, hh) in enumerate(streams):
            g = hh // grp
            if j < len(blocks):
                i, r = blocks[j]
                s_cur[st] = _dot_nt(k_refs[i][bi, g, r:r + kb, :], qs[st])
            if j >= 1:
                i, r = blocks[j - 1]
                bm = jnp.max(s_prev[st], axis=0, keepdims=True)
                m_new = bm if run_max[st] is None else jnp.maximum(run_max[st], bm)
                t = _dot(vt_refs[i][bi, g * dve:(g + 1) * dve, r:r + kb], jnp.exp2(s_prev[st] - m_new).astype(BF16))
                acc[st] = t if acc[st] is None else acc[st] * jnp.exp2(run_max[st] - m_new) + t
                run_max[st] = m_new
        s_prev = s_cur
    for bi in range(q_ref.shape[0]):
        outs = [a[0:dv] / a[dv:dv + 1] for (b2, _), a in zip(streams, acc) if b2 == bi]
        o_ref[bi] = jnp.concatenate(outs, axis=0).T.astype(o_ref.dtype)


def _attn_call(q, ks, vts, heads, kv_heads, dv, name):
    B, _, n, dk = q.shape
    tq = min(n, ATTN_Q_TILE)
    bt = min(B, max(1, ATTN_Q_TILE // n))
    kspec = lambda a: pl.BlockSpec((bt,) + a.shape[1:], lambda b, i: (b,) + (0,) * (a.ndim - 1))
    return pl.pallas_call(
        functools.partial(_attn_kernel, heads, kv_heads, dv, len(ks)),
        grid=(B // bt, n // tq),
        in_specs=[pl.BlockSpec((bt, heads, tq, dk), lambda b, i: (b, 0, i, 0))]
        + [kspec(a) for a in ks] + [kspec(a) for a in vts],
        out_specs=pl.BlockSpec((bt, tq, heads * dv), lambda b, i: (b, i, 0)),
        out_shape=jax.ShapeDtypeStruct((B, n, heads * dv), BF16),
        compiler_params=_cp(("arbitrary", "arbitrary")),
        name=name,
    )(q, *ks, *vts)


POOL_HALO = 8
assert all(b == 2 * a for a, b in zip(POOL_WINDOWS, POOL_WINDOWS[1:])) and POOL_WINDOWS[0] == 2
assert POOL_WINDOWS[-1] // 2 <= POOL_HALO


def _pool_kernel(n, u_ref, cnt_ref, wbd_ref, scale_ref, o_ref, ue_ref):
    w = u_ref.shape[2]
    gc = w // len(POOL_WINDOWS)
    ne = n + 2 * POOL_HALO
    u = u_ref[0]
    ue_ref[0:POOL_HALO, :] = jnp.zeros((POOL_HALO, w), F32)
    ue_ref[POOL_HALO:POOL_HALO + n, :] = u
    ue_ref[POOL_HALO + n:ne, :] = jnp.zeros((POOL_HALO, w), F32)
    ue = ue_ref[...]
    down = lambda x, k: pltpu.roll(x, k, axis=0)
    up = lambda x, k: pltpu.roll(x, ne - k, axis=0)
    sums = [down(ue, 1) + ue]
    for win in POOL_WINDOWS[:-1]:
        sums.append(down(sums[-1], win // 2) + up(sums[-1], win // 2))
    lane = lax.broadcasted_iota(I32, (n, w), 1)
    total = sums[-1][POOL_HALO:POOL_HALO + n]
    for gi in range(len(POOL_WINDOWS) - 2, -1, -1):
        total = jnp.where(lane < (gi + 1) * gc, sums[gi][POOL_HALO:POOL_HALO + n], total)
    d = total / cnt_ref[...] - u
    o_ref[0] = (_dot(d.astype(BF16), wbd_ref[...]) * scale_ref[...]).astype(o_ref.dtype)


def _pool_counts(n, width):
    t = np.arange(n)
    gc = width // len(POOL_WINDOWS)
    cols = []
    for win in POOL_WINDOWS:
        left = win // 2
        right = win - 1 - left
        cnt = np.minimum(t + right, n - 1) - np.maximum(t - left, 0) + 1
        cols.append(np.repeat(cnt[:, None], gc, axis=1))
    return jnp.asarray(np.concatenate(cols, axis=1), dtype=F32)


def _pool_call(u, lp):
    B, n, W = u.shape
    return pl.pallas_call(
        functools.partial(_pool_kernel, n),
        grid=(B,),
        in_specs=[pl.BlockSpec((1, n, W), lambda b: (b, 0, 0)),
                  pl.BlockSpec((n, W), lambda b: (0, 0)),
                  pl.BlockSpec((W, W), lambda b: (0, 0)),
                  pl.BlockSpec((1, W), lambda b: (0, 0))],
        out_specs=pl.BlockSpec((1, n, W), lambda b: (b, 0, 0)),
        out_shape=jax.ShapeDtypeStruct((B, n, W), BF16),
        scratch_shapes=[pltpu.VMEM((n + 2 * POOL_HALO, W), F32)],
        compiler_params=_cp(("arbitrary",)),
        name="pool",
    )(u, _pool_counts(n, W), lp["pool_bd"], lp["pool_scale"])


HY_BLOCK = 512
HY_SPEC_DTYPE = F32


def _dot3(a_hi, a_lo, x):
    x_hi = x.astype(BF16)
    x_lo = (x - x_hi.astype(F32)).astype(BF16)
    return _dot(a_hi, x_hi) + _dot(a_hi, x_lo) + _dot(a_lo, x_hi)


def _hyfilt_kernel(n, t, feats_ref, w1_ref, b1_ref, w2_ref, b2_ref, w3_ref, fr_ref, decay_ref,
                   chi_ref, clo_ref, shi_ref, slo_ref, hre_ref, him_ref, hn_ref, tc_ref, ts_ref, edge_ref):
    di = pl.program_id(0)
    nb = n // t
    hc2 = w3_ref.shape[1] // 2

    @pl.when(di == 0)
    def _():
        fr = fr_ref[...]
        hid = jnp.sin(fr * (jnp.dot(feats_ref[...], w1_ref[...], precision=HIGHEST, preferred_element_type=F32)
                            + b1_ref[...]))
        hid = jnp.sin(fr * (jnp.dot(hid, w2_ref[...], precision=HIGHEST, preferred_element_type=F32) + b2_ref[...]))
        h = jnp.dot(hid, w3_ref[...], precision=HIGHEST, preferred_element_type=F32)
        dec = decay_ref[...]
        dec2 = jnp.concatenate([dec, dec], axis=1)
        row = lax.broadcasted_iota(I32, (n, 1), 0)
        fwd = h[:, 0:hc2] * dec2
        bwd = jnp.where(row == 0, 0.0, h[:, hc2:2 * hc2] * dec2)
        scl = lax.rsqrt(jnp.sum(fwd * fwd + bwd * bwd, axis=0, keepdims=True) + EPS)
        alt = jnp.where(lax.broadcasted_iota(I32, (t, 1), 0) % 2 == 0, 1.0, -1.0)
        for fam, filt in enumerate((fwd * scl, bwd * scl)):
            for j in range(nb):
                x = filt[j * t:(j + 1) * t]
                tc_ref[fam, j] = _dot3(chi_ref[...], clo_ref[...], x)
                ts_ref[fam, j] = _dot3(shi_ref[...], slo_ref[...], x)
                r = 2 * (fam * nb + j)
                edge_ref[r:r + 1, :] = x[0:1, :]
                edge_ref[r + 1:r + 2, :] = jnp.sum(x * alt, axis=0, keepdims=True)

    k = lax.broadcasted_iota(I32, (t, 1), 0)
    wk = jnp.where(k == 0, 1.0 / (2 * t), 2.0 / (2 * t))
    sig = jnp.where(k % 2 == 0, 1.0, -1.0)
    first = lambda fam, j: edge_ref[2 * (fam * nb + j):2 * (fam * nb + j) + 1, :]
    altsum = lambda fam, j: edge_ref[2 * (fam * nb + j) + 1:2 * (fam * nb + j) + 2, :]
    for dd in range(2 * nb - 1):
        d = dd - (nb - 1)

        def put(ref, val):
            for o in range(2):
                ref[o, 0] = val[:, o * (hc2 // 2):(o + 1) * (hc2 // 2)].astype(ref.dtype)

        @pl.when(di == dd)
        def _(d=d):
            if d == 0:
                put(hre_ref, wk * (tc_ref[0, 0] + tc_ref[1, 0]))
                put(him_ref, wk * (ts_ref[1, 0] - ts_ref[0, 0]))
                put(hn_ref, (altsum(0, 0) + altsum(1, 0)) * (1.0 / (2 * t)))
            else:
                fam, e, sgn = (0, d, -1.0) if d > 0 else (1, -d, 1.0)
                put(hre_ref, wk * (sig * (tc_ref[fam, e - 1] - first(fam, e - 1)) + tc_ref[fam, e]))
                put(him_ref, (sgn * wk) * (sig * ts_ref[fam, e - 1] + ts_ref[fam, e]))
                put(hn_ref, (altsum(fam, e - 1) - first(fam, e - 1) + altsum(fam, e)) * (1.0 / (2 * t)))


def _hyfilt_call(lp, hc):
    n, t = hc["n"], hc["t"]
    nb = n // t
    nd = 2 * nb - 1
    hc2 = lp["hy_w3"].shape[1] // 2
    ch = hc2 // 2
    full = lambda a: pl.BlockSpec(a.shape, lambda i: (0,) * a.ndim)
    args = [hc["feats"], lp["hy_w1"], lp["hy_b1"], lp["hy_w2"], lp["hy_b2"], lp["hy_w3"], lp["hy_freq"], hc["decay"],
            hc["c_hi"], hc["c_lo"], hc["s_hi"], hc["s_lo"]]
    return pl.pallas_call(
        functools.partial(_hyfilt_kernel, n, t),
        grid=(nd,),
        in_specs=[full(a) for a in args],
        out_specs=[pl.BlockSpec((2, 1, t, ch), lambda i: (0, i, 0, 0)), pl.BlockSpec((2, 1, t, ch), lambda i: (0, i, 0, 0)),
                   pl.BlockSpec((2, 1, 1, ch), lambda i: (0, i, 0, 0))],
        out_shape=[jax.ShapeDtypeStruct((2, nd, t, ch), HY_SPEC_DTYPE), jax.ShapeDtypeStruct((2, nd, t, ch), HY_SPEC_DTYPE),
                   jax.ShapeDtypeStruct((2, nd, 1, ch), F32)],
        scratch_shapes=[pltpu.VMEM((2, nb, t, hc2), F32), pltpu.VMEM((2, nb, t, hc2), F32),
                        pltpu.VMEM((4 * nb, hc2), F32)],
        compiler_params=_cp(("arbitrary",)),
        name="hyena_filter",
    )(*args)


HY_FREQ_ROWS = 32


def _hyena_kernel(n, t, u_ref, cw_ref, cb_ref, c_ref, s_ref, hre_ref, him_ref, hn_ref, bias_ref, o_ref,
                  uc_ref, z_ref, a_ref, b_ref, p_ref, q_ref):
    ch = o_ref.shape[2]
    nb = n // t
    row = lax.broadcasted_iota(I32, (t, 1), 0)
    alt = jnp.where(row % 2 == 0, 1.0, -1.0)
    sub = lax.broadcasted_iota(I32, (SUBLANES, 1), 0)
    for g in range(3):
        cols = slice(g * ch, (g + 1) * ch)
        for j in range(nb):
            r0 = j * t
            u = u_ref[0, r0:r0 + t, cols]
            prev = jnp.zeros((1, ch), F32) if j == 0 else u_ref[0, r0 - 1:r0, cols]
            nxt = jnp.zeros((1, ch), F32) if j == nb - 1 else u_ref[0, r0 + t:r0 + t + 1, cols]
            up = pltpu.roll(u, 1, axis=0)
            un = pltpu.roll(u, t - 1, axis=0)
            up = jnp.concatenate([jnp.where(sub == 0, prev, up[0:SUBLANES]), up[SUBLANES:]], axis=0)
            un = jnp.concatenate([un[:t - SUBLANES], jnp.where(sub == SUBLANES - 1, nxt, un[t - SUBLANES:])], axis=0)
            uc_ref[g, r0:r0 + t, :] = (up * cw_ref[0:1, cols] + u * cw_ref[1:2, cols] + un * cw_ref[2:3, cols]
                                       + cb_ref[:, cols])

    z_ref[...] = uc_ref[0]

    def long_conv(o, carry):
        nyq = []
        for j in range(nb):
            z = z_ref[j * t:(j + 1) * t, :]
            zb = z.astype(BF16)
            a_ref[j] = _dot(c_ref[...], zb).astype(a_ref.dtype)
            b_ref[j] = _dot(s_ref[...], zb).astype(b_ref.dtype)
            nyq.append(jnp.sum(z * alt, axis=0, keepdims=True))
        for r0 in range(0, t, HY_FREQ_ROWS):
            rows = slice(r0, r0 + HY_FREQ_ROWS)
            ab = [(a_ref[j, rows, :], b_ref[j, rows, :]) for j in range(nb)]
            for i in range(nb):
                p = q = None
                for j, (a, b) in enumerate(ab):
                    d = i - j + nb - 1
                    hre, him = hre_ref[o, d, rows, :], him_ref[o, d, rows, :]
                    tp = a * hre + b * him
                    tq = b * hre - a * him
                    p = tp if p is None else p + tp
                    q = tq if q is None else q + tq
                p_ref[i, rows, :] = p.astype(BF16)
                q_ref[i, rows, :] = q.astype(BF16)
        for i in range(nb):
            yn = functools.reduce(jnp.add, [nyq[j] * hn_ref[o, i - j + nb - 1] for j in range(nb)])
            y = _dot(c_ref[...], p_ref[i]) + _dot(s_ref[...], q_ref[i])
            blk = slice(i * t, (i + 1) * t)
            z_ref[blk, :] = uc_ref[o + 1, blk, :] * (y + alt * yn + z_ref[blk, :] * bias_ref[o])
        return carry

    lax.fori_loop(0, 2, long_conv, 0)
    o_ref[0] = z_ref[...].astype(o_ref.dtype)


def _hyena_call(u, lp, hc, filt):
    B, n, W = u.shape
    ch = W // 3
    t = hc["t"]
    hre, him, hn = filt
    once = lambda a: pl.BlockSpec(a.shape, lambda b: (0,) * a.ndim, pipeline_mode=pl.Buffered(1))
    return pl.pallas_call(
        functools.partial(_hyena_kernel, n, t),
        scratch_shapes=[pltpu.VMEM((3, n, ch), F32), pltpu.VMEM((n, ch), F32),
                        pltpu.VMEM((n // t, t, ch), HY_SPEC_DTYPE), pltpu.VMEM((n // t, t, ch), HY_SPEC_DTYPE),
                        pltpu.VMEM((n // t, t, ch), BF16), pltpu.VMEM((n // t, t, ch), BF16)],
        grid=(B,),
        in_specs=[pl.BlockSpec((1, n, W), lambda b: (b, 0, 0)),
                  once(lp["hy_conv_w"]), once(lp["hy_conv_b"]), once(hc["c16"]), once(hc["s16"]),
                  once(hre), once(him), once(hn), once(lp["hy_bias"])],
        out_specs=pl.BlockSpec((1, n, ch), lambda b: (b, 0, 0)),
        out_shape=jax.ShapeDtypeStruct((B, n, ch), BF16),
        compiler_params=_cp(("arbitrary",)),
        name="hyena",
    )(u, lp["hy_conv_w"], lp["hy_conv_b"], hc["c16"], hc["s16"], hre, him, hn, lp["hy_bias"])


def _outproj_kernel(x_ref, g_ref, a_ref, b_ref, c_ref, d_ref, w_ref, o_ref):
    mixed = (_dot(a_ref[0], w_ref[0]) + _dot(b_ref[0], w_ref[1])
             + _dot(c_ref[0], w_ref[2]) + _dot(d_ref[0], w_ref[3]))
    o_ref[0] = x_ref[0] + g_ref[0] * mixed


def _outproj_call(xs, gate, parts, lp):
    B, n, D = xs.shape
    tm = min(n, STREAM_ROWS)
    gw = parts[0].shape[2]
    return pl.pallas_call(
        _outproj_kernel,
        grid=(B, n // tm),
        in_specs=[pl.BlockSpec((1, tm, D), lambda b, i: (b, i, 0)), _mod_spec(gate, 0)]
        + [pl.BlockSpec((1, tm, gw), lambda b, i: (b, i, 0))] * 4
        + [pl.BlockSpec(lp["w_out"].shape, lambda b, i: (0, 0, 0))],
        out_specs=pl.BlockSpec((1, tm, D), lambda b, i: (b, i, 0)),
        out_shape=jax.ShapeDtypeStruct((B, n, D), F32),
        compiler_params=_cp(("arbitrary", "arbitrary")),
        name="outproj",
    )(xs, gate[0], *parts, lp["w_out"])


PREFIX_BLOCK = 256


def _prefix_lanes(m, tri):
    e, n = m.shape
    nb = n // PREFIX_BLOCK
    stk = jnp.concatenate([m[:, j * PREFIX_BLOCK:(j + 1) * PREFIX_BLOCK] for j in range(nb)], axis=0)
    pre = _dot(stk.astype(BF16), tri)
    outs = []
    carry = jnp.zeros((e, 1), F32)
    for j in range(nb):
        pj = pre[j * e:(j + 1) * e]
        outs.append(pj + carry)
        carry = carry + pj[:, PREFIX_BLOCK - 1:PREFIX_BLOCK]
    return jnp.concatenate(outs, axis=1)


SLOT_RADIX = 16
RADIX_BITS = 4


def _split3(x):
    a = x.astype(BF16).astype(F32)
    r = x - a
    b = r.astype(BF16).astype(F32)
    c = (r - b).astype(BF16).astype(F32)
    return a, b, c


def _digit_onehots(v, weights=None):
    e_n = v.shape[0]
    hi_d = jnp.floor(v * (1.0 / SLOT_RADIX))
    lo_d = v - SLOT_RADIX * hi_d
    dig = lax.broadcasted_iota(I32, (SLOT_RADIX, v.shape[1]), 0).astype(F32)
    his = [hi_d[e:e + 1, :] == dig for e in range(e_n)]
    lo = jnp.concatenate([jnp.where(lo_d[e:e + 1, :] == dig, 1.0, 0.0) for e in range(e_n)], axis=0)
    if weights is None:
        hi = jnp.concatenate([jnp.where(m, 1.0, 0.0) for m in his], axis=0)
    else:
        hi = jnp.concatenate([jnp.where(m, w[e:e + 1, :], 0.0) for w in weights for e, m in enumerate(his)], axis=0)
    return hi.astype(BF16), lo.astype(BF16)


def _route_kernel(n, cap, x_ref, sh_ref, sc_ref, g_ref, wr_ref, tri_ref, fold_ref, pre_ref, low_ref,
                  hcm_ref, idx_ref, gate_ref):
    x = x_ref[0]
    h = _rms_rows(x) * g_ref[...] * (1.0 + sc_ref[0]) + sh_ref[0]
    for c in range(h.shape[1] // LANES):
        hcm_ref[0, pl.ds(c, n, stride=SUBLANES), :] = h[:, c * LANES:(c + 1) * LANES]
    h_hi = h.astype(BF16)
    h_lo = (h - h_hi.astype(F32)).astype(BF16)
    rt = _dot(jnp.concatenate([h_hi, h_lo], axis=1), wr_ref[...]).T
    lt = rt[0:N_EXPERTS, :] + rt[N_EXPERTS:2 * N_EXPERTS, :]
    ex = jnp.exp(lt - jnp.max(lt, axis=0, keepdims=True))
    aff = ex / jnp.sum(ex, axis=0, keepdims=True)
    bits = pltpu.bitcast(aff, I32)
    thr = jnp.zeros((N_EXPERTS, 1), I32)
    for shift in range(32 - RADIX_BITS, -1, -RADIX_BITS):
        top = (1 << (31 - shift)) if shift + RADIX_BITS > 31 else (1 << RADIX_BITS)
        digit = jnp.zeros((N_EXPERTS, 1), I32)
        for c in range(1, top):
            cnt = jnp.sum(jnp.where(bits >= (thr | (c << shift)), 1.0, 0.0), axis=1, keepdims=True)
            digit = digit + jnp.where(cnt >= cap, 1, 0)
        thr = thr | (digit << shift)
    gt = jnp.where(bits > thr, 1.0, 0.0)
    eq = jnp.where(bits == thr, 1.0, 0.0)
    need = cap - jnp.sum(gt, axis=1, keepdims=True)
    tri = tri_ref[...]
    sel = gt + eq * jnp.where(_prefix_lanes(eq, tri) <= need, 1.0, 0.0)
    incl = _prefix_lanes(sel, tri)
    rows = N_EXPERTS * SLOT_RADIX
    shift = SLOT_RADIX.bit_length() - 1
    same = ((lax.broadcasted_iota(I32, (rows, rows), 0) >> shift)
            == (lax.broadcasted_iota(I32, (rows, rows), 1) >> shift))
    fold = fold_ref[...]
    a_hi, a_lo = _digit_onehots(incl)
    hist = jnp.where(same, _dot_nt(a_hi, a_lo), 0.0)
    hist = _dot_split(hist, fold)
    pre = _dot_split(hist, pre_ref[...])
    tot = jnp.broadcast_to(pre[:, SLOT_RADIX - 1:SLOT_RADIX], pre.shape)
    t_hi = tot.astype(BF16)
    t_lo = (tot - t_hi.astype(F32)).astype(BF16)
    idx_ref[0] = (pre + _dot(low_ref[...], t_hi) + _dot(low_ref[...], t_lo)).astype(I32)
    g_hi, g_lo = _digit_onehots(incl - 1.0, _split3(sel * aff))
    gsum = _dot_nt(g_hi, g_lo)
    gsum = jnp.where(same, gsum[0:rows] + gsum[rows:2 * rows] + gsum[2 * rows:3 * rows], 0.0)
    gate_ref[0] = functools.reduce(jnp.add, [_dot(part.astype(BF16), fold) for part in _split3(gsum)])


def _route_consts():
    rows = N_EXPERTS * SLOT_RADIX
    i = np.arange(rows)
    j = np.arange(LANES)
    fold = (i[:, None] % SLOT_RADIX == j[None, :]).astype(np.float32)
    pre = ((j[:, None] <= j[None, :]) & (j[None, :] < SLOT_RADIX)).astype(np.float32)
    low = ((i[:, None] // SLOT_RADIX == i[None, :] // SLOT_RADIX) & (i[None, :] < i[:, None])).astype(np.float32)
    tri = np.triu(np.ones((PREFIX_BLOCK, PREFIX_BLOCK), np.float32))
    return tuple(jnp.asarray(a, dtype=BF16) for a in (tri, fold, pre, low))


def _route_call(xs, sh, sc, lp, consts):
    B, n, D = xs.shape
    cap = EC_CAPACITY * n // N_EXPERTS
    assert cap <= SLOT_RADIX * SLOT_RADIX
    rows = n * (D // LANES)
    srows = N_EXPERTS * SLOT_RADIX
    full = lambda a: pl.BlockSpec(a.shape, lambda b: (0,) * a.ndim)
    return pl.pallas_call(
        functools.partial(_route_kernel, n, cap),
        grid=(B,),
        in_specs=[pl.BlockSpec((1, n, D), lambda b: (b, 0, 0)), _mod_spec(sh, 0), _mod_spec(sc, 0),
                  full(lp["norm2_g"]), full(lp["router_split"])] + [full(a) for a in consts],
        out_specs=[pl.BlockSpec((1, rows, LANES), lambda b: (b, 0, 0)),
                   pl.BlockSpec((1, srows, LANES), lambda b: (b, 0, 0)),
                   pl.BlockSpec((1, srows, LANES), lambda b: (b, 0, 0))],
        out_shape=[jax.ShapeDtypeStruct((B, rows, LANES), F32),
                   jax.ShapeDtypeStruct((B, srows, LANES), I32),
                   jax.ShapeDtypeStruct((B, srows, LANES), F32)],
        compiler_params=_cp(("arbitrary",)),
        name="route",
    )(xs, sh[0], sc[0], lp["norm2_g"], lp["router_split"], *consts)


SCATTER_GROUP = 4


FFN_ROWS = 256


def _ffn_kernel(cap, bt, idx_ref, h_ref, gate_ref, wg_ref, wu_ref, wd_ref, o_ref, xga_ref, xgb_ref, ya_ref, yb_ref):
    b0 = pl.program_id(0) * bt
    k = pl.program_id(1)
    steps = pl.num_programs(1)
    chunks = wg_ref.shape[1] // LANES
    m = bt * cap
    tile = lambda r: pl.ds(pl.multiple_of(r * SUBLANES, SUBLANES), SUBLANES)
    e0 = 2 * k
    e1 = e0 + 1
    wrap = lambda e: jnp.where(e < 0, e + N_EXPERTS, jnp.where(e >= N_EXPERTS, e - N_EXPERTS, e))

    def gather(e, xg_ref):
        for bi in range(bt):
            base = ((b0 + bi) * N_EXPERTS + e) * cap
            for r in range(cap):
                xg_ref[tile(bi * cap + r), :] = h_ref[bi, tile(idx_ref[base + r]), :]

    def scatter(e, y_ref):
        for bi in range(bt):
            base = ((b0 + bi) * N_EXPERTS + e) * cap
            for r0 in range(0, cap, SCATTER_GROUP):
                upd = []
                for r in range(r0, r0 + SCATTER_GROUP):
                    dst = tile(idx_ref[base + r])
                    upd.append((dst, o_ref[bi, dst, :] + y_ref[tile(bi * cap + r), :]))
                for dst, v in upd:
                    o_ref[bi, dst, :] = v

    def ffn(e, w, xg_ref, y_ref):
        xt = jnp.concatenate([xg_ref[pl.ds(c, m, stride=SUBLANES), :] for c in range(chunks)], axis=1).astype(BF16)
        a = _dot(xt, wg_ref[w])
        u = _dot(xt, wu_ref[w])
        hmid = (a * (1.0 / (1.0 + jnp.exp(-a))) * u).astype(BF16)
        lane = lax.broadcasted_iota(I32, (m, LANES), 1)
        gcol = jnp.sum(jnp.where(lane == e, gate_ref[...].reshape(m, LANES), 0.0), axis=1, keepdims=True)
        y = _dot(hmid, wd_ref[w]) * gcol
        for c in range(chunks):
            y_ref[pl.ds(c, m, stride=SUBLANES), :] = y[:, c * LANES:(c + 1) * LANES]

    @pl.when(k == 0)
    def _():
        o_ref[...] = jnp.zeros_like(o_ref)
        yb_ref[...] = jnp.zeros_like(yb_ref)
        gather(e0, xga_ref)

    ffn(e0, 0, xga_ref, ya_ref)
    gather(e1, xgb_ref)
    scatter(wrap(e0 - 1), yb_ref)
    ffn(e1, 1, xgb_ref, yb_ref)
    gather(wrap(e1 + 1), xga_ref)
    scatter(e0, ya_ref)

    @pl.when(k == steps - 1)
    def _():
        scatter(e1, yb_ref)


def _ffn_call(idx_flat, hcm, gate, lp, cap):
    B, rows, _ = hcm.shape
    D, ff = lp["w_gate"].shape[1:]
    bt = min(B, max(1, FFN_ROWS // cap))
    m = bt * cap
    return pl.pallas_call(
        functools.partial(_ffn_kernel, cap, bt),
        grid=(B // bt, N_EXPERTS // 2),
        in_specs=[pl.BlockSpec(memory_space=pltpu.SMEM),
                  pl.BlockSpec((bt, rows, LANES), lambda b, k: (b, 0, 0)),
                  pl.BlockSpec((bt, cap, LANES), lambda b, k: (b, 0, 0)),
                  pl.BlockSpec((2, D, ff), lambda b, k: (k, 0, 0)),
                  pl.BlockSpec((2, D, ff), lambda b, k: (k, 0, 0)),
                  pl.BlockSpec((2, ff, D), lambda b, k: (k, 0, 0))],
        out_specs=pl.BlockSpec((bt, rows, LANES), lambda b, k: (b, 0, 0)),
        out_shape=jax.ShapeDtypeStruct((B, rows, LANES), F32),
        scratch_shapes=[pltpu.VMEM((m * SUBLANES, LANES), F32)] * 4,
        compiler_params=_cp(("arbitrary", "arbitrary")),
        name="expert_ffn",
    )(idx_flat, hcm, gate, lp["w_gate"], lp["w_up"], lp["w_down"])


def _moe_residual_kernel(x_ref, g_ref, m_ref, o_ref):
    tm = x_ref.shape[1]
    chunks = x_ref.shape[2] // LANES
    moe = jnp.concatenate([m_ref[0, pl.ds(c, tm, stride=SUBLANES), :] for c in range(chunks)], axis=1)
    o_ref[0] = x_ref[0] + g_ref[0] * moe


def _moe_residual_call(xs, gate, moe_cm):
    B, n, D = xs.shape
    tm = min(n, STREAM_ROWS)
    return pl.pallas_call(
        _moe_residual_kernel,
        grid=(B, n // tm),
        in_specs=[pl.BlockSpec((1, tm, D), lambda b, i: (b, i, 0)), _mod_spec(gate, 0),
                  pl.BlockSpec((1, tm * (D // LANES), LANES), lambda b, i: (b, i, 0))],
        out_specs=pl.BlockSpec((1, tm, D), lambda b, i: (b, i, 0)),
        out_shape=jax.ShapeDtypeStruct((B, n, D), F32),
        compiler_params=_cp(("arbitrary", "arbitrary")),
        name="moe_residual",
    )(xs, gate[0], moe_cm)


def _axial_tables(n, head_pat, half, reps, roped):
    width = reps * len(head_pat)
    if not roped:
        return jnp.ones((n, width), F32), jnp.zeros((n, width), F32)
    pats = list(head_pat) * reps
    on_col = np.array([p is not None and p[0] == 1 for p in pats])
    freq = np.array([0 if p is None else p[1] for p in pats])
    sign = np.array([0.0 if p is None else (-1.0 if p[2] else 1.0) for p in pats], np.float32)
    t = jnp.arange(n)
    row = jnp.floor_divide(t, GRID_W).astype(F32)[:, None]
    col = jnp.remainder(t, GRID_W).astype(F32)[:, None]
    inv = (ROPE_THETA ** (-jnp.arange(half, dtype=F32) / half))[freq][None, :]
    ang = jnp.where(on_col[None, :], col, row) * inv
    return jnp.where(sign[None, :] == 0.0, 1.0, jnp.cos(ang)), sign[None, :] * jnp.sin(ang)


def _partner_matrix(head_pat, half, reps):
    hw = len(head_pat)
    p = np.zeros((hw * reps, hw * reps), np.float32)
    for r in range(reps):
        for j, pat in enumerate(head_pat):
            if pat is not None:
                p[r * hw + (j + half if pat[2] else j - half), r * hw + j] = 1.0
    return jnp.asarray(p, dtype=BF16)


def _rope_patterns():
    gh = GQA_HD // 4
    gpat = [(j // (2 * gh), j % gh, (j % (2 * gh)) < gh) for j in range(GQA_HD)]
    mh = MLA_ROPE // 4
    mpat = [None] * MLA_NOPE + [(j // (2 * mh), j % mh, (j % (2 * mh)) < mh) for j in range(MLA_ROPE)]
    mpat += [None] * (MLA_PAD - MLA_HD)
    return gpat, gh, mpat, mh


def _rope_tables(n, roped):
    gpat, gh, mpat, mh = _rope_patterns()
    return _axial_tables(n, gpat, gh, GQA_HEADS, roped) + _axial_tables(n, mpat, mh, MLA_HEADS, roped)


def _rope_partners():
    gpat, gh, mpat, mh = _rope_patterns()
    return _partner_matrix(gpat, gh, GQA_HEADS), _partner_matrix(mpat, mh, 2)


def _hyena_consts(n, ch):
    f32 = F32
    tb = min(n, HY_BLOCK)
    k = jnp.arange(tb, dtype=I32)
    ang = jnp.remainder(k[:, None] * k[None, :], 2 * tb).astype(f32) * (math.pi / tb)
    c32, s32 = jnp.cos(ang), jnp.sin(ang)
    c_hi, s_hi = c32.astype(BF16), s32.astype(BF16)
    c_lo, s_lo = (c32 - c_hi.astype(f32)).astype(BF16), (s32 - s_hi.astype(f32)).astype(BF16)
    t = jnp.linspace(0.0, 1.0, n, dtype=f32)[:, None]
    lag = jnp.arange(n, dtype=f32)[:, None]
    bands = jnp.linspace(1e-4, HY_BANDS - 1, HY_BANDS, dtype=f32)[None, :]
    a = (2.0 * math.pi / n) * lag * bands
    feats = jnp.concatenate([t, jnp.cos(a), -jnp.sin(a)], axis=-1)
    feats = jnp.pad(feats, ((0, 0), (0, LANES - feats.shape[1])))
    deltas = jnp.abs(jnp.linspace(math.log(HY_DECAY_TARGET) / HY_SLOW_PCT,
                                  math.log(HY_DECAY_TARGET) / HY_FAST_PCT, ch, dtype=f32))
    decay = jnp.exp(-t * deltas[None, :])
    return {"n": n, "t": tb, "c16": c_hi, "s16": s_hi, "c_hi": c_hi, "c_lo": c_lo, "s_hi": s_hi, "s_lo": s_lo,
            "feats": feats, "decay": decay}


def _seg_matrix(width, seg):
    i = np.arange(width)
    return jnp.asarray((i[:, None] // seg == i[None, :] // seg).astype(np.float32), dtype=BF16)


def _router_split(rw):
    d = rw.shape[0]
    hi = rw.astype(BF16)
    lo = (rw - hi.astype(F32)).astype(BF16)
    top = jnp.concatenate([hi, lo, jnp.zeros((d, LANES - 2 * N_EXPERTS), BF16)], axis=1)
    bot = jnp.concatenate([hi, jnp.zeros((d, LANES - N_EXPERTS), BF16)], axis=1)
    return jnp.concatenate([top, bot], axis=0)


def _layer_params(l, w):
    D = w["w_in"].shape[1]
    wi = w["w_in"][l]
    z = lambda c: jnp.zeros((D, c), F32)
    w_in_r = jnp.concatenate(
        [wi[:, 0:384], z(MLA_NOPE), wi[:, 384:416], z(MLA_PAD - MLA_HD), wi[:, 416:]], axis=1).astype(BF16)
    ukv = w["mla_w_ukv"][l].reshape(-1, MLA_HEADS, MLA_NOPE + MLA_V)
    kpart = jnp.pad(ukv[:, :, :MLA_NOPE], ((0, 0), (0, 0), (0, MLA_PAD - MLA_NOPE)))
    w_ukv_r = jnp.concatenate([kpart.reshape(ukv.shape[0], -1), ukv[:, :, MLA_NOPE:].reshape(ukv.shape[0], -1)],
                              axis=1).astype(BF16)
    uq = w["mla_w_uq"][l].reshape(-1, MLA_HEADS, MLA_HD)
    w_uq_r = jnp.pad(uq, ((0, 0), (0, 0), (0, MLA_PAD - MLA_HD))).reshape(uq.shape[0], -1).astype(BF16)
    padg = lambda g: jnp.tile(jnp.pad(g, (0, MLA_PAD - MLA_HD)), MLA_HEADS)[None, :]
    pw = w["pool_w"][l]
    gc = pw.shape[1]
    pool_bd = jnp.zeros((len(POOL_WINDOWS) * gc,) * 2, F32)
    for gi in range(len(POOL_WINDOWS)):
        pool_bd = pool_bd.at[gi * gc:(gi + 1) * gc, gi * gc:(gi + 1) * gc].set(pw[gi])
    nf = w["hy_f_w1"].shape[1]
    perm_g, perm_m = _rope_partners()
    return {
        "perm_g": perm_g, "perm_m": perm_m,
        "norm1_g": w["norm1_g"][l][None, :], "norm2_g": w["norm2_g"][l][None, :],
        "w_in_r": w_in_r, "w_ukv_r": w_ukv_r, "w_uq_r": w_uq_r,
        "g_k": jnp.tile(w["gqa_knorm_g"][l], GQA_KV_HEADS)[None, :],
        "g_q": jnp.tile(w["gqa_qnorm_g"][l], GQA_HEADS)[None, :],
        "g_ckv": w["mla_ckv_g"][l][None, :], "g_cq": w["mla_cq_g"][l][None, :],
        "g_mk": padg(w["mla_knorm_g"][l]), "g_mq": padg(w["mla_qnorm_g"][l]),
        "seg": _seg_matrix(GQA_HEADS * GQA_HD, GQA_HD),
        "pool_bd": pool_bd.astype(BF16), "pool_scale": w["pool_scale"][l][None, :],
        "hy_conv_w": w["hy_conv_w"][l], "hy_conv_b": w["hy_conv_b"][l][None, :],
        "hy_w1": jnp.pad(w["hy_f_w1"][l], ((0, LANES - nf), (0, 0))), "hy_b1": w["hy_f_b1"][l][None, :],
        "hy_w2": w["hy_f_w2"][l], "hy_b2": w["hy_f_b2"][l][None, :], "hy_w3": w["hy_f_w3"][l],
        "hy_freq": w["hy_freq"][l][None, :], "hy_bias": w["hy_bias"][l][:, None, :],
        "w_out": w["w_out"][l].reshape(4, -1, D).astype(BF16),
        "router_split": _router_split(w["router_w"][l]),
        "w_gate": w["exp_w_gate"][l].astype(BF16), "w_up": w["exp_w_up"][l].astype(BF16),
        "w_down": w["exp_w_down"][l].astype(BF16),
    }


def _moe(xs, sh, sc, lp, consts):
    B, n, _ = xs.shape
    cap = EC_CAPACITY * n // N_EXPERTS
    hcm, idx_t, gate_t = _route_call(xs, sh, sc, lp, consts)
    unfold = lambda a: a[:, :, :SLOT_RADIX].reshape(B, N_EXPERTS, SLOT_RADIX * SLOT_RADIX)[:, :, :cap]
    idx_flat = unfold(idx_t).reshape(-1)
    gts = jnp.pad(jnp.swapaxes(unfold(gate_t), 1, 2), ((0, 0), (0, 0), (0, LANES - N_EXPERTS)))
    return _ffn_call(idx_flat, hcm, gts, lp, cap)


def kernel(x, c, ctx, c_ctx, norm1_g, norm2_g, w_mod, b_mod, w_in, w_out, pool_w, pool_scale, gqa_qnorm_g, gqa_knorm_g, hy_conv_w, hy_conv_b, hy_f_w1, hy_f_b1, hy_f_w2, hy_f_b2, hy_f_w3, hy_freq, hy_bias, mla_cq_g, mla_w_uq, mla_ckv_g, mla_w_ukv, mla_qnorm_g, mla_knorm_g, router_w, exp_w_gate, exp_w_up, exp_w_down):
    w = dict(norm1_g=norm1_g, norm2_g=norm2_g, w_in=w_in, w_out=w_out, pool_w=pool_w, pool_scale=pool_scale,
             gqa_qnorm_g=gqa_qnorm_g, gqa_knorm_g=gqa_knorm_g, hy_conv_w=hy_conv_w, hy_conv_b=hy_conv_b,
             hy_f_w1=hy_f_w1, hy_f_b1=hy_f_b1, hy_f_w2=hy_f_w2, hy_f_b2=hy_f_b2, hy_f_w3=hy_f_w3, hy_freq=hy_freq,
             hy_bias=hy_bias, mla_cq_g=mla_cq_g, mla_w_uq=mla_w_uq, mla_ckv_g=mla_ckv_g, mla_w_ukv=mla_w_ukv,
             mla_qnorm_g=mla_qnorm_g, mla_knorm_g=mla_knorm_g, router_w=router_w, exp_w_gate=exp_w_gate,
             exp_w_up=exp_w_up, exp_w_down=exp_w_down)
    B, n, D = x.shape
    nc = ctx.shape[1]
    depth = w_mod.shape[0]
    ch = hy_bias.shape[2]
    rows = -(-(B + 1) // SUBLANES) * SUBLANES
    cc = jnp.concatenate([c, c_ctx[None, :], jnp.zeros((rows - B - 1, D), F32)], axis=0)
    mod3 = _mod_call(cc, w_mod, b_mod).reshape(depth * rows * MOD_PARTS, 1, D)
    tabs_x = _rope_tables(n, True)
    tabs_c = _rope_tables(nc, False)
    hc_x = _hyena_consts(n, ch)
    hc_c = _hyena_consts(nc, ch)
    tri = _route_consts()
    xc, moe_c = ctx, None
    moe_x = None
    for l in range(depth):
        last = l == depth - 1
        lp = _layer_params(l, w)
        mx = [_mod_vec(mod3, l, rows, i, 0, True) for i in range(MOD_PARTS)]
        mc = [_mod_vec(mod3, l, rows, i, B, False) for i in range(MOD_PARTS)]
        pc = _inproj_call(xc, mc[0], mc[1], lp, tabs_c, last, moe_c)
        kc, vc, mkc, mvc = pc[:4]
        px = _inproj_call(x, mx[0], mx[1], lp, tabs_x, False, moe_x)
        kx, vx, mkx, mvx, qx, mqx, poolx, hyx = px[:8]
        if moe_x is not None:
            x = px[8]
        o_gqa = _attn_call(qx, [kx, kc], [vx, vc], GQA_HEADS, GQA_KV_HEADS, GQA_HD, "attn_gqa")
        o_mla = _attn_call(mqx, [mkx, mkc], [mvx, mvc], MLA_HEADS, MLA_HEADS, MLA_V, "attn_mla")
        o_pool = _pool_call(poolx, lp)
        o_hy = _hyena_call(hyx, lp, hc_x, _hyfilt_call(lp, hc_x))
        x = _outproj_call(x, mx[2], [o_pool, o_gqa, o_hy, o_mla], lp)
        moe_x = (_moe(x, mx[3], mx[4], lp, tri), mx[5])
        if not last:
            qc, mqc, poolc, hyc = pc[4:8]
            if moe_c is not None:
                xc = pc[8]
            oc_gqa = _attn_call(qc, [kc], [vc], GQA_HEADS, GQA_KV_HEADS, GQA_HD, "attn_gqa_ctx")
            oc_mla = _attn_call(mqc, [mkc], [mvc], MLA_HEADS, MLA_HEADS, MLA_V, "attn_mla_ctx")
            oc_pool = _pool_call(poolc, lp)
            oc_hy = _hyena_call(hyc, lp, hc_c, _hyfilt_call(lp, hc_c))
            xc = _outproj_call(xc, mc[2], [oc_pool, oc_gqa, oc_hy, oc_mla], lp)
            moe_c = (_moe(xc, mc[3], mc[4], lp, tri), mc[5])
    return _moe_residual_call(x, moe_x[1], moe_x[0])
```
